```python
import math
import jax, jax.numpy as jnp
from jax import lax
import numpy as np

D_MODEL = 1024
BATCH = 2
SEQ = 8192
DEPTH = 2
DEC_BATCH = 128
DEC_SEQ = 4
PAST_LEN = 8192
PAGE_SIZE = 128

N_EVEN = (DEPTH + 1) // 2
N_ODD = DEPTH // 2
D_FF = 2816
RMS_EPS = 1e-6
ROPE_THETA = 10000.0
Q_BLOCK = 128
NEG_INF = -1e30
GLA_HEADS = 4
GLA_DK = 64
GLA_DV = 128
GLA_GATE_RANK = 16
GLA_TAU = 16.0
GLA_CHUNK = 64
MLA_HEADS = 4
MLA_Q_LORA = 384
MLA_KV_LORA = 256
MLA_NOPE = 128
MLA_ROPE = 64
MLA_V = 128
MOBA_HEADS = 4
MOBA_HD = 128
MOBA_BLOCK = 256
MOBA_TOPK = 3
S5_WIDTH = 512
S5_GROUP = 16
S5_GROUPS = S5_WIDTH // S5_GROUP
S5_STATE = 64
N_MEM = 256
X_HEADS = 4
X_HD = D_MODEL // X_HEADS
A_SIZES = (GLA_HEADS * GLA_DK, GLA_HEADS * GLA_DK, GLA_HEADS * GLA_DV, GLA_GATE_RANK, GLA_HEADS * GLA_DV, MLA_Q_LORA, MLA_KV_LORA, MLA_ROPE)
N_IN_A = sum(A_SIZES)
A_OUT = GLA_HEADS * GLA_DV + MLA_HEADS * MLA_V
B_SIZES = (MOBA_HEADS * MOBA_HD, MOBA_HEADS * MOBA_HD, MOBA_HEADS * MOBA_HD, S5_WIDTH)
N_IN_B = sum(B_SIZES)
B_OUT = MOBA_HEADS * MOBA_HD + S5_WIDTH

kernel_name = "hybrid_gla_mla_moba_s5_macaron_step"


def rms_norm(x, g):
    x32 = x.astype(jnp.float32)
    y = x32 * lax.rsqrt(jnp.mean(x32 * x32, axis=-1, keepdims=True) + RMS_EPS)
    return (y * g.astype(jnp.float32)).astype(x.dtype)


def swiglu(h, wg, wu, wd):
    return (jax.nn.silu(h @ wg) * (h @ wu)) @ wd


def split_cols(h, sizes):
    idx = [int(i) for i in np.cumsum(sizes)[:-1]]
    return jnp.split(h, idx, axis=-1)


def rope(x, pos):
    half = x.shape[-1] // 2
    inv = ROPE_THETA ** (-jnp.arange(half, dtype=jnp.float32) / half)
    ang = pos.astype(jnp.float32)[:, None] * inv[None, :]
    shape = (ang.shape[0],) + (1,) * (x.ndim - 3) + (half,)
    cos = jnp.cos(ang).reshape(shape)
    sin = jnp.sin(ang).reshape(shape)
    x1 = x[..., :half].astype(jnp.float32)
    x2 = x[..., half:].astype(jnp.float32)
    return jnp.concatenate([x1 * cos - x2 * sin, x2 * cos + x1 * sin], axis=-1).astype(x.dtype)


def sweep_query_blocks(fn, q, q_pos):
    B, T = q.shape[0], q.shape[1]
    qb = math.gcd(T, Q_BLOCK)
    n = T // qb
    qs = jnp.moveaxis(q.reshape((B, n, qb) + q.shape[2:]), 1, 0)
    ps = q_pos.reshape(n, qb)
    out = lax.map(lambda a: fn(a[0], a[1]), (qs, ps))
    return jnp.moveaxis(out, 0, 1).reshape((B, T) + out.shape[3:])


def gla_recurrence(q, k, v, lg, s0, chunk):
    B, T, H, DK = q.shape
    DV = v.shape[-1]
    n = T // chunk

    def to_chunks(t):
        return jnp.moveaxis(t.astype(jnp.float32).reshape(B, n, chunk, H, t.shape[-1]), 1, 0)

    causal = jnp.tril(jnp.ones((chunk, chunk), dtype=bool))

    def step(S, inp):
        qc, kc, vc, gc = inp
        b = jnp.cumsum(gc, axis=1)
        qe = qc * jnp.exp(b)
        ke = kc * jnp.exp(-b)
        att = jnp.where(causal, jnp.einsum('blhk,bshk->bhls', qe, ke), 0.0)
        o = jnp.einsum('blhk,bhkv->blhv', qe, S) + jnp.einsum('bhls,bshv->blhv', att, vc)
        b_last = b[:, -1]
        kd = kc * jnp.exp(b_last[:, None] - b)
        S = S * jnp.exp(b_last)[..., None] + jnp.einsum('bshk,bshv->bhkv', kd, vc)
        return S, o

    S, o = lax.scan(step, s0.astype(jnp.float32), (to_chunks(q), to_chunks(k), to_chunks(v), to_chunks(lg)))
    o = jnp.moveaxis(o, 0, 1).reshape(B, T, H, DV)
    return o.astype(q.dtype), S.astype(s0.dtype)


def gla_mixer(qa, ka, va, ga, ra, s0, w_gate, b_gate, g_norm):
    B, T, _ = qa.shape
    q = qa.reshape(B, T, GLA_HEADS, GLA_DK) * (GLA_DK ** -0.5)
    k = ka.reshape(B, T, GLA_HEADS, GLA_DK)
    v = va.reshape(B, T, GLA_HEADS, GLA_DV)
    lg = (jax.nn.log_sigmoid((ga @ w_gate + b_gate).astype(jnp.float32)) / GLA_TAU).reshape(B, T, GLA_HEADS, GLA_DK)
    o, s = gla_recurrence(q, k, v, lg, s0, math.gcd(T, GLA_CHUNK))
    o = rms_norm(o, g_norm).reshape(B, T, GLA_HEADS * GLA_DV) * jax.nn.silu(ra)
    return o, s


def mqa_attention(q, k, v, q_pos, k_pos, scale):
    s = jnp.einsum('bqhd,bkd->bhqk', q, k).astype(jnp.float32) * scale
    s = jnp.where(k_pos[None, :] <= q_pos[:, None], s, NEG_INF)
    p = jax.nn.softmax(s, axis=-1)
    return jnp.einsum('bhqk,bkd->bqhd', p.astype(v.dtype), v)


def mla_mixer(q_lat, kv_lat, k_rope, ckv_past, kpe_past, pos, g_q, w_qb, g_kv, w_kvb):
    B, T, _ = q_lat.shape
    q = (rms_norm(q_lat, g_q) @ w_qb).reshape(B, T, MLA_HEADS, MLA_NOPE + MLA_ROPE)
    q_nope = q[..., :MLA_NOPE]
    q_pe = rope(q[..., MLA_NOPE:], pos)
    ckv = rms_norm(kv_lat, g_kv)
    kpe = rope(k_rope, pos)
    w = w_kvb.reshape(MLA_KV_LORA, MLA_HEADS, MLA_NOPE + MLA_V)
    q_abs = jnp.einsum('bthn,chn->bthc', q_nope, w[..., :MLA_NOPE])
    qq = jnp.concatenate([q_abs, q_pe], axis=-1)
    c_all = jnp.concatenate([ckv_past, ckv], axis=1)
    kk = jnp.concatenate([c_all, jnp.concatenate([kpe_past, kpe], axis=1)], axis=-1)
    k_pos = jnp.arange(c_all.shape[1], dtype=jnp.int32)
    scale = (MLA_NOPE + MLA_ROPE) ** -0.5
    o_lat = sweep_query_blocks(lambda qb, pb: mqa_attention(qb, kk, c_all, pb, k_pos, scale), qq, pos)
    o = jnp.einsum('bthc,chv->bthv', o_lat, w[..., MLA_NOPE:]).reshape(B, T, MLA_HEADS * MLA_V)
    return o, ckv, kpe


def even_mixer(h, pos, gla_s0, ckv_past, kpe_past, w_in, w_gate, b_gate, g_norm, g_q, w_qb, g_kv, w_kvb, w_out):
    qa, ka, va, ga, ra, q_lat, kv_lat, k_rope = split_cols(h @ w_in, A_SIZES)
    o_a, gla_s = gla_mixer(qa, ka, va, ga, ra, gla_s0, w_gate, b_gate, g_norm)
    o_b, ckv, kpe = mla_mixer(q_lat, kv_lat, k_rope, ckv_past, kpe_past, pos, g_q, w_qb, g_kv, w_kvb)
    return jnp.concatenate([o_a, o_b], axis=-1) @ w_out, gla_s, ckv, kpe


def moba_combine(s_sel, v_sel, s_own, v_own):
    if s_sel is None:
        p = jax.nn.softmax(s_own, axis=-1)
        return jnp.einsum('bhqk,bhkd->bhqd', p, v_own)
    B, H, Q, N, L = s_sel.shape
    p = jax.nn.softmax(jnp.concatenate([s_sel.reshape(B, H, Q, N * L), s_own], axis=-1), axis=-1)
    return (jnp.einsum('bhqnk,bhqnkd->bhqd', p[..., :N * L].reshape(B, H, Q, N, L), v_sel)
            + jnp.einsum('bhqk,bhkd->bhqd', p[..., N * L:], v_own))


def moba_prompt(q, k, v, pos):
    B, T, H, D = q.shape
    scale = D ** -0.5
    nb_all = -(-T // MOBA_BLOCK)
    nb_full = T // MOBA_BLOCK
    topn = min(MOBA_TOPK, nb_full)
    pad = nb_all * MOBA_BLOCK - T
    kp = jnp.pad(k, ((0, 0), (0, pad), (0, 0), (0, 0)))
    vp = jnp.pad(v, ((0, 0), (0, pad), (0, 0), (0, 0)))
    k_blocks = kp.reshape(B, nb_all, MOBA_BLOCK, H, D).transpose(0, 3, 1, 2, 4)
    v_blocks = vp.reshape(B, nb_all, MOBA_BLOCK, H, D).transpose(0, 3, 1, 2, 4)
    k_mean = jnp.mean(k_blocks[:, :, :nb_full].astype(jnp.float32), axis=3)
    bidx = jnp.arange(B)[:, None, None, None]
    hidx = jnp.arange(H)[None, :, None, None]

    def block_fn(qc, qpos):
        qh = jnp.swapaxes(qc, 1, 2)
        own = qpos[0] // MOBA_BLOCK
        k_own = lax.dynamic_index_in_dim(k_blocks, own, axis=2, keepdims=False)
        v_own = lax.dynamic_index_in_dim(v_blocks, own, axis=2, keepdims=False)
        own_pos = own * MOBA_BLOCK + jnp.arange(MOBA_BLOCK, dtype=jnp.int32)
        s_own = jnp.einsum('bhqd,bhkd->bhqk', qh, k_own).astype(jnp.float32) * scale
        s_own = jnp.where(own_pos[None, :] <= qpos[:, None], s_own, NEG_INF)
        if topn == 0:
            return jnp.swapaxes(moba_combine(None, None, s_own, v_own), 1, 2).astype(q.dtype)
        j = qpos // MOBA_BLOCK
        gate = jnp.einsum('bhqd,bhnd->bhqn', qh.astype(jnp.float32), k_mean)
        gate = jnp.where(jnp.arange(nb_full)[None, :] < j[:, None], gate, NEG_INF)
        _, sel = lax.top_k(gate, topn)
        valid = sel < j[:, None]
        k_sel = k_blocks[bidx, hidx, sel]
        v_sel = v_blocks[bidx, hidx, sel]
        s_sel = jnp.einsum('bhqd,bhqnkd->bhqnk', qh, k_sel).astype(jnp.float32) * scale
        s_sel = jnp.where(valid[..., None], s_sel, NEG_INF)
        return jnp.swapaxes(moba_combine(s_sel, v_sel, s_own, v_own), 1, 2).astype(q.dtype)

    return sweep_query_blocks(block_fn, q, pos)


def moba_sample(q, k_new, v_new, pool_k, pool_v, layer, page_table, pos):
    DB, S, H, D = q.shape
    n_pages = page_table.shape[1]
    past = n_pages * PAGE_SIZE
    ppb = MOBA_BLOCK // PAGE_SIZE
    j = past // MOBA_BLOCK
    topn = min(MOBA_TOPK, j)
    scale = D ** -0.5
    qh = jnp.swapaxes(q, 1, 2)
    own_pages = page_table[:, j * ppb:]
    n_own = own_pages.shape[1] * PAGE_SIZE
    k_own = jnp.concatenate([pool_k[layer, own_pages].reshape(DB, n_own, H, D), k_new], axis=1)
    v_own = jnp.concatenate([pool_v[layer, own_pages].reshape(DB, n_own, H, D), v_new], axis=1)
    k_own = jnp.swapaxes(k_own, 1, 2)
    v_own = jnp.swapaxes(v_own, 1, 2)
    own_pos = j * MOBA_BLOCK + jnp.arange(n_own + S, dtype=jnp.int32)
    s_own = jnp.einsum('bhqd,bhkd->bhqk', qh, k_own).astype(jnp.float32) * scale
    s_own = jnp.where(own_pos[None, :] <= pos[:, None], s_own, NEG_INF)
    if topn == 0:
        return jnp.swapaxes(moba_combine(None, None, s_own, v_own), 1, 2).astype(q.dtype)
    k_full = pool_k[layer, page_table[:, :j * ppb]].reshape(DB, j, MOBA_BLOCK, H, D)
    k_mean = jnp.mean(k_full.astype(jnp.float32), axis=2)
    gate = jnp.einsum('bhqd,bnhd->bhqn', qh.astype(jnp.float32), k_mean)
    _, sel = lax.top_k(gate, topn)
    logical = sel[..., None] * ppb + jnp.arange(ppb, dtype=jnp.int32)
    phys = page_table[jnp.arange(DB)[:, None, None, None, None], logical]
    rows = jnp.arange(PAGE_SIZE)
    hidx = jnp.arange(H)[None, :, None, None, None, None]
    k_sel = pool_k[layer, phys[..., None], rows, hidx].reshape(DB, H, S, topn, MOBA_BLOCK, D)
    v_sel = pool_v[layer, phys[..., None], rows, hidx].reshape(DB, H, S, topn, MOBA_BLOCK, D)
    s_sel = jnp.einsum('bhqd,bhqnkd->bhqnk', qh, k_sel).astype(jnp.float32) * scale
    return jnp.swapaxes(moba_combine(s_sel, v_sel, s_own, v_own), 1, 2).astype(q.dtype)


def s5_mixer(u, s0, lam_re, lam_im, log_dt, b_re, b_im, c_re, c_im, d_skip, w_glu, b_glu):
    B, T, _ = u.shape
    f32 = jnp.float32
    uf = u.astype(f32).reshape(B, T, S5_GROUPS, S5_GROUP)
    lr, li = lam_re.astype(f32), lam_im.astype(f32)
    dt = jnp.exp(log_dt.astype(f32))[:, None]
    mag = jnp.exp(lr * dt)
    lb_re, lb_im = mag * jnp.cos(li * dt), mag * jnp.sin(li * dt)
    nr, ni = lb_re - 1.0, lb_im
    den = lr * lr + li * li
    f_re, f_im = (nr * lr + ni * li) / den, (ni * lr - nr * li) / den
    br, bi = b_re.astype(f32), b_im.astype(f32)
    bb_re = f_re[..., None] * br - f_im[..., None] * bi
    bb_im = f_re[..., None] * bi + f_im[..., None] * br
    bu_re = jnp.einsum('btgc,gpc->btgp', uf, bb_re)
    bu_im = jnp.einsum('btgc,gpc->btgp', uf, bb_im)
    x0_re, x0_im = s0[..., 0].astype(f32), s0[..., 1].astype(f32)
    bu_re = bu_re.at[:, 0].add(lb_re * x0_re - lb_im * x0_im)
    bu_im = bu_im.at[:, 0].add(lb_re * x0_im + lb_im * x0_re)
    a_re = jnp.broadcast_to(lb_re, bu_re.shape)
    a_im = jnp.broadcast_to(lb_im, bu_im.shape)

    def combine(e1, e2):
        a1r, a1i, b1r, b1i = e1
        a2r, a2i, b2r, b2i = e2
        return (a2r * a1r - a2i * a1i, a2r * a1i + a2i * a1r,
                a2r * b1r - a2i * b1i + b2r, a2r * b1i + a2i * b1r + b2i)

    _, _, xr, xi = lax.associative_scan(combine, (a_re, a_im, bu_re, bu_im), axis=1)
    y = (jnp.einsum('btgp,gcp->btgc', xr, c_re.astype(f32)) - jnp.einsum('btgp,gcp->btgc', xi, c_im.astype(f32))
         + d_skip.astype(f32).reshape(S5_GROUPS, S5_GROUP) * uf)
    y = jax.nn.gelu(y.reshape(B, T, S5_WIDTH))
    out = y * jax.nn.sigmoid(y @ w_glu.astype(f32) + b_glu.astype(f32))
    s_new = jnp.stack([xr[:, -1], xi[:, -1]], axis=-1)
    return out.astype(u.dtype), s_new.astype(s0.dtype)


def odd_mixer(h, pos, s5_s0, pool_k, pool_v, layer, page_table, w_in, lam_re, lam_im, log_dt, b_re, b_im,
              c_re, c_im, d_skip, w_glu, b_glu, w_out):
    B, T, _ = h.shape
    qc, kc, vc, u = split_cols(h @ w_in, B_SIZES)
    shp = (B, T, MOBA_HEADS, MOBA_HD)
    q = rope(qc.reshape(shp), pos)
    k = rope(kc.reshape(shp), pos)
    v = vc.reshape(shp)
    if pool_k is None:
        o_c = moba_prompt(q, k, v, pos)
    else:
        o_c = moba_sample(q, k, v, pool_k, pool_v, layer, page_table, pos)
    o_d, s5_s = s5_mixer(u, s5_s0, lam_re, lam_im, log_dt, b_re, b_im, c_re, c_im, d_skip, w_glu, b_glu)
    out = jnp.concatenate([o_c.reshape(B, T, MOBA_HEADS * MOBA_HD), o_d], axis=-1) @ w_out
    return out, k, v, s5_s


def mem_kv(mem, wk, wv):
    B, M, _ = mem.shape
    return (mem @ wk).reshape(B, M, X_HEADS, X_HD), (mem @ wv).reshape(B, M, X_HEADS, X_HD)


def cross_attention(h, mk, mv, wq, wo):
    B, T, _ = h.shape
    q = (h @ wq).reshape(B, T, X_HEADS, X_HD)
    s = jnp.einsum('bthd,bmhd->bhtm', q, mk).astype(jnp.float32) * (X_HD ** -0.5)
    p = jax.nn.softmax(s, axis=-1).astype(mv.dtype)
    return jnp.einsum('bhtm,bmhd->bthd', p, mv).reshape(B, T, X_HEADS * X_HD) @ wo


def setup_inputs(seed: int = 0) -> dict:
    key = jax.random.key(seed)
    ks = iter(jax.random.split(key, 64))
    f32 = jnp.float32

    def nrm(shape, scale=1.0):
        return jax.random.normal(next(ks), shape, f32) * scale

    def gain(shape):
        return 1.0 + nrm(shape, 0.01)

    n_pages = PAST_LEN // PAGE_SIZE
    n_pool = (DEC_BATCH * n_pages * 5) // 4
    page_table = jax.random.permutation(next(ks), n_pool)[:DEC_BATCH * n_pages].reshape(DEC_BATCH, n_pages).astype(jnp.int32)
    n_idx = jnp.arange(S5_STATE, dtype=f32)
    return {
        "x_prompt": nrm((BATCH, SEQ, D_MODEL)),
        "x_sample": nrm((DEC_BATCH, DEC_SEQ, D_MODEL)),
        "cache_mem_k": nrm((DEPTH, DEC_BATCH, N_MEM, X_HEADS, X_HD)),
        "cache_mem_v": nrm((DEPTH, DEC_BATCH, N_MEM, X_HEADS, X_HD)),
        "state_gla": nrm((N_EVEN, DEC_BATCH, GLA_HEADS, GLA_DK, GLA_DV)),
        "cache_mla_ckv": nrm((N_EVEN, n_pool, PAGE_SIZE, MLA_KV_LORA)),
        "cache_mla_kpe": nrm((N_EVEN, n_pool, PAGE_SIZE, MLA_ROPE)),
        "cache_moba_k": nrm((N_ODD, n_pool, PAGE_SIZE, MOBA_HEADS, MOBA_HD)),
        "cache_moba_v": nrm((N_ODD, n_pool, PAGE_SIZE, MOBA_HEADS, MOBA_HD)),
        "state_s5": nrm((N_ODD, DEC_BATCH, S5_GROUPS, S5_STATE, 2)),
        "page_table": page_table,
        "mem_prompt": nrm((BATCH, N_MEM, D_MODEL)),
        "g_ffn1": gain((DEPTH, D_MODEL)),
        "w_ffn1_gate": nrm((DEPTH, D_MODEL, D_FF), D_MODEL ** -0.5),
        "w_ffn1_up": nrm((DEPTH, D_MODEL, D_FF), D_MODEL ** -0.5),
        "w_ffn1_down": nrm((DEPTH, D_FF, D_MODEL), D_FF ** -0.5),
        "g_mix": gain((DEPTH, D_MODEL)),
        "w_in_a": nrm((N_EVEN, D_MODEL, N_IN_A), D_MODEL ** -0.5),
        "w_gla_gate": nrm((N_EVEN, GLA_GATE_RANK, GLA_HEADS * GLA_DK), GLA_GATE_RANK ** -0.5),
        "b_gla_gate": nrm((N_EVEN, GLA_HEADS * GLA_DK), 0.1),
        "g_gla_norm": gain((N_EVEN, GLA_DV)),
        "g_mla_q": gain((N_EVEN, MLA_Q_LORA)),
        "w_mla_qb": nrm((N_EVEN, MLA_Q_LORA, MLA_HEADS * (MLA_NOPE + MLA_ROPE)), MLA_Q_LORA ** -0.5),
        "g_mla_kv": gain((N_EVEN, MLA_KV_LORA)),
        "w_mla_kvb": nrm((N_EVEN, MLA_KV_LORA, MLA_HEADS * (MLA_NOPE + MLA_V)), MLA_KV_LORA ** -0.5),
        "w_out_a": nrm((N_EVEN, A_OUT, D_MODEL), A_OUT ** -0.5),
        "w_in_b": nrm((N_ODD, D_MODEL, N_IN_B), D_MODEL ** -0.5),
        "s5_lam_re": -0.5 + nrm((N_ODD, S5_GROUPS, S5_STATE), 0.01),
        "s5_lam_im": math.pi * n_idx + nrm((N_ODD, S5_GROUPS, S5_STATE), 0.01),
        "s5_log_dt": jax.random.uniform(next(ks), (N_ODD, S5_GROUPS), f32, math.log(1e-3), math.log(1e-1)),
        "s5_b_re": nrm((N_ODD, S5_GROUPS, S5_STATE, S5_GROUP), (2 * S5_GROUP) ** -0.5),
        "s5_b_im": nrm((N_ODD, S5_GROUPS, S5_STATE, S5_GROUP), (2 * S5_GROUP) ** -0.5),
        "s5_c_re": nrm((N_ODD, S5_GROUPS, S5_GROUP, S5_STATE), (2 * S5_STATE) ** -0.5),
        "s5_c_im": nrm((N_ODD, S5_GROUPS, S5_GROUP, S5_STATE), (2 * S5_STATE) ** -0.5),
        "s5_d": nrm((N_ODD, S5_WIDTH)),
        "w_s5_glu": nrm((N_ODD, S5_WIDTH, S5_WIDTH), S5_WIDTH ** -0.5),
        "b_s5_glu": nrm((N_ODD, S5_WIDTH), 0.01),
        "w_out_b": nrm((N_ODD, B_OUT, D_MODEL), B_OUT ** -0.5),
        "g_xattn": gain((DEPTH, D_MODEL)),
        "w_xq": nrm((DEPTH, D_MODEL, X_HEADS * X_HD), D_MODEL ** -0.5),
        "w_xk": nrm((DEPTH, D_MODEL, X_HEADS * X_HD), D_MODEL ** -0.5),
        "w_xv": nrm((DEPTH, D_MODEL, X_HEADS * X_HD), D_MODEL ** -0.5),
        "w_xo": nrm((DEPTH, X_HEADS * X_HD, D_MODEL), (X_HEADS * X_HD) ** -0.5),
        "g_ffn2": gain((DEPTH, D_MODEL)),
        "w_ffn2_gate": nrm((DEPTH, D_MODEL, D_FF), D_MODEL ** -0.5),
        "w_ffn2_up": nrm((DEPTH, D_MODEL, D_FF), D_MODEL ** -0.5),
        "w_ffn2_down": nrm((DEPTH, D_FF, D_MODEL), D_FF ** -0.5),
        "g_final": gain((D_MODEL,)),
    }


def reference(x_prompt, x_sample, cache_mem_k, cache_mem_v, state_gla, cache_mla_ckv, cache_mla_kpe,
              cache_moba_k, cache_moba_v, state_s5, page_table, mem_prompt,
              g_ffn1, w_ffn1_gate, w_ffn1_up, w_ffn1_down, g_mix,
              w_in_a, w_gla_gate, b_gla_gate, g_gla_norm, g_mla_q, w_mla_qb, g_mla_kv, w_mla_kvb, w_out_a,
              w_in_b, s5_lam_re, s5_lam_im, s5_log_dt, s5_b_re, s5_b_im, s5_c_re, s5_c_im, s5_d, w_s5_glu, b_s5_glu, w_out_b,
              g_xattn, w_xq, w_xk, w_xv, w_xo, g_ffn2, w_ffn2_gate, w_ffn2_up, w_ffn2_down, g_final):
    xp, xs = x_prompt, x_sample
    B, T = xp.shape[0], xp.shape[1]
    DB, S = xs.shape[0], xs.shape[1]
    past = page_table.shape[1] * PAGE_SIZE
    pos_p = jnp.arange(T, dtype=jnp.int32)
    pos_s = past + jnp.arange(S, dtype=jnp.int32)
    mem_k_p, mem_v_p = [], []
    gla_p, gla_s = [], []
    ckv_p, kpe_p, ckv_s, kpe_s = [], [], [], []
    mk_p, mv_p, mk_s, mv_s = [], [], [], []
    s5_p, s5_s = [], []
    for l in range(DEPTH):
        i = l // 2
        xp = xp + 0.5 * swiglu(rms_norm(xp, g_ffn1[l]), w_ffn1_gate[l], w_ffn1_up[l], w_ffn1_down[l])
        xs = xs + 0.5 * swiglu(rms_norm(xs, g_ffn1[l]), w_ffn1_gate[l], w_ffn1_up[l], w_ffn1_down[l])
        hp = rms_norm(xp, g_mix[l])
        hs = rms_norm(xs, g_mix[l])
        if l % 2 == 0:
            wa = (w_in_a[i], w_gla_gate[i], b_gla_gate[i], g_gla_norm[i], g_mla_q[i], w_mla_qb[i],
                  g_mla_kv[i], w_mla_kvb[i], w_out_a[i])
            s0p = jnp.zeros((B, GLA_HEADS, GLA_DK, GLA_DV), jnp.float32)
            ckv0 = jnp.zeros((B, 0, MLA_KV_LORA), xp.dtype)
            kpe0 = jnp.zeros((B, 0, MLA_ROPE), xp.dtype)
            mp, sp, cp, kp_ = even_mixer(hp, pos_p, s0p, ckv0, kpe0, *wa)
            ckv_past = cache_mla_ckv[i, page_table].reshape(DB, past, MLA_KV_LORA)
            kpe_past = cache_mla_kpe[i, page_table].reshape(DB, past, MLA_ROPE)
            ms, ss, cs, ks_ = even_mixer(hs, pos_s, state_gla[i], ckv_past, kpe_past, *wa)
            gla_p.append(sp)
            gla_s.append(ss)
            ckv_p.append(cp)
            kpe_p.append(kp_)
            ckv_s.append(cs)
            kpe_s.append(ks_)
        else:
            wb = (w_in_b[i], s5_lam_re[i], s5_lam_im[i], s5_log_dt[i], s5_b_re[i], s5_b_im[i], s5_c_re[i],
                  s5_c_im[i], s5_d[i], w_s5_glu[i], b_s5_glu[i], w_out_b[i])
            s0p = jnp.zeros((B, S5_GROUPS, S5_STATE, 2), jnp.float32)
            mp, kp_, vp_, sp = odd_mixer(hp, pos_p, s0p, None, None, i, None, *wb)
            ms, ks_, vs_, ss = odd_mixer(hs, pos_s, state_s5[i], cache_moba_k, cache_moba_v, i, page_table, *wb)
            mk_p.append(kp_)
            mv_p.append(vp_)
            mk_s.append(ks_)
            mv_s.append(vs_)
            s5_p.append(sp)
            s5_s.append(ss)
        xp = xp + mp
        xs = xs + ms
        mkp, mvp = mem_kv(mem_prompt, w_xk[l], w_xv[l])
        mem_k_p.append(mkp)
        mem_v_p.append(mvp)
        xp = xp + cross_attention(rms_norm(xp, g_xattn[l]), mkp, mvp, w_xq[l], w_xo[l])
        xs = xs + cross_attention(rms_norm(xs, g_xattn[l]), cache_mem_k[l], cache_mem_v[l], w_xq[l], w_xo[l])
        xp = xp + 0.5 * swiglu(rms_norm(xp, g_ffn2[l]), w_ffn2_gate[l], w_ffn2_up[l], w_ffn2_down[l])
        xs = xs + 0.5 * swiglu(rms_norm(xs, g_ffn2[l]), w_ffn2_gate[l], w_ffn2_up[l], w_ffn2_down[l])
    y_prompt = rms_norm(xp, g_final)
    y_sample = rms_norm(xs, g_final)
    return (y_prompt, y_sample,
            jnp.stack(mem_k_p), jnp.stack(mem_v_p),
            jnp.stack(gla_p), jnp.stack(gla_s),
            jnp.stack(ckv_p), jnp.stack(kpe_p), jnp.stack(ckv_s), jnp.stack(kpe_s),
            jnp.stack(mk_p), jnp.stack(mv_p), jnp.stack(mk_s), jnp.stack(mv_s),
            jnp.stack(s5_p), jnp.stack(s5_s))
```

```python
import functools
import math

import jax
import jax.numpy as jnp
from jax import lax
from jax.experimental import pallas as pl
from jax.experimental.pallas import tpu as pltpu

F32 = jnp.float32
BF16 = jnp.bfloat16

RMS_EPS = 1e-6
ROPE_THETA = 10000.0
NEG_INF = -1e30
BELOW_NEG_INF = -3.0e38
PAGE_SIZE = 128
GLA_HEADS, GLA_DK, GLA_DV, GLA_GATE_RANK, GLA_TAU, GLA_CHUNK = 4, 64, 128, 16, 16.0, 64
MLA_HEADS, MLA_Q_LORA, MLA_KV_LORA, MLA_NOPE, MLA_ROPE, MLA_V = 4, 384, 256, 128, 64, 128
MOBA_HEADS, MOBA_HD, MOBA_BLOCK, MOBA_TOPK = 4, 128, 256, 3
S5_WIDTH, S5_GROUP, S5_STATE = 512, 16, 64
S5_GROUPS = S5_WIDTH // S5_GROUP
S5_N = S5_GROUPS * S5_STATE
X_HEADS = 4
LANES = 128
SUBLANES = 8
SAMPLE_PAD = SUBLANES
VMEM_LIMIT = 56 * 1024 * 1024
MLA_PAGES_PER_STEP = 8
MOBA_PAGES_PER_STEP = 8


def _cparams(*sem):
    return pltpu.CompilerParams(dimension_semantics=sem, vmem_limit_bytes=VMEM_LIMIT)


def _row_tile(m, target):
    t = min(m, target)
    while m % t or t % SUBLANES:
        t -= 1
    return t


def _rms(x, g):
    return x * lax.rsqrt(jnp.mean(x * x, axis=-1, keepdims=True) + RMS_EPS) * g


def _sigmoid(x):
    return 1.0 / (1.0 + jnp.exp(-x))


def _dot(a, b):
    return jnp.dot(a, b, preferred_element_type=F32)


def _dot_nt(a, b):
    return lax.dot_general(a, b, (((1,), (1,)), ((), ())), preferred_element_type=F32)


def _dot_nt_f32(a, b):
    return lax.dot_general(a, b, (((1,), (1,)), ((), ())), precision=lax.Precision.HIGHEST,
                           preferred_element_type=F32)


def _dot_tn(a, b):
    return lax.dot_general(a, b, (((0,), (0,)), ((), ())), preferred_element_type=F32)


def _full(a):
    return pl.BlockSpec(a.shape, lambda *_: (0,) * a.ndim)


def _ffn_kernel(x_ref, g_ref, wg_ref, wu_ref, wd_ref, o_ref, h_ref):
    j = pl.program_id(1)

    @pl.when(j == 0)
    def _():
        h_ref[...] = _rms(x_ref[...], g_ref[...]).astype(BF16)
        o_ref[...] = jnp.zeros_like(o_ref)

    h = h_ref[...]
    a = _dot(h, wg_ref[...])
    z = a * _sigmoid(a) * _dot(h, wu_ref[...])
    o_ref[...] += _dot(z.astype(BF16), wd_ref[...])

    @pl.when(j == pl.num_programs(1) - 1)
    def _():
        o_ref[...] = x_ref[...] + 0.5 * o_ref[...]


def _ffn(x, g, wg, wu, wd):
    m, d = x.shape
    f = wg.shape[1]
    tm = _row_tile(m, 1024)
    tf = 256 if f % 256 == 0 else f
    return pl.pallas_call(
        _ffn_kernel,
        grid=(m // tm, f // tf),
        in_specs=[
            pl.BlockSpec((tm, d), lambda i, j: (i, 0)),
            pl.BlockSpec((1, d), lambda i, j: (0, 0)),
            pl.BlockSpec((d, tf), lambda i, j: (0, j)),
            pl.BlockSpec((d, tf), lambda i, j: (0, j)),
            pl.BlockSpec((tf, d), lambda i, j: (j, 0)),
        ],
        out_specs=pl.BlockSpec((tm, d), lambda i, j: (i, 0)),
        out_shape=jax.ShapeDtypeStruct((m, d), F32),
        scratch_shapes=[pltpu.VMEM((tm, d), BF16)],
        compiler_params=_cparams("parallel", "arbitrary"),
        name="ffn",
    )(x, g.reshape(1, d), wg, wu, wd)


def _proj_kernel(x_ref, g_ref, w_ref, *o_refs, widths, norm):
    x = x_ref[...]
    h = _rms(x, g_ref[...]) if norm else x
    y = _dot(h.astype(BF16), w_ref[...])
    off = 0
    for o_ref, wd in zip(o_refs, widths):
        o_ref[...] = y[:, off:off + wd].astype(o_ref.dtype)
        off += wd


def _proj(x, g, w, widths, norm=True):
    m, d = x.shape
    n = w.shape[1]
    assert sum(widths) == n
    tm = _row_tile(m, 512)
    return pl.pallas_call(
        functools.partial(_proj_kernel, widths=widths, norm=norm),
        grid=(m // tm,),
        in_specs=[pl.BlockSpec((tm, d), lambda i: (i, 0)), pl.BlockSpec((1, d), lambda i: (0, 0)), _full(w)],
        out_specs=[pl.BlockSpec((tm, wd), lambda i: (i, 0)) for wd in widths],
        out_shape=[jax.ShapeDtypeStruct((m, wd), F32) for wd in widths],
        compiler_params=_cparams("parallel"),
        name="proj",
    )(x, g.reshape(1, d), w)


def _outproj_kernel(a_ref, b_ref, wa_ref, wb_ref, x_ref, o_ref):
    o_ref[...] = x_ref[...] + _dot(a_ref[...], wa_ref[...]) + _dot(b_ref[...], wb_ref[...])


def _outproj(a, b, wa, wb, x):
    m, d = x.shape
    tm = _row_tile(m, 512)
    return pl.pallas_call(
        _outproj_kernel,
        grid=(m // tm,),
        in_specs=[
            pl.BlockSpec((tm, a.shape[1]), lambda i: (i, 0)),
            pl.BlockSpec((tm, b.shape[1]), lambda i: (i, 0)),
            _full(wa), _full(wb),
            pl.BlockSpec((tm, d), lambda i: (i, 0)),
        ],
        out_specs=pl.BlockSpec((tm, d), lambda i: (i, 0)),
        out_shape=jax.ShapeDtypeStruct((m, d), F32),
        compiler_params=_cparams("parallel"),
        name="outproj",
    )(a, b, wa, wb, x)


def _norm_kernel(x_ref, g_ref, o_ref):
    o_ref[...] = _rms(x_ref[...], g_ref[...])


def _final_norm(x, g):
    m, d = x.shape
    tm = _row_tile(m, 1024)
    return pl.pallas_call(
        _norm_kernel,
        grid=(m // tm,),
        in_specs=[pl.BlockSpec((tm, d), lambda i: (i, 0)), pl.BlockSpec((1, d), lambda i: (0, 0))],
        out_specs=pl.BlockSpec((tm, d), lambda i: (i, 0)),
        out_shape=jax.ShapeDtypeStruct((m, d), F32),
        compiler_params=_cparams("parallel"),
        name="final_norm",
    )(x, g.reshape(1, d))


def _xattn_core(x, g, wq, wo, mem_k, mem_v, seq_rows):
    d = x.shape[1]
    hd = d // X_HEADS
    q = (_dot(_rms(x, g).astype(BF16), wq) * (hd ** -0.5)).astype(BF16)
    outs = []
    for s in range(x.shape[0] // seq_rows):
        qs = q[s * seq_rows:(s + 1) * seq_rows]
        heads = []
        for h in range(X_HEADS):
            sc = _dot_nt(qs[:, h * hd:(h + 1) * hd], mem_k(s, h))
            p = jnp.exp(sc - jnp.max(sc, axis=-1, keepdims=True))
            p = p / jnp.sum(p, axis=-1, keepdims=True)
            heads.append(_dot(p.astype(BF16), mem_v(s, h)))
        outs.append(jnp.concatenate(heads, axis=-1))
    o = outs[0] if len(outs) == 1 else jnp.concatenate(outs, axis=0)
    return x + _dot(o.astype(BF16), wo)


def _xattn_prompt_kernel(x_ref, g_ref, wq_ref, wo_ref, mk_ref, mv_ref, o_ref):
    x = x_ref[...]
    hd = x.shape[1] // X_HEADS
    mk = mk_ref[...].astype(BF16)
    mv = mv_ref[...].astype(BF16)
    o_ref[...] = _xattn_core(x, g_ref[...], wq_ref[...], wo_ref[...],
                             lambda s, h: mk[:, h * hd:(h + 1) * hd], lambda s, h: mv[:, h * hd:(h + 1) * hd],
                             x.shape[0])


def _xattn_prompt(x, g, wq, wo, mk, mv, seq):
    m, d = x.shape
    nm = mk.shape[1]
    tm = _row_tile(seq, 512)
    nt = seq // tm
    return pl.pallas_call(
        _xattn_prompt_kernel,
        grid=(m // tm,),
        in_specs=[
            pl.BlockSpec((tm, d), lambda i: (i, 0)),
            pl.BlockSpec((1, d), lambda i: (0, 0)),
            _full(wq), _full(wo),
            pl.BlockSpec((None, nm, d), lambda i: (i // nt, 0, 0)),
            pl.BlockSpec((None, nm, d), lambda i: (i // nt, 0, 0)),
        ],
        out_specs=pl.BlockSpec((tm, d), lambda i: (i, 0)),
        out_shape=jax.ShapeDtypeStruct((m, d), F32),
        compiler_params=_cparams("parallel"),
        name="xattn_prompt",
    )(x, g.reshape(1, d), wq, wo, mk, mv)


def _xattn_sample_kernel(x_ref, g_ref, wq_ref, wo_ref, mk_ref, mv_ref, o_ref, *, nseq):
    o_ref[...] = _xattn_core(x_ref[...], g_ref[...], wq_ref[...], wo_ref[...],
                             lambda s, h: mk_ref[s, :, h, :].astype(BF16),
                             lambda s, h: mv_ref[s, :, h, :].astype(BF16),
                             x_ref.shape[0] // nseq)


def _xattn_sample(x, g, wq, wo, cache_k, cache_v, layer):
    m, d = x.shape
    _, db, nm, nh, hd = cache_k.shape
    nseq = 4 if db % 4 == 0 else 1
    tm = nseq * (m // db)
    cache_spec = pl.BlockSpec((None, nseq, nm, nh, hd), lambda i: (layer, i, 0, 0, 0))
    return pl.pallas_call(
        functools.partial(_xattn_sample_kernel, nseq=nseq),
        grid=(db // nseq,),
        in_specs=[
            pl.BlockSpec((tm, d), lambda i: (i, 0)),
            pl.BlockSpec((1, d), lambda i: (0, 0)),
            _full(wq), _full(wo), cache_spec, cache_spec,
        ],
        out_specs=pl.BlockSpec((tm, d), lambda i: (i, 0)),
        out_shape=jax.ShapeDtypeStruct((m, d), F32),
        compiler_params=_cparams("parallel"),
        name="xattn_sample",
    )(x, g.reshape(1, d), wq, wo, cache_k, cache_v)


def _split3(x):
    hi = x.astype(BF16)
    r = x - hi.astype(F32)
    mid = r.astype(BF16)
    lo = (r - mid.astype(F32)).astype(BF16)
    return hi, mid, lo


def _gla_kernel(q_ref, k_ref, v_ref, gs_ref, ra_ref, s0_ref, wg_ref, bg_ref, gn_ref, o_ref, sout_ref, st_ref,
                *, rows, chunk, nchunks, valid):
    t = pl.program_id(1)

    @pl.when(t == 0)
    def _():
        st_ref[...] = s0_ref[...]

    L = chunk
    r_io = lax.broadcasted_iota(jnp.int32, (L, L), 0)
    c_io = lax.broadcasted_iota(jnp.int32, (L, L), 1)
    tril = r_io >= c_io
    tril_bf = jnp.where(tril, 1.0, 0.0).astype(BF16)
    ones_bf = jnp.ones((L, GLA_DV), BF16)
    live = lax.broadcasted_iota(jnp.int32, (L, 1), 0) < valid

    def load(ref, r0):
        x = ref[pl.ds(r0, rows), :]
        if rows < L:
            x = jnp.concatenate([x, jnp.zeros((L - rows, x.shape[1]), x.dtype)], axis=0)
        return x

    def body(c, carry):
        r0 = pl.multiple_of(c * rows, rows)
        q = load(q_ref, r0) * (GLA_DK ** -0.5)
        k = load(k_ref, r0)
        v = load(v_ref, r0)
        ra = load(ra_ref, r0)
        z = _dot(load(gs_ref, r0).astype(BF16), wg_ref[...]) + bg_ref[...]
        lg = (jnp.minimum(z, 0.0) - jnp.log1p(jnp.exp(-jnp.abs(z)))) * (1.0 / GLA_TAU)
        if valid < L:
            lg = jnp.where(live, lg, 0.0)
            k = jnp.where(live, k, 0.0)
            v = jnp.where(live, v, 0.0)
        hi, mid, lo = _split3(lg)
        b = _dot(tril_bf, hi) + _dot(tril_bf, mid) + _dot(tril_bf, lo)
        bl = b[L - 1:L, :]
        qe = (q * jnp.exp(b)).astype(BF16)
        ke = (k * jnp.exp(-b)).astype(BF16)
        kd = (k * jnp.exp(bl - b)).astype(BF16)
        vb = v.astype(BF16)
        outs = []
        for h in range(GLA_HEADS):
            ks = slice(h * GLA_DK, (h + 1) * GLA_DK)
            vs = slice(h * GLA_DV, (h + 1) * GLA_DV)
            att = jnp.where(tril, _dot_nt(qe[:, ks], ke[:, ks]), 0.0)
            s_h = st_ref[h]
            o_h = _dot(qe[:, ks], s_h.astype(BF16)) + _dot(att.astype(BF16), vb[:, vs])
            dec = _dot_tn(hi[:, ks], ones_bf) + _dot_tn(mid[:, ks], ones_bf) + _dot_tn(lo[:, ks], ones_bf)
            st_ref[h] = s_h * jnp.exp(dec) + _dot_tn(kd[:, ks], vb[:, vs])
            on = _rms(o_h, gn_ref[...])
            rh = ra[:, vs]
            outs.append(on * (rh * _sigmoid(rh)))
        o = jnp.concatenate(outs, axis=-1)
        o_ref[pl.ds(r0, rows), :] = o[:rows].astype(o_ref.dtype)
        return carry

    if nchunks == 1:
        body(0, 0)
    else:
        lax.fori_loop(0, nchunks, body, 0)

    @pl.when(t == pl.num_programs(1) - 1)
    def _():
        sout_ref[...] = st_ref[...]


def _gla(q, k, v, gs, ra, s0, wg, bg, gn, *, seq, rows, chunk, valid):
    m = q.shape[0]
    nb = m // seq
    tt = _row_tile(seq, 512) if rows == chunk else rows
    nchunks = tt // rows
    nt = seq // tt
    row_spec = lambda w: pl.BlockSpec((tt, w), lambda b, t: (b * nt + t, 0))
    st_spec = pl.BlockSpec((None, GLA_HEADS, GLA_DK, GLA_DV), lambda b, t: (b, 0, 0, 0))
    return pl.pallas_call(
        functools.partial(_gla_kernel, rows=rows, chunk=chunk, nchunks=nchunks, valid=valid),
        grid=(nb, nt),
        in_specs=[row_spec(q.shape[1]), row_spec(k.shape[1]), row_spec(v.shape[1]), row_spec(gs.shape[1]),
                  row_spec(ra.shape[1]), st_spec, _full(wg), _full(bg), _full(gn)],
        out_specs=[row_spec(v.shape[1]), st_spec],
        out_shape=[jax.ShapeDtypeStruct((m, v.shape[1]), BF16),
                   jax.ShapeDtypeStruct((nb, GLA_HEADS, GLA_DK, GLA_DV), F32)],
        scratch_shapes=[pltpu.VMEM((GLA_HEADS, GLA_DK, GLA_DV), F32)],
        compiler_params=_cparams("parallel", "arbitrary"),
        name="gla",
    )(q, k, v, gs, ra, s0, wg, bg, gn)


def _rope(x, cos, sin_signed, group):
    w = x.shape[-1]
    ax = x.ndim - 1
    lane = lax.broadcasted_iota(jnp.int32, x.shape, ax)
    first = (lane % group) < (group // 2)
    partner = jnp.where(first, pltpu.roll(x, w - group // 2, ax), pltpu.roll(x, group // 2, ax))
    return x * cos + partner * sin_signed


def _rope_tables(pos, dim, width, reps):
    half = dim // 2
    inv = ROPE_THETA ** (-jnp.arange(half, dtype=F32) / half)
    ang = pos.astype(F32)[:, None] * inv[None, :]
    cos, sin = jnp.cos(ang), jnp.sin(ang)
    pad = jnp.zeros((pos.shape[0], width - dim), F32)
    c = jnp.concatenate([cos, cos, pad], axis=-1)
    s = jnp.concatenate([-sin, sin, pad], axis=-1)
    return jnp.tile(c, (1, reps)), jnp.tile(s, (1, reps))


def _mla_prep_kernel(ql_ref, kvl_ref, ks_ref, cos_ref, sin_ref, gq_ref, gkv_ref, wqn_ref, wqp_ref, wuk_ref,
                     qa_ref, qp_ref, ckv_ref, kpe_ref, ckvb_ref, kpeb_ref):
    scale = (MLA_NOPE + MLA_ROPE) ** -0.5
    hq = _rms(ql_ref[...], gq_ref[...]).astype(BF16)
    qn = _dot(hq, wqn_ref[...]).astype(BF16)
    qabs = [_dot(qn[:, h * MLA_NOPE:(h + 1) * MLA_NOPE], wuk_ref[h]) for h in range(MLA_HEADS)]
    qa_ref[...] = (jnp.concatenate(qabs, axis=-1) * scale).astype(BF16)
    cos, sin = cos_ref[...], sin_ref[...]
    qp = _rope(_dot(hq, wqp_ref[...]), cos, sin, MLA_ROPE)
    qp_ref[...] = (qp * scale).astype(BF16)
    ckv = _rms(kvl_ref[...], gkv_ref[...])
    ckv_ref[...] = ckv
    ckvb_ref[...] = ckv.astype(BF16)
    kpe = _rope(ks_ref[...], cos[:, :LANES], sin[:, :LANES], MLA_ROPE)
    kpe_ref[...] = kpe[:, :MLA_ROPE]
    kpeb_ref[...] = kpe.astype(BF16)


def _mla_prep(q_lat, kv_lat, ks, cos, sin, gq, gkv, wqn, wqp, wuk):
    m = q_lat.shape[0]
    p = cos.shape[0]
    tm = _row_tile(math.gcd(m, p), 512)
    npos = p // tm
    row = lambda w: pl.BlockSpec((tm, w), lambda i: (i, 0))
    pos = lambda w: pl.BlockSpec((tm, w), lambda i: (i % npos, 0))
    hq, hp = MLA_HEADS * MLA_KV_LORA, MLA_HEADS * LANES
    return pl.pallas_call(
        _mla_prep_kernel,
        grid=(m // tm,),
        in_specs=[row(MLA_Q_LORA), row(MLA_KV_LORA), row(LANES), pos(hp), pos(hp),
                  _full(gq), _full(gkv), _full(wqn), _full(wqp), _full(wuk)],
        out_specs=[row(hq), row(hp), row(MLA_KV_LORA), row(MLA_ROPE), row(MLA_KV_LORA), row(LANES)],
        out_shape=[jax.ShapeDtypeStruct((m, hq), BF16), jax.ShapeDtypeStruct((m, hp), BF16),
                   jax.ShapeDtypeStruct((m, MLA_KV_LORA), F32), jax.ShapeDtypeStruct((m, MLA_ROPE), F32),
                   jax.ShapeDtypeStruct((m, MLA_KV_LORA), BF16), jax.ShapeDtypeStruct((m, LANES), BF16)],
        compiler_params=_cparams("parallel"),
        name="mla_prep",
    )(q_lat, kv_lat, ks, cos, sin, gq, gkv, wqn, wqp, wuk)


def _stack_heads(x, width):
    return jnp.concatenate([x[:, h * width:(h + 1) * width] for h in range(MLA_HEADS)], axis=0)


def _softmax_update(s, v, m_ref, l_ref, acc_ref):
    m_prev = m_ref[...]
    m_new = jnp.maximum(m_prev, jnp.max(s, axis=-1, keepdims=True))
    alpha = jnp.exp(m_prev - m_new)
    p = jnp.exp(s - m_new)
    l_ref[...] = alpha * l_ref[...] + jnp.sum(p, axis=-1, keepdims=True)
    acc_ref[...] = alpha * acc_ref[...] + _dot(p.astype(BF16), v)
    m_ref[...] = m_new


def _softmax_init(m_ref, l_ref, acc_ref):
    m_ref[...] = jnp.full_like(m_ref, NEG_INF)
    l_ref[...] = jnp.zeros_like(l_ref)
    acc_ref[...] = jnp.zeros_like(acc_ref)


def _mla_out(acc_ref, l_ref, wuv_ref, rows):
    o_lat = (acc_ref[...] / l_ref[...]).astype(BF16)
    return jnp.concatenate([_dot(o_lat[h * rows:(h + 1) * rows], wuv_ref[h]) for h in range(MLA_HEADS)], axis=-1)


def _mla_prompt_kernel(qa_ref, qp_ref, ckv_ref, kpe_ref, wuv_ref, o_ref, qas_ref, qps_ref, m_ref, l_ref, acc_ref,
                       *, tq, tk):
    qi, ki = pl.program_id(1), pl.program_id(2)
    last = ((qi + 1) * tq - 1) // tk

    @pl.when(ki == 0)
    def _():
        qas_ref[...] = _stack_heads(qa_ref[...], MLA_KV_LORA)
        qps_ref[...] = _stack_heads(qp_ref[...], LANES)
        _softmax_init(m_ref, l_ref, acc_ref)

    def step(masked):
        ckv = ckv_ref[...]
        s = _dot_nt(qas_ref[...], ckv) + _dot_nt(qps_ref[...], kpe_ref[...])
        if masked:
            rows = MLA_HEADS * tq
            qpos = qi * tq + lax.broadcasted_iota(jnp.int32, (rows, tk), 0) % tq
            kpos = ki * tk + lax.broadcasted_iota(jnp.int32, (rows, tk), 1)
            s = jnp.where(kpos <= qpos, s, NEG_INF)
        _softmax_update(s, ckv, m_ref, l_ref, acc_ref)

    crosses = (ki + 1) * tk - 1 > qi * tq

    @pl.when(jnp.logical_and(ki <= last, crosses))
    def _():
        step(True)

    @pl.when(jnp.logical_and(ki <= last, jnp.logical_not(crosses)))
    def _():
        step(False)

    @pl.when(ki == last)
    def _():
        o_ref[...] = _mla_out(acc_ref, l_ref, wuv_ref, tq).astype(o_ref.dtype)


def _mla_prompt(qa, qp, ckvb, kpeb, wuv, seq):
    m = qa.shape[0]
    nb = m // seq
    tq = tk = _row_tile(seq, 512)
    nq, nk = seq // tq, seq // tk
    kmap = lambda b, qi, ki: (b * nk + jnp.minimum(ki, ((qi + 1) * tq - 1) // tk), 0)
    qmap = lambda b, qi, ki: (b * nq + qi, 0)
    rows = MLA_HEADS * tq
    return pl.pallas_call(
        functools.partial(_mla_prompt_kernel, tq=tq, tk=tk),
        grid=(nb, nq, nk),
        in_specs=[pl.BlockSpec((tq, qa.shape[1]), qmap), pl.BlockSpec((tq, qp.shape[1]), qmap),
                  pl.BlockSpec((tk, ckvb.shape[1]), kmap), pl.BlockSpec((tk, kpeb.shape[1]), kmap), _full(wuv)],
        out_specs=pl.BlockSpec((tq, MLA_HEADS * MLA_V), qmap),
        out_shape=jax.ShapeDtypeStruct((m, MLA_HEADS * MLA_V), BF16),
        scratch_shapes=[pltpu.VMEM((rows, MLA_KV_LORA), BF16), pltpu.VMEM((rows, LANES), BF16),
                        pltpu.VMEM((rows, 1), F32), pltpu.VMEM((rows, 1), F32), pltpu.VMEM((rows, MLA_KV_LORA), F32)],
        compiler_params=_cparams("parallel", "parallel", "arbitrary"),
        name="mla_prompt",
    )(qa, qp, ckvb, kpeb, wuv)


def _mla_sample_kernel(pt_ref, qa_ref, qp_ref, ckvn_ref, kpen_ref, wuv_ref, *rest, npg, valid):
    ckv_pages, kpe_pages = rest[:npg], rest[npg:2 * npg]
    o_ref, qas_ref, qps_ref, m_ref, l_ref, acc_ref = rest[2 * npg:]
    g = pl.program_id(1)
    sp = qa_ref.shape[0]

    @pl.when(g == 0)
    def _():
        qas_ref[...] = _stack_heads(qa_ref[...], MLA_KV_LORA)
        qps_ref[...] = _stack_heads(qp_ref[...], LANES)
        _softmax_init(m_ref, l_ref, acc_ref)

    qas = qas_ref[...]
    qps = qps_ref[...]
    ckv = jnp.concatenate([r[...].astype(BF16) for r in ckv_pages], axis=0)
    kpe = jnp.concatenate([r[...].astype(BF16) for r in kpe_pages], axis=0)
    s = _dot_nt(qas, ckv) + _dot_nt(qps[:, :MLA_ROPE], kpe)
    _softmax_update(s, ckv, m_ref, l_ref, acc_ref)

    @pl.when(g == pl.num_programs(1) - 1)
    def _():
        ckvn = ckvn_ref[...]
        s = _dot_nt(qas, ckvn) + _dot_nt(qps, kpen_ref[...])
        rows = MLA_HEADS * sp
        qpos = lax.broadcasted_iota(jnp.int32, (rows, sp), 0) % sp
        kpos = lax.broadcasted_iota(jnp.int32, (rows, sp), 1)
        s = jnp.where(jnp.logical_and(kpos <= qpos, kpos < valid), s, NEG_INF)
        _softmax_update(s, ckvn, m_ref, l_ref, acc_ref)
        o_ref[...] = _mla_out(acc_ref, l_ref, wuv_ref, sp).astype(o_ref.dtype)


def _mla_sample(qa, qp, ckvb, kpeb, wuv, cache_ckv, cache_kpe, layer, page_table, valid):
    m = qa.shape[0]
    db, n_pages = page_table.shape
    sp = m // db
    npg = math.gcd(n_pages, MLA_PAGES_PER_STEP)
    rows = MLA_HEADS * sp
    qmap = lambda b, g, pt: (b, 0)

    def page_spec(width, j):
        return pl.BlockSpec((None, None, PAGE_SIZE, width),
                            lambda b, g, pt: (layer, pt[b * n_pages + g * npg + j], 0, 0))

    grid_spec = pltpu.PrefetchScalarGridSpec(
        num_scalar_prefetch=1,
        grid=(db, n_pages // npg),
        in_specs=[pl.BlockSpec((sp, qa.shape[1]), qmap), pl.BlockSpec((sp, qp.shape[1]), qmap),
                  pl.BlockSpec((sp, ckvb.shape[1]), qmap), pl.BlockSpec((sp, kpeb.shape[1]), qmap),
                  pl.BlockSpec(wuv.shape, lambda b, g, pt: (0, 0, 0))]
                 + [page_spec(MLA_KV_LORA, j) for j in range(npg)]
                 + [page_spec(MLA_ROPE, j) for j in range(npg)],
        out_specs=pl.BlockSpec((sp, MLA_HEADS * MLA_V), qmap),
        scratch_shapes=[pltpu.VMEM((rows, MLA_KV_LORA), BF16), pltpu.VMEM((rows, LANES), BF16),
                        pltpu.VMEM((rows, 1), F32), pltpu.VMEM((rows, 1), F32), pltpu.VMEM((rows, MLA_KV_LORA), F32)],
    )
    return pl.pallas_call(
        functools.partial(_mla_sample_kernel, npg=npg, valid=valid),
        grid_spec=grid_spec,
        out_shape=jax.ShapeDtypeStruct((m, MLA_HEADS * MLA_V), BF16),
        compiler_params=_cparams("parallel", "arbitrary"),
        name="mla_sample",
    )(page_table.reshape(-1), qa, qp, ckvb, kpeb, wuv, *([cache_ckv] * npg), *([cache_kpe] * npg))


def _moba_prep_kernel(q_ref, k_ref, v_ref, cos_ref, sin_ref, qr_ref, kr_ref, kb_ref, vb_ref, *km_ref, nblk):
    cos, sin = cos_ref[...], sin_ref[...]
    qr_ref[...] = _rope(q_ref[...], cos, sin, MOBA_HD)
    kr = _rope(k_ref[...], cos, sin, MOBA_HD)
    kr_ref[...] = kr
    kb_ref[...] = kr.astype(BF16)
    vb_ref[...] = v_ref[...].astype(BF16)
    for j in range(nblk):
        km_ref[0][j] = jnp.mean(kr[j * MOBA_BLOCK:(j + 1) * MOBA_BLOCK], axis=0, keepdims=True)


def _moba_prep(q, k, v, cos, sin, with_means):
    m, w = q.shape
    p = cos.shape[0]
    tm = _row_tile(math.gcd(m, p), 512)
    npos = p // tm
    if with_means:
        assert tm % MOBA_BLOCK == 0
    nblk = tm // MOBA_BLOCK if with_means else 0
    row = pl.BlockSpec((tm, w), lambda i: (i, 0))
    pos = pl.BlockSpec((tm, w), lambda i: (i % npos, 0))
    out_specs = [row, row, row, row]
    out_shape = [jax.ShapeDtypeStruct((m, w), F32), jax.ShapeDtypeStruct((m, w), F32),
                 jax.ShapeDtypeStruct((m, w), BF16), jax.ShapeDtypeStruct((m, w), BF16)]
    if with_means:
        out_specs.append(pl.BlockSpec((nblk, 1, w), lambda i: (i, 0, 0)))
        out_shape.append(jax.ShapeDtypeStruct((m // MOBA_BLOCK, 1, w), F32))
    return pl.pallas_call(
        functools.partial(_moba_prep_kernel, nblk=nblk),
        grid=(m // tm,),
        in_specs=[row, row, row, pos, pos],
        out_specs=out_specs,
        out_shape=out_shape,
        compiler_params=_cparams("parallel"),
        name="moba_prep",
    )(q, k, v, cos, sin)


def _top_rows(gate, idx, count, limit, axis):
    sel = jnp.zeros(gate.shape, F32)
    for _ in range(count):
        mx = jnp.max(gate, axis=axis, keepdims=True)
        first = jnp.min(jnp.where(gate == mx, idx, limit), axis=axis, keepdims=True)
        pick = idx == first
        sel = jnp.where(pick, 1.0, sel)
        gate = jnp.where(pick, BELOW_NEG_INF, gate)
    return sel


def _moba_prompt_kernel(q_ref, k_ref, v_ref, km_ref, o_ref, m_ref, l_ref, acc_ref, *, nb, topn):
    qi = pl.program_id(2)
    blk = MOBA_BLOCK
    q = q_ref[...]
    qb = (q * (MOBA_HD ** -0.5)).astype(BF16)
    n_idx = lax.broadcasted_iota(jnp.int32, (blk, nb), 1)
    past = n_idx < qi
    gate = jnp.where(past, _dot_nt_f32(q, km_ref[...]), NEG_INF)
    sel = jnp.where(past, _top_rows(gate, n_idx, topn, nb, 1), 0.0)

    r0 = pl.multiple_of(qi * blk, blk)
    s = _dot_nt(qb, k_ref[pl.ds(r0, blk), :])
    causal = lax.broadcasted_iota(jnp.int32, (blk, blk), 1) <= lax.broadcasted_iota(jnp.int32, (blk, blk), 0)
    s = jnp.where(causal, s, NEG_INF)
    _softmax_init(m_ref, l_ref, acc_ref)
    _softmax_update(s, v_ref[pl.ds(r0, blk), :], m_ref, l_ref, acc_ref)

    def body(n, carry):
        allowed = jnp.max(jnp.where(n_idx == n, sel, 0.0), axis=1, keepdims=True) > 0.0
        rn = pl.multiple_of(n * blk, blk)
        sn = jnp.where(allowed, _dot_nt(qb, k_ref[pl.ds(rn, blk), :]), NEG_INF)
        _softmax_update(sn, v_ref[pl.ds(rn, blk), :], m_ref, l_ref, acc_ref)
        return carry

    lax.fori_loop(0, qi, body, 0)
    o_ref[...] = (acc_ref[...] / l_ref[...]).astype(o_ref.dtype)


def _moba_prompt(qr, kb, vb, km, seq):
    m = qr.shape[0]
    nbatch = m // seq
    assert seq % MOBA_BLOCK == 0
    nb = seq // MOBA_BLOCK
    topn = min(MOBA_TOPK, nb)
    hd = MOBA_HD
    qmap = lambda b, h, qi: (b * nb + qi, h)
    return pl.pallas_call(
        functools.partial(_moba_prompt_kernel, nb=nb, topn=topn),
        grid=(nbatch, MOBA_HEADS, nb),
        in_specs=[pl.BlockSpec((MOBA_BLOCK, hd), qmap),
                  pl.BlockSpec((seq, hd), lambda b, h, qi: (b, h)),
                  pl.BlockSpec((seq, hd), lambda b, h, qi: (b, h)),
                  pl.BlockSpec((None, nb, hd), lambda b, h, qi: (b, 0, h))],
        out_specs=pl.BlockSpec((MOBA_BLOCK, hd), qmap),
        out_shape=jax.ShapeDtypeStruct((m, MOBA_HEADS * hd), BF16),
        scratch_shapes=[pltpu.VMEM((MOBA_BLOCK, 1), F32), pltpu.VMEM((MOBA_BLOCK, 1), F32),
                        pltpu.VMEM((MOBA_BLOCK, hd), F32)],
        compiler_params=_cparams("parallel", "parallel", "arbitrary"),
        name="moba_prompt",
    )(qr, kb, vb, km)


def _moba_sample_kernel(pt_ref, q_ref, kn_ref, vn_ref, *rest, npg, ng, nb, topn, valid):
    k_pages, v_pages = rest[:npg], rest[npg:2 * npg]
    o_ref, qrow_ref, s_ref, p_ref, km_ref, acc_ref, pown_ref, l_ref = rest[2 * npg:]
    g = pl.program_id(1)
    sp = q_ref.shape[0]
    hw = MOBA_HEADS * MOBA_HD
    ppb = MOBA_BLOCK // PAGE_SIZE
    nbs = npg // ppb
    scale = MOBA_HD ** -0.5

    def query_rows():
        qt = jnp.concatenate([q_ref[...]] * (LANES // sp), axis=0)
        rh = lax.broadcasted_iota(jnp.int32, (LANES, hw), 0) // sp
        lh = lax.broadcasted_iota(jnp.int32, (LANES, hw), 1) // MOBA_HD
        return jnp.where(rh == lh, qt, 0.0)

    def page_heads(ref):
        x = ref[...]
        return [x[:, h, :] for h in range(MOBA_HEADS)]

    @pl.when(g == 0)
    def _():
        qrow_ref[...] = (query_rows() * scale).astype(BF16)
        km_ref[...] = jnp.zeros_like(km_ref)
        acc_ref[...] = jnp.zeros_like(acc_ref)

    @pl.when(g < ng)
    def _():
        qrow = qrow_ref[...]
        base = pl.multiple_of(g * (npg * PAGE_SIZE), npg * PAGE_SIZE)
        sums = []
        for j in range(npg):
            xh = page_heads(k_pages[j])
            kall = jnp.concatenate([t.astype(BF16) for t in xh], axis=1)
            s_ref[pl.ds(base + j * PAGE_SIZE, PAGE_SIZE), :] = _dot_nt(kall, qrow)
            sums.append(jnp.concatenate([jnp.sum(t, axis=0, keepdims=True) for t in xh], axis=1))
        blk_row = lax.broadcasted_iota(jnp.int32, (nb, 1), 0)
        km = km_ref[...]
        for jb in range(nbs):
            mean = sum(sums[jb * ppb:(jb + 1) * ppb]) * (1.0 / MOBA_BLOCK)
            km = km + jnp.where(blk_row == g * nbs + jb, mean, 0.0)
        km_ref[...] = km

    @pl.when(g == ng - 1)
    def _():
        b_idx = lax.broadcasted_iota(jnp.int32, (nb, LANES), 0)
        sel = _top_rows(_dot_nt_f32(km_ref[...], query_rows()), b_idx, topn, nb, 0)
        s_own = _dot_nt(kn_ref[...].astype(BF16), qrow_ref[...])
        kt = lax.broadcasted_iota(jnp.int32, (sp, LANES), 0)
        qt = lax.broadcasted_iota(jnp.int32, (sp, LANES), 1) % sp
        s_own = jnp.where(jnp.logical_and(kt <= qt, kt < valid), s_own, NEG_INF)

        def masked(n):
            return jnp.where(sel[n:n + 1, :] > 0.0, s_ref[n * MOBA_BLOCK:(n + 1) * MOBA_BLOCK, :], NEG_INF)

        mx = jnp.max(s_own, axis=0, keepdims=True)
        for n in range(nb):
            mx = jnp.maximum(mx, jnp.max(masked(n), axis=0, keepdims=True))
        p_own = jnp.exp(s_own - mx)
        l = jnp.sum(p_own, axis=0, keepdims=True)
        for n in range(nb):
            pn = jnp.exp(masked(n) - mx)
            l = l + jnp.sum(pn, axis=0, keepdims=True)
            p_ref[n * MOBA_BLOCK:(n + 1) * MOBA_BLOCK, :] = pn.astype(BF16)
        pown_ref[...] = jnp.concatenate([p_own, jnp.zeros((LANES - sp, LANES), F32)], axis=0).astype(BF16)
        l_ref[...] = jnp.broadcast_to(l, l_ref.shape)

    @pl.when(g >= ng)
    def _():
        base = pl.multiple_of((g - ng) * (npg * PAGE_SIZE), npg * PAGE_SIZE)
        acc = acc_ref[...]
        for j in range(npg):
            vall = jnp.concatenate([t.astype(BF16) for t in page_heads(v_pages[j])], axis=1)
            acc = acc + _dot_tn(p_ref[pl.ds(base + j * PAGE_SIZE, PAGE_SIZE), :], vall)
        acc_ref[...] = acc

    @pl.when(g == 2 * ng - 1)
    def _():
        vn = jnp.concatenate([vn_ref[...], jnp.zeros((LANES - sp, hw), F32)], axis=0).astype(BF16)
        acc = acc_ref[...] + _dot_tn(pown_ref[...], vn)
        o = acc / l_ref[...].T[:, :1]
        o_ref[...] = jnp.concatenate(
            [o[h * sp:(h + 1) * sp, h * MOBA_HD:(h + 1) * MOBA_HD] for h in range(MOBA_HEADS)], axis=1
        ).astype(o_ref.dtype)


def _moba_sample(qr, kr, v, cache_k, cache_v, layer, page_table, valid):
    m, hw = qr.shape
    db, n_pages = page_table.shape
    sp = m // db
    past = n_pages * PAGE_SIZE
    assert past % MOBA_BLOCK == 0
    nb = past // MOBA_BLOCK
    topn = min(MOBA_TOPK, nb)
    ppb = MOBA_BLOCK // PAGE_SIZE
    npg = math.gcd(n_pages, MOBA_PAGES_PER_STEP)
    assert npg % ppb == 0
    ng = n_pages // npg
    qmap = lambda b, g, pt: (b, 0)
    blk = (None, None, PAGE_SIZE, MOBA_HEADS, MOBA_HD)

    def k_spec(j):
        return pl.BlockSpec(blk, lambda b, g, pt: (layer, pt[b * n_pages + jnp.minimum(g, ng - 1) * npg + j], 0, 0, 0))

    def v_spec(j):
        return pl.BlockSpec(blk, lambda b, g, pt: (layer, pt[b * n_pages + jnp.maximum(g - ng, 0) * npg + j], 0, 0, 0))

    grid_spec = pltpu.PrefetchScalarGridSpec(
        num_scalar_prefetch=1,
        grid=(db, 2 * ng),
        in_specs=[pl.BlockSpec((sp, hw), qmap)] * 3 + [k_spec(j) for j in range(npg)] + [v_spec(j) for j in range(npg)],
        out_specs=pl.BlockSpec((sp, hw), qmap),
        scratch_shapes=[pltpu.VMEM((LANES, hw), BF16), pltpu.VMEM((past, LANES), F32), pltpu.VMEM((past, LANES), BF16),
                        pltpu.VMEM((nb, hw), F32), pltpu.VMEM((LANES, hw), F32), pltpu.VMEM((LANES, LANES), BF16),
                        pltpu.VMEM((SUBLANES, LANES), F32)],
    )
    return pl.pallas_call(
        functools.partial(_moba_sample_kernel, npg=npg, ng=ng, nb=nb, topn=topn, valid=valid),
        grid_spec=grid_spec,
        out_shape=jax.ShapeDtypeStruct((m, hw), BF16),
        compiler_params=_cparams("parallel", "arbitrary"),
        name="moba_sample",
    )(page_table.reshape(-1), qr, kr, v, *([cache_k] * npg), *([cache_v] * npg))


def _s5_disc_kernel(lr_ref, li_ref, dt_ref, bre_ref, bim_ref, pwr_ref, pwi_ref, bbr_ref, bbi_ref):
    lr, li, dt = lr_ref[...], li_ref[...], dt_ref[...]
    k = (lax.broadcasted_iota(jnp.int32, pwr_ref.shape, 0) + 1).astype(F32)
    mag = jnp.exp(k * (lr * dt))
    ang = k * (li * dt)
    pwr = mag * jnp.cos(ang)
    pwi = mag * jnp.sin(ang)
    pwr_ref[...] = pwr
    pwi_ref[...] = pwi
    nr, ni = pwr[0:1] - 1.0, pwi[0:1]
    den = lr * lr + li * li
    f_re = (nr * lr + ni * li) / den
    f_im = (ni * lr - nr * li) / den
    bre, bim = bre_ref[...], bim_ref[...]
    bbr_ref[...] = f_re * bre - f_im * bim
    bbi_ref[...] = f_re * bim + f_im * bre


def _s5_discretize(lam_re, lam_im, log_dt, b_re, b_im):
    n = S5_N
    flat = lambda a: a.astype(F32).reshape(1, n)
    dt = jnp.exp(jnp.repeat(log_dt.astype(F32), S5_STATE)).reshape(1, n)
    bt = lambda b: jnp.transpose(b.astype(F32), (2, 0, 1)).reshape(S5_GROUP, n)
    args = (flat(lam_re), flat(lam_im), dt, bt(b_re), bt(b_im))
    return pl.pallas_call(
        _s5_disc_kernel,
        in_specs=[_full(a) for a in args],
        out_shape=[jax.ShapeDtypeStruct((SUBLANES, n), F32)] * 2 + [jax.ShapeDtypeStruct((S5_GROUP, n), F32)] * 2,
        name="s5_discretize",
    )(*args)


def _s5_block_diag(pieces):
    r, g, c = pieces.shape
    eye = jnp.eye(g, dtype=pieces.dtype)
    return jnp.einsum('rgc,gh->grhc', pieces, eye).reshape(g * r, g * c)


def _s5_kernel(u_ref, bb_ref, pwr_ref, pwi_ref, cc_ref, d_ref, wg_ref, bg_ref, *rest, chain):
    n = S5_N
    u = u_ref[...]
    tt = u.shape[0]
    ngrp = tt // SUBLANES
    bu = _dot(u.astype(BF16), bb_ref[...])
    xr = bu[:, :n].reshape(ngrp, SUBLANES, n)
    xi = bu[:, n:].reshape(ngrp, SUBLANES, n)
    row = lax.broadcasted_iota(jnp.int32, (ngrp, SUBLANES, n), 1)
    for d in (1, 2, 4):
        pr, pi = pwr_ref[d - 1:d, :], pwi_ref[d - 1:d, :]
        sr, si = pltpu.roll(xr, d, 1), pltpu.roll(xi, d, 1)
        keep = row >= d
        xr, xi = (xr + jnp.where(keep, pr * sr - pi * si, 0.0),
                  xi + jnp.where(keep, pr * si + pi * sr, 0.0))
    pw_r, pw_i = pwr_ref[...], pwi_ref[...]

    if chain:
        o_ref, sr_ref, si_ref, xr_s, xi_s, cr_s, ci_s = rest
        t = pl.program_id(1)

        @pl.when(t == 0)
        def _():
            cr_s[...] = jnp.zeros_like(cr_s)
            ci_s[...] = jnp.zeros_like(ci_s)

        xr_s[...] = xr
        xi_s[...] = xi

        def body(i, carry):
            cr, ci = carry
            a = xr_s[i] + pw_r * cr - pw_i * ci
            b = xi_s[i] + pw_r * ci + pw_i * cr
            xr_s[i] = a
            xi_s[i] = b
            return a[SUBLANES - 1:SUBLANES], b[SUBLANES - 1:SUBLANES]

        cr, ci = lax.fori_loop(0, ngrp, body, (cr_s[...], ci_s[...]))
        cr_s[...] = cr
        ci_s[...] = ci
        xr, xi = xr_s[...], xi_s[...]

        @pl.when(t == pl.num_programs(1) - 1)
        def _():
            sr_ref[...] = cr
            si_ref[...] = ci
    else:
        s0r_ref, s0i_ref, o_ref, xr_ref, xi_ref = rest
        s0r = s0r_ref[...].reshape(ngrp, SUBLANES, n)
        s0i = s0i_ref[...].reshape(ngrp, SUBLANES, n)
        xr, xi = xr + pw_r * s0r - pw_i * s0i, xi + pw_r * s0i + pw_i * s0r
        xr_ref[...] = xr.reshape(tt, n)
        xi_ref[...] = xi.reshape(tt, n)

    xcat = jnp.concatenate([xr.reshape(tt, n), xi.reshape(tt, n)], axis=1).astype(BF16)
    y = _dot(xcat, cc_ref[...]) + d_ref[...] * u
    y = y * (0.5 * (1.0 + jnp.tanh(math.sqrt(2.0 / math.pi) * (y + 0.044715 * (y * y * y)))))
    z = _dot(y.astype(BF16), wg_ref[...]) + bg_ref[...]
    o_ref[...] = (y * _sigmoid(z)).astype(o_ref.dtype)


def _s5_prompt(u, bb, pwr, pwi, cc, d, wg, bg, seq):
    m, w = u.shape
    nbatch = m // seq
    tt = _row_tile(seq, 256)
    nt = seq // tt
    n = S5_N
    row = pl.BlockSpec((tt, w), lambda b, t: (b * nt + t, 0))
    st = pl.BlockSpec((None, 1, n), lambda b, t: (b, 0, 0))
    return pl.pallas_call(
        functools.partial(_s5_kernel, chain=True),
        grid=(nbatch, nt),
        in_specs=[row, _full(bb), _full(pwr), _full(pwi), _full(cc), _full(d), _full(wg), _full(bg)],
        out_specs=[row, st, st],
        out_shape=[jax.ShapeDtypeStruct((m, w), BF16), jax.ShapeDtypeStruct((nbatch, 1, n), F32),
                   jax.ShapeDtypeStruct((nbatch, 1, n), F32)],
        scratch_shapes=[pltpu.VMEM((tt // SUBLANES, SUBLANES, n), F32), pltpu.VMEM((tt // SUBLANES, SUBLANES, n), F32),
                        pltpu.VMEM((1, n), F32), pltpu.VMEM((1, n), F32)],
        compiler_params=_cparams("parallel", "arbitrary"),
        name="s5_prompt",
    )(u, bb, pwr, pwi, cc, d, wg, bg)


def _s5_sample(u, bb, pwr, pwi, cc, d, wg, bg, s0r, s0i):
    m, w = u.shape
    n = S5_N
    tt = _row_tile(m, 256)
    row = pl.BlockSpec((tt, w), lambda i: (i, 0))
    st = pl.BlockSpec((tt, n), lambda i: (i, 0))
    return pl.pallas_call(
        functools.partial(_s5_kernel, chain=False),
        grid=(m // tt,),
        in_specs=[row, _full(bb), _full(pwr), _full(pwi), _full(cc), _full(d), _full(wg), _full(bg), st, st],
        out_specs=[row, st, st],
        out_shape=[jax.ShapeDtypeStruct((m, w), BF16), jax.ShapeDtypeStruct((m, n), F32),
                   jax.ShapeDtypeStruct((m, n), F32)],
        compiler_params=_cparams("parallel"),
        name="s5_sample",
    )(u, bb, pwr, pwi, cc, d, wg, bg, s0r, s0i)


def kernel(x_prompt, x_sample, cache_mem_k, cache_mem_v, state_gla, cache_mla_ckv, cache_mla_kpe, cache_moba_k, cache_moba_v, state_s5, page_table, mem_prompt, g_ffn1, w_ffn1_gate, w_ffn1_up, w_ffn1_down, g_mix, w_in_a, w_gla_gate, b_gla_gate, g_gla_norm, g_mla_q, w_mla_qb, g_mla_kv, w_mla_kvb, w_out_a, w_in_b, s5_lam_re, s5_lam_im, s5_log_dt, s5_b_re, s5_b_im, s5_c_re, s5_c_im, s5_d, w_s5_glu, b_s5_glu, w_out_b, g_xattn, w_xq, w_xk, w_xv, w_xo, g_ffn2, w_ffn2_gate, w_ffn2_up, w_ffn2_down, g_final):
    B, T, D = x_prompt.shape
    DB, S, _ = x_sample.shape
    depth = g_ffn1.shape[0]
    n_pages = page_table.shape[1]
    past = n_pages * PAGE_SIZE
    SP = SAMPLE_PAD
    assert S <= SP
    n_mem = mem_prompt.shape[1]
    bf = lambda a: a.astype(BF16)

    xp = x_prompt.reshape(B * T, D)
    xs = jnp.pad(x_sample, ((0, 0), (0, SP - S), (0, 0))).reshape(DB * SP, D)
    pos_p = jnp.arange(T, dtype=jnp.int32)
    pos_s = jnp.tile(past + jnp.arange(SP, dtype=jnp.int32), DB)
    mla_tab_p = _rope_tables(pos_p, MLA_ROPE, LANES, MLA_HEADS)
    mla_tab_s = _rope_tables(pos_s, MLA_ROPE, LANES, MLA_HEADS)
    moba_tab_p = _rope_tables(pos_p, MOBA_HD, MOBA_HD, MOBA_HEADS)
    moba_tab_s = _rope_tables(pos_s, MOBA_HD, MOBA_HD, MOBA_HEADS)
    mem2d = mem_prompt.reshape(B * n_mem, D)
    ones_d = jnp.ones((D,), F32)
    gla_chunk = math.gcd(T, GLA_CHUNK)

    def unpad(a):
        return a.reshape((DB, SP) + a.shape[1:])[:, :S]

    mem_k_p, mem_v_p = [], []
    gla_p, gla_s = [], []
    ckv_p, kpe_p, ckv_s, kpe_s = [], [], [], []
    mk_p, mv_p, mk_s, mv_s = [], [], [], []
    s5_p, s5_s = [], []

    for l in range(depth):
        i = l // 2
        wg1, wu1, wd1 = bf(w_ffn1_gate[l]), bf(w_ffn1_up[l]), bf(w_ffn1_down[l])
        xp = _ffn(xp, g_ffn1[l], wg1, wu1, wd1)
        xs = _ffn(xs, g_ffn1[l], wg1, wu1, wd1)

        if l % 2 == 0:
            qa_w, ka_w, va_w, ga_w, ra_w, ql_w, kvl_w, kr_w = jnp.split(
                w_in_a[i], [256, 512, 1024, 1040, 1552, 1936, 2192], axis=1)
            padc = lambda w: jnp.pad(w, ((0, 0), (0, LANES - w.shape[1])))
            w_in = bf(jnp.concatenate([va_w, ra_w, qa_w, ka_w, kvl_w, ql_w, padc(kr_w), padc(ga_w)], axis=1))
            widths = (512, 512, 256, 256, 256, 384, LANES, LANES)
            wgate = bf(jnp.pad(w_gla_gate[i], ((0, LANES - GLA_GATE_RANK), (0, 0))))
            bgate = b_gla_gate[i].reshape(1, -1)
            gnorm = g_gla_norm[i].reshape(1, -1)
            wqb = w_mla_qb[i].reshape(MLA_Q_LORA, MLA_HEADS, MLA_NOPE + MLA_ROPE)
            wqn = bf(wqb[:, :, :MLA_NOPE].reshape(MLA_Q_LORA, MLA_HEADS * MLA_NOPE))
            wqp = bf(jnp.pad(wqb[:, :, MLA_NOPE:], ((0, 0), (0, 0), (0, LANES - MLA_ROPE))).reshape(MLA_Q_LORA, MLA_HEADS * LANES))
            wkvb = w_mla_kvb[i].reshape(MLA_KV_LORA, MLA_HEADS, MLA_NOPE + MLA_V)
            wuk = bf(jnp.transpose(wkvb[:, :, :MLA_NOPE], (1, 2, 0)))
            wuv = bf(jnp.transpose(wkvb[:, :, MLA_NOPE:], (1, 0, 2)))
            gq = g_mla_q[i].reshape(1, -1)
            gkv = g_mla_kv[i].reshape(1, -1)
            wo = bf(w_out_a[i])
            wo_a, wo_b = wo[:GLA_HEADS * GLA_DV], wo[GLA_HEADS * GLA_DV:]

            def even(x, tabs, s0, prompt):
                va, ra, qa, ka, kvl, ql, ksl, gsl = _proj(x, g_mix[l], w_in, widths)
                if prompt:
                    o_a, st = _gla(qa, ka, va, gsl, ra, s0, wgate, bgate, gnorm,
                                   seq=T, rows=gla_chunk, chunk=gla_chunk, valid=gla_chunk)
                else:
                    o_a, st = _gla(qa, ka, va, gsl, ra, s0, wgate, bgate, gnorm,
                                   seq=SP, rows=SP, chunk=2 * SP, valid=S)
                q_abs, q_pe, ckv, kpe, ckvb, kpeb = _mla_prep(ql, kvl, ksl, tabs[0], tabs[1], gq, gkv, wqn, wqp, wuk)
                if prompt:
                    o_b = _mla_prompt(q_abs, q_pe, ckvb, kpeb, wuv, T)
                else:
                    o_b = _mla_sample(q_abs, q_pe, ckvb, kpeb, wuv, cache_mla_ckv, cache_mla_kpe, i, page_table, S)
                return _outproj(o_a, o_b, wo_a, wo_b, x), st, ckv, kpe

            xp, st, ckv, kpe = even(xp, mla_tab_p, jnp.zeros((B, GLA_HEADS, GLA_DK, GLA_DV), F32), True)
            gla_p.append(st)
            ckv_p.append(ckv.reshape(B, T, MLA_KV_LORA))
            kpe_p.append(kpe.reshape(B, T, MLA_ROPE))
            xs, st, ckv, kpe = even(xs, mla_tab_s, state_gla[i], False)
            gla_s.append(st)
            ckv_s.append(unpad(ckv))
            kpe_s.append(unpad(kpe))
        else:
            hw = MOBA_HEADS * MOBA_HD
            w_in = bf(w_in_b[i])
            widths = (hw, hw, hw, S5_WIDTH)
            pwr, pwi, bbr, bbi = _s5_discretize(s5_lam_re[i], s5_lam_im[i], s5_log_dt[i], s5_b_re[i], s5_b_im[i])
            to_blocks = lambda a: a.reshape(a.shape[0], S5_GROUPS, S5_STATE)
            bb = bf(jnp.concatenate([_s5_block_diag(to_blocks(bbr)), _s5_block_diag(to_blocks(bbi))], axis=1))
            c_t = lambda c: jnp.transpose(c.astype(F32), (2, 0, 1))
            cc = bf(jnp.concatenate([_s5_block_diag(c_t(s5_c_re[i])), -_s5_block_diag(c_t(s5_c_im[i]))], axis=0))
            dsk = s5_d[i].reshape(1, -1)
            wglu = bf(w_s5_glu[i])
            bglu = b_s5_glu[i].reshape(1, -1)
            wo = bf(w_out_b[i])
            wo_c, wo_d = wo[:hw], wo[hw:]

            q, k, v, u = _proj(xp, g_mix[l], w_in, widths)
            qr, kr, kb, vb, km = _moba_prep(q, k, v, moba_tab_p[0], moba_tab_p[1], True)
            o_c = _moba_prompt(qr, kb, vb, km.reshape(B, T // MOBA_BLOCK, hw), T)
            o_d, sr, si = _s5_prompt(u, bb, pwr, pwi, cc, dsk, wglu, bglu, T)
            xp = _outproj(o_c, o_d, wo_c, wo_d, xp)
            mk_p.append(kr.reshape(B, T, MOBA_HEADS, MOBA_HD))
            mv_p.append(v.reshape(B, T, MOBA_HEADS, MOBA_HD))
            s5_p.append(jnp.stack([sr.reshape(B, S5_GROUPS, S5_STATE), si.reshape(B, S5_GROUPS, S5_STATE)], axis=-1))

            q, k, v, u = _proj(xs, g_mix[l], w_in, widths)
            qr, kr, _, _ = _moba_prep(q, k, v, moba_tab_s[0], moba_tab_s[1], False)
            o_c = _moba_sample(qr, kr, v, cache_moba_k, cache_moba_v, i, page_table, S)
            s0 = state_s5[i].astype(F32).reshape(DB, S5_N, 2)
            s0r = jnp.repeat(s0[..., 0], SP, axis=0)
            s0i = jnp.repeat(s0[..., 1], SP, axis=0)
            o_d, xr, xi = _s5_sample(u, bb, pwr, pwi, cc, dsk, wglu, bglu, s0r, s0i)
            xs = _outproj(o_c, o_d, wo_c, wo_d, xs)
            mk_s.append(unpad(kr).reshape(DB, S, MOBA_HEADS, MOBA_HD))
            mv_s.append(unpad(v).reshape(DB, S, MOBA_HEADS, MOBA_HD))
            last = lambda a: a.reshape(DB, SP, S5_GROUPS, S5_STATE)[:, S - 1]
            s5_s.append(jnp.stack([last(xr), last(xi)], axis=-1))

        mkp, mvp = _proj(mem2d, ones_d, bf(jnp.concatenate([w_xk[l], w_xv[l]], axis=1)), (D, D), norm=False)
        mem_k_p.append(mkp.reshape(B, n_mem, X_HEADS, D // X_HEADS))
        mem_v_p.append(mvp.reshape(B, n_mem, X_HEADS, D // X_HEADS))
        wq, wo = bf(w_xq[l]), bf(w_xo[l])
        xp = _xattn_prompt(xp, g_xattn[l], wq, wo, mkp.reshape(B, n_mem, D), mvp.reshape(B, n_mem, D), T)
        xs = _xattn_sample(xs, g_xattn[l], wq, wo, cache_mem_k, cache_mem_v, l)

        wg2, wu2, wd2 = bf(w_ffn2_gate[l]), bf(w_ffn2_up[l]), bf(w_ffn2_down[l])
        xp = _ffn(xp, g_ffn2[l], wg2, wu2, wd2)
        xs = _ffn(xs, g_ffn2[l], wg2, wu2, wd2)

    y_prompt = _final_norm(xp, g_final).reshape(B, T, D)
    y_sample = unpad(_final_norm(xs, g_final))
    return (y_prompt, y_sample,
            jnp.stack(mem_k_p), jnp.stack(mem_v_p),
            jnp.stack(gla_p), jnp.stack(gla_s),
            jnp.stack(ckv_p), jnp.stack(kpe_p), jnp.stack(ckv_s), jnp.stack(kpe_s),
            jnp.stack(mk_p), jnp.stack(mv_p), jnp.stack(mk_s), jnp.stack(mv_s),
            jnp.stack(s5_p), jnp.stack(s5_s))
```

```python
import functools
import math

import jax
import jax.numpy as jnp
from jax import lax
from jax.experimental import pallas as pl
from jax.experimental.pallas import tpu as pltpu

F32 = jnp.float32
BF16 = jnp.bfloat16

RMS_EPS = 1e-6
ROPE_THETA = 10000.0
NEG_INF = -1e30
BELOW_NEG_INF = -3.0e38
PAGE_SIZE = 128
GLA_HEADS, GLA_DK, GLA_DV, GLA_GATE_RANK, GLA_TAU, GLA_CHUNK = 4, 64, 128, 16, 16.0, 64
MLA_HEADS, MLA_Q_LORA, MLA_KV_LORA, MLA_NOPE, MLA_ROPE, MLA_V = 4, 384, 256, 128, 64, 128
MOBA_HEADS, MOBA_HD, MOBA_BLOCK, MOBA_TOPK = 4, 128, 256, 3
S5_WIDTH, S5_GROUP, S5_STATE = 512, 16, 64
S5_GROUPS = S5_WIDTH // S5_GROUP
S5_N = S5_GROUPS * S5_STATE
X_HEADS = 4
LANES = 128
SUBLANES = 8
SAMPLE_PAD = SUBLANES
VMEM_LIMIT = 56 * 1024 * 1024
MLA_PAGES_PER_STEP = 32
MOBA_PAGES_PER_STEP = 8
LOG2E = math.log2(math.e)


def _cparams(*sem):
    return pltpu.CompilerParams(dimension_semantics=sem, vmem_limit_bytes=VMEM_LIMIT)


def _row_tile(m, target):
    t = min(m, target)
    while m % t or t % SUBLANES:
        t -= 1
    return t


def _rms(x, g):
    return x * lax.rsqrt(jnp.mean(x * x, axis=-1, keepdims=True) + RMS_EPS) * g


def _sigmoid(x):
    return 1.0 / (1.0 + jnp.exp(-x))


def _dot(a, b):
    return jnp.dot(a, b, preferred_element_type=F32)


def _dot_nt(a, b):
    return lax.dot_general(a, b, (((1,), (1,)), ((), ())), preferred_element_type=F32)


def _dot_nt_f32(a, b):
    return lax.dot_general(a, b, (((1,), (1,)), ((), ())), precision=lax.Precision.HIGHEST,
                           preferred_element_type=F32)


def _dot_tn(a, b):
    return lax.dot_general(a, b, (((0,), (0,)), ((), ())), preferred_element_type=F32)


def _full(a):
    return pl.BlockSpec(a.shape, lambda *_: (0,) * a.ndim)


def _ffn_kernel(x_ref, g_ref, wg_ref, wu_ref, wd_ref, o_ref, h_ref):
    j = pl.program_id(1)

    @pl.when(j == 0)
    def _():
        h_ref[...] = _rms(x_ref[...], g_ref[...]).astype(BF16)
        o_ref[...] = jnp.zeros_like(o_ref)

    h = h_ref[...]
    a = _dot(h, wg_ref[...])
    z = a * _sigmoid(a) * _dot(h, wu_ref[...])
    o_ref[...] += _dot(z.astype(BF16), wd_ref[...])

    @pl.when(j == pl.num_programs(1) - 1)
    def _():
        o_ref[...] = x_ref[...] + 0.5 * o_ref[...]


def _ffn(x, g, wg, wu, wd):
    m, d = x.shape
    f = wg.shape[1]
    tm = _row_tile(m, 1024)
    tf = 256 if f % 256 == 0 else f
    return pl.pallas_call(
        _ffn_kernel,
        grid=(m // tm, f // tf),
        in_specs=[
            pl.BlockSpec((tm, d), lambda i, j: (i, 0)),
            pl.BlockSpec((1, d), lambda i, j: (0, 0)),
            pl.BlockSpec((d, tf), lambda i, j: (0, j)),
            pl.BlockSpec((d, tf), lambda i, j: (0, j)),
            pl.BlockSpec((tf, d), lambda i, j: (j, 0)),
        ],
        out_specs=pl.BlockSpec((tm, d), lambda i, j: (i, 0)),
        out_shape=jax.ShapeDtypeStruct((m, d), F32),
        scratch_shapes=[pltpu.VMEM((tm, d), BF16)],
        compiler_params=_cparams("parallel", "arbitrary"),
        name="ffn",
    )(x, g.reshape(1, d), wg, wu, wd)


def _proj_kernel(x_ref, g_ref, w_ref, *o_refs, widths, norm):
    x = x_ref[...]
    h = _rms(x, g_ref[...]) if norm else x
    y = _dot(h.astype(BF16), w_ref[...])
    off = 0
    for o_ref, wd in zip(o_refs, widths):
        o_ref[...] = y[:, off:off + wd].astype(o_ref.dtype)
        off += wd


def _proj(x, g, w, widths, norm=True):
    m, d = x.shape
    n = w.shape[1]
    assert sum(widths) == n
    tm = _row_tile(m, 512)
    return pl.pallas_call(
        functools.partial(_proj_kernel, widths=widths, norm=norm),
        grid=(m // tm,),
        in_specs=[pl.BlockSpec((tm, d), lambda i: (i, 0)), pl.BlockSpec((1, d), lambda i: (0, 0)), _full(w)],
        out_specs=[pl.BlockSpec((tm, wd), lambda i: (i, 0)) for wd in widths],
        out_shape=[jax.ShapeDtypeStruct((m, wd), F32) for wd in widths],
        compiler_params=_cparams("parallel"),
        name="proj",
    )(x, g.reshape(1, d), w)


def _outproj_kernel(a_ref, b_ref, wa_ref, wb_ref, x_ref, o_ref):
    o_ref[...] = x_ref[...] + _dot(a_ref[...], wa_ref[...]) + _dot(b_ref[...], wb_ref[...])


def _outproj(a, b, wa, wb, x):
    m, d = x.shape
    tm = _row_tile(m, 512)
    return pl.pallas_call(
        _outproj_kernel,
        grid=(m // tm,),
        in_specs=[
            pl.BlockSpec((tm, a.shape[1]), lambda i: (i, 0)),
            pl.BlockSpec((tm, b.shape[1]), lambda i: (i, 0)),
            _full(wa), _full(wb),
            pl.BlockSpec((tm, d), lambda i: (i, 0)),
        ],
        out_specs=pl.BlockSpec((tm, d), lambda i: (i, 0)),
        out_shape=jax.ShapeDtypeStruct((m, d), F32),
        compiler_params=_cparams("parallel"),
        name="outproj",
    )(a, b, wa, wb, x)


def _norm_kernel(x_ref, g_ref, o_ref):
    o_ref[...] = _rms(x_ref[...], g_ref[...])


def _final_norm(x, g):
    m, d = x.shape
    tm = _row_tile(m, 1024)
    return pl.pallas_call(
        _norm_kernel,
        grid=(m // tm,),
        in_specs=[pl.BlockSpec((tm, d), lambda i: (i, 0)), pl.BlockSpec((1, d), lambda i: (0, 0))],
        out_specs=pl.BlockSpec((tm, d), lambda i: (i, 0)),
        out_shape=jax.ShapeDtypeStruct((m, d), F32),
        compiler_params=_cparams("parallel"),
        name="final_norm",
    )(x, g.reshape(1, d))


def _xattn_core(x, g, wq, wo, mem_k, mem_v, seq_rows):
    d = x.shape[1]
    hd = d // X_HEADS
    q = (_dot(_rms(x, g).astype(BF16), wq) * (hd ** -0.5)).astype(BF16)
    outs = []
    for s in range(x.shape[0] // seq_rows):
        qs = q[s * seq_rows:(s + 1) * seq_rows]
        heads = []
        for h in range(X_HEADS):
            sc = _dot_nt(qs[:, h * hd:(h + 1) * hd], mem_k(s, h))
            p = jnp.exp(sc - jnp.max(sc, axis=-1, keepdims=True))
            p = p / jnp.sum(p, axis=-1, keepdims=True)
            heads.append(_dot(p.astype(BF16), mem_v(s, h)))
        outs.append(jnp.concatenate(heads, axis=-1))
    o = outs[0] if len(outs) == 1 else jnp.concatenate(outs, axis=0)
    return x + _dot(o.astype(BF16), wo)


def _xattn_prompt_kernel(x_ref, g_ref, wq_ref, wo_ref, mk_ref, mv_ref, o_ref):
    x = x_ref[...]
    hd = x.shape[1] // X_HEADS
    mk = mk_ref[...].astype(BF16)
    mv = mv_ref[...].astype(BF16)
    o_ref[...] = _xattn_core(x, g_ref[...], wq_ref[...], wo_ref[...],
                             lambda s, h: mk[:, h * hd:(h + 1) * hd], lambda s, h: mv[:, h * hd:(h + 1) * hd],
                             x.shape[0])


def _xattn_prompt(x, g, wq, wo, mk, mv, seq):
    m, d = x.shape
    nm = mk.shape[1]
    tm = _row_tile(seq, 512)
    nt = seq // tm
    return pl.pallas_call(
        _xattn_prompt_kernel,
        grid=(m // tm,),
        in_specs=[
            pl.BlockSpec((tm, d), lambda i: (i, 0)),
            pl.BlockSpec((1, d), lambda i: (0, 0)),
            _full(wq), _full(wo),
            pl.BlockSpec((None, nm, d), lambda i: (i // nt, 0, 0)),
            pl.BlockSpec((None, nm, d), lambda i: (i // nt, 0, 0)),
        ],
        out_specs=pl.BlockSpec((tm, d), lambda i: (i, 0)),
        out_shape=jax.ShapeDtypeStruct((m, d), F32),
        compiler_params=_cparams("parallel"),
        name="xattn_prompt",
    )(x, g.reshape(1, d), wq, wo, mk, mv)


def _xattn_sample_kernel(x_ref, g_ref, wq_ref, wo_ref, mk_ref, mv_ref, o_ref, *, nseq):
    nc = x_ref.shape[1] // X_HEADS // LANES
    nm = mk_ref.shape[1] // (X_HEADS * nc)

    def head(ref, s, h):
        tiles = [ref[s, pl.ds(h + X_HEADS * c, nm, stride=X_HEADS * nc), :] for c in range(nc)]
        return jnp.concatenate(tiles, axis=1).astype(BF16)

    o_ref[...] = _xattn_core(x_ref[...], g_ref[...], wq_ref[...], wo_ref[...],
                             lambda s, h: head(mk_ref, s, h), lambda s, h: head(mv_ref, s, h),
                             x_ref.shape[0] // nseq)


def _xattn_sample(x, g, wq, wo, cache_k, cache_v, layer):
    m, d = x.shape
    depth, db, nm, nh, hd = cache_k.shape
    assert nh == X_HEADS and hd % LANES == 0
    nc = hd // LANES

    def rows_view(c):
        c = c.reshape(depth, db, nm, nh, nc, LANES)
        return jnp.transpose(c, (0, 1, 2, 4, 3, 5)).reshape(depth, db, nm * nc * nh, LANES)

    cache_k, cache_v = rows_view(cache_k), rows_view(cache_v)
    nseq = 4 if db % 4 == 0 else 1
    tm = nseq * (m // db)
    cache_spec = pl.BlockSpec((None, nseq, nm * nc * nh, LANES), lambda i: (layer, i, 0, 0))
    return pl.pallas_call(
        functools.partial(_xattn_sample_kernel, nseq=nseq),
        grid=(db // nseq,),
        in_specs=[
            pl.BlockSpec((tm, d), lambda i: (i, 0)),
            pl.BlockSpec((1, d), lambda i: (0, 0)),
            _full(wq), _full(wo), cache_spec, cache_spec,
        ],
        out_specs=pl.BlockSpec((tm, d), lambda i: (i, 0)),
        out_shape=jax.ShapeDtypeStruct((m, d), F32),
        compiler_params=_cparams("parallel"),
        name="xattn_sample",
    )(x, g.reshape(1, d), wq, wo, cache_k, cache_v)


def _split3(x):
    hi = x.astype(BF16)
    r = x - hi.astype(F32)
    mid = r.astype(BF16)
    lo = (r - mid.astype(F32)).astype(BF16)
    return hi, mid, lo


def _gla_kernel(q_ref, k_ref, v_ref, gs_ref, ra_ref, s0_ref, wg_ref, bg_ref, gn_ref, o_ref, sout_ref, st_ref,
                *, rows, chunk, nchunks, valid):
    t = pl.program_id(1)

    @pl.when(t == 0)
    def _():
        st_ref[...] = s0_ref[...]

    L = chunk
    r_io = lax.broadcasted_iota(jnp.int32, (L, L), 0)
    c_io = lax.broadcasted_iota(jnp.int32, (L, L), 1)
    tril = r_io >= c_io
    tril_bf = jnp.where(tril, 1.0, 0.0).astype(BF16)
    ones_bf = jnp.ones((L, GLA_DV), BF16)
    live = lax.broadcasted_iota(jnp.int32, (L, 1), 0) < valid

    def load(ref, r0):
        x = ref[pl.ds(r0, rows), :]
        if rows < L:
            x = jnp.concatenate([x, jnp.zeros((L - rows, x.shape[1]), x.dtype)], axis=0)
        return x

    def body(c, carry):
        r0 = pl.multiple_of(c * rows, rows)
        q = load(q_ref, r0) * (GLA_DK ** -0.5)
        k = load(k_ref, r0)
        v = load(v_ref, r0)
        ra = load(ra_ref, r0)
        z = _dot(load(gs_ref, r0).astype(BF16), wg_ref[...]) + bg_ref[...]
        lg = (jnp.minimum(z, 0.0) - jnp.log1p(jnp.exp(-jnp.abs(z)))) * (1.0 / GLA_TAU)
        if valid < L:
            lg = jnp.where(live, lg, 0.0)
            k = jnp.where(live, k, 0.0)
            v = jnp.where(live, v, 0.0)
        hi, mid, lo = _split3(lg)
        b = _dot(tril_bf, hi) + _dot(tril_bf, mid) + _dot(tril_bf, lo)
        bl = b[L - 1:L, :]
        qe = (q * jnp.exp(b)).astype(BF16)
        ke = (k * jnp.exp(-b)).astype(BF16)
        kd = (k * jnp.exp(bl - b)).astype(BF16)
        vb = v.astype(BF16)
        outs = []
        for h in range(GLA_HEADS):
            ks = slice(h * GLA_DK, (h + 1) * GLA_DK)
            vs = slice(h * GLA_DV, (h + 1) * GLA_DV)
            att = jnp.where(tril, _dot_nt(qe[:, ks], ke[:, ks]), 0.0)
            s_h = st_ref[h]
            o_h = _dot(qe[:, ks], s_h.astype(BF16)) + _dot(att.astype(BF16), vb[:, vs])
            dec = _dot_tn(hi[:, ks], ones_bf) + _dot_tn(mid[:, ks], ones_bf) + _dot_tn(lo[:, ks], ones_bf)
            st_ref[h] = s_h * jnp.exp(dec) + _dot_tn(kd[:, ks], vb[:, vs])
            on = _rms(o_h, gn_ref[...])
            rh = ra[:, vs]
            outs.append(on * (rh * _sigmoid(rh)))
        o = jnp.concatenate(outs, axis=-1)
        o_ref[pl.ds(r0, rows), :] = o[:rows].astype(o_ref.dtype)
        return carry

    if nchunks == 1:
        body(0, 0)
    else:
        lax.fori_loop(0, nchunks, body, 0)

    @pl.when(t == pl.num_programs(1) - 1)
    def _():
        sout_ref[...] = st_ref[...]


def _gla(q, k, v, gs, ra, s0, wg, bg, gn, *, seq, rows, chunk, valid):
    m = q.shape[0]
    nb = m // seq
    tt = _row_tile(seq, 512) if rows == chunk else rows
    nchunks = tt // rows
    nt = seq // tt
    row_spec = lambda w: pl.BlockSpec((tt, w), lambda b, t: (b * nt + t, 0))
    st_spec = pl.BlockSpec((None, GLA_HEADS, GLA_DK, GLA_DV), lambda b, t: (b, 0, 0, 0))
    return pl.pallas_call(
        functools.partial(_gla_kernel, rows=rows, chunk=chunk, nchunks=nchunks, valid=valid),
        grid=(nb, nt),
        in_specs=[row_spec(q.shape[1]), row_spec(k.shape[1]), row_spec(v.shape[1]), row_spec(gs.shape[1]),
                  row_spec(ra.shape[1]), st_spec, _full(wg), _full(bg), _full(gn)],
        out_specs=[row_spec(v.shape[1]), st_spec],
        out_shape=[jax.ShapeDtypeStruct((m, v.shape[1]), BF16),
                   jax.ShapeDtypeStruct((nb, GLA_HEADS, GLA_DK, GLA_DV), F32)],
        scratch_shapes=[pltpu.VMEM((GLA_HEADS, GLA_DK, GLA_DV), F32)],
        compiler_params=_cparams("parallel", "arbitrary"),
        name="gla",
    )(q, k, v, gs, ra, s0, wg, bg, gn)


def _rope(x, cos, sin_signed, group):
    w = x.shape[-1]
    ax = x.ndim - 1
    lane = lax.broadcasted_iota(jnp.int32, x.shape, ax)
    first = (lane % group) < (group // 2)
    partner = jnp.where(first, pltpu.roll(x, w - group // 2, ax), pltpu.roll(x, group // 2, ax))
    return x * cos + partner * sin_signed


def _rope_tables(pos, dim, width, reps):
    half = dim // 2
    inv = ROPE_THETA ** (-jnp.arange(half, dtype=F32) / half)
    ang = pos.astype(F32)[:, None] * inv[None, :]
    cos, sin = jnp.cos(ang), jnp.sin(ang)
    pad = jnp.zeros((pos.shape[0], width - dim), F32)
    c = jnp.concatenate([cos, cos, pad], axis=-1)
    s = jnp.concatenate([-sin, sin, pad], axis=-1)
    return jnp.tile(c, (1, reps)), jnp.tile(s, (1, reps))


MLA_QK = MLA_KV_LORA + LANES


def _mla_prep_kernel(ql_ref, kvl_ref, ks_ref, cos_ref, sin_ref, gq_ref, gkv_ref, wqn_ref, wqp_ref, wuk_ref,
                     q_ref, k_ref, ckv_ref, kpe_ref, *ct_ref, transposed):
    scale = (MLA_NOPE + MLA_ROPE) ** -0.5 * LOG2E
    hq = _rms(ql_ref[...], gq_ref[...]).astype(BF16)
    qn = _dot(hq, wqn_ref[...]).astype(BF16)
    cos, sin = cos_ref[...], sin_ref[...]
    qp = _rope(_dot(hq, wqp_ref[...]), cos, sin, MLA_ROPE)
    parts = []
    for h in range(MLA_HEADS):
        parts.append(_dot(qn[:, h * MLA_NOPE:(h + 1) * MLA_NOPE], wuk_ref[h]))
        parts.append(qp[:, h * LANES:(h + 1) * LANES])
    q = jnp.concatenate(parts, axis=-1) * scale
    ckv = _rms(kvl_ref[...], gkv_ref[...])
    ckv_ref[...] = ckv
    kpe = _rope(ks_ref[...], cos[:, :LANES], sin[:, :LANES], MLA_ROPE)
    kpe_ref[...] = kpe[:, :MLA_ROPE]
    k_ref[...] = jnp.concatenate([ckv, kpe], axis=-1).astype(BF16)
    if transposed:
        q_ref[...] = q.T.astype(BF16)
        ct_ref[0][...] = ckv.T.astype(BF16)
    else:
        q_ref[...] = q.astype(BF16)


def _mla_prep(q_lat, kv_lat, ks, cos, sin, gq, gkv, wqn, wqp, wuk, transposed):
    m = q_lat.shape[0]
    p = cos.shape[0]
    tm = _row_tile(math.gcd(m, p), 512)
    npos = p // tm
    row = lambda w: pl.BlockSpec((tm, w), lambda i: (i, 0))
    col = lambda w: pl.BlockSpec((w, tm), lambda i: (0, i))
    pos = lambda w: pl.BlockSpec((tm, w), lambda i: (i % npos, 0))
    hq, hp = MLA_HEADS * MLA_QK, MLA_HEADS * LANES
    out_specs = [col(hq) if transposed else row(hq), row(MLA_QK), row(MLA_KV_LORA), row(MLA_ROPE)]
    out_shape = [jax.ShapeDtypeStruct((hq, m) if transposed else (m, hq), BF16),
                 jax.ShapeDtypeStruct((m, MLA_QK), BF16),
                 jax.ShapeDtypeStruct((m, MLA_KV_LORA), F32), jax.ShapeDtypeStruct((m, MLA_ROPE), F32)]
    if transposed:
        out_specs.append(col(MLA_KV_LORA))
        out_shape.append(jax.ShapeDtypeStruct((MLA_KV_LORA, m), BF16))
    return pl.pallas_call(
        functools.partial(_mla_prep_kernel, transposed=transposed),
        grid=(m // tm,),
        in_specs=[row(MLA_Q_LORA), row(MLA_KV_LORA), row(LANES), pos(hp), pos(hp),
                  _full(gq), _full(gkv), _full(wqn), _full(wqp), _full(wuk)],
        out_specs=out_specs,
        out_shape=out_shape,
        compiler_params=_cparams("parallel"),
        name="mla_prep",
    )(q_lat, kv_lat, ks, cos, sin, gq, gkv, wqn, wqp, wuk)


def _stack_heads(x, width):
    return jnp.concatenate([x[:, h * width:(h + 1) * width] for h in range(MLA_HEADS)], axis=0)


def _softmax_update(s, v, m_ref, l_ref, acc_ref):
    m_prev = m_ref[...]
    m_new = jnp.maximum(m_prev, jnp.max(s, axis=-1, keepdims=True))
    alpha = jnp.exp2(m_prev - m_new)
    p = jnp.exp2(s - m_new)
    l_ref[...] = alpha * l_ref[...] + jnp.sum(p, axis=-1, keepdims=True)
    acc_ref[...] = alpha * acc_ref[...] + _dot(p.astype(BF16), v)
    m_ref[...] = m_new


def _softmax_init(m_ref, l_ref, acc_ref):
    m_ref[...] = jnp.full_like(m_ref, NEG_INF)
    l_ref[...] = jnp.zeros_like(l_ref)
    acc_ref[...] = jnp.zeros_like(acc_ref)


def _mla_out(acc_ref, l_ref, wuv_ref, rows):
    o_lat = (acc_ref[...] / l_ref[...]).astype(BF16)
    return jnp.concatenate([_dot(o_lat[h * rows:(h + 1) * rows], wuv_ref[h]) for h in range(MLA_HEADS)], axis=-1)


def _mla_prompt_kernel(qt_ref, k_ref, ct_ref, wuvt_ref, o_ref, qs_ref, m_ref, l_ref, acc_ref, *, tq, tk):
    qi, ki = pl.program_id(1), pl.program_id(2)
    last = ((qi + 1) * tq - 1) // tk
    cols = MLA_HEADS * tq

    @pl.when(ki == 0)
    def _():
        qt = qt_ref[...]
        qs_ref[...] = jnp.concatenate([qt[h * MLA_QK:(h + 1) * MLA_QK] for h in range(MLA_HEADS)], axis=1)
        _softmax_init(m_ref, l_ref, acc_ref)

    def step(masked):
        st = _dot(k_ref[...], qs_ref[...])
        if masked:
            kpos = ki * tk + lax.broadcasted_iota(jnp.int32, (tk, cols), 0)
            qpos = qi * tq + lax.broadcasted_iota(jnp.int32, (tk, cols), 1) % tq
            st = jnp.where(kpos <= qpos, st, NEG_INF)
        m_prev = m_ref[...]
        m_new = jnp.maximum(m_prev, jnp.max(st, axis=0, keepdims=True))
        alpha = jnp.exp2(m_prev - m_new)
        p = jnp.exp2(st - m_new)
        l_ref[...] = alpha * l_ref[...] + jnp.sum(p, axis=0, keepdims=True)
        acc_ref[...] = alpha * acc_ref[...] + _dot(ct_ref[...], p.astype(BF16))
        m_ref[...] = m_new

    crosses = (ki + 1) * tk - 1 > qi * tq

    @pl.when(jnp.logical_and(ki <= last, crosses))
    def _():
        step(True)

    @pl.when(jnp.logical_and(ki <= last, jnp.logical_not(crosses)))
    def _():
        step(False)

    @pl.when(ki == last)
    def _():
        o_lat = (acc_ref[...] / l_ref[...]).astype(BF16)
        ot = jnp.concatenate([_dot(wuvt_ref[h], o_lat[:, h * tq:(h + 1) * tq]) for h in range(MLA_HEADS)], axis=0)
        o_ref[...] = ot.T.astype(o_ref.dtype)


def _mla_prompt(qt, k, ct, wuvt, seq):
    m = k.shape[0]
    nb = m // seq
    tq = tk = _row_tile(seq, 512)
    nq, nk = seq // tq, seq // tk
    kblk = lambda b, qi, ki: b * nk + jnp.minimum(ki, ((qi + 1) * tq - 1) // tk)
    cols = MLA_HEADS * tq
    return pl.pallas_call(
        functools.partial(_mla_prompt_kernel, tq=tq, tk=tk),
        grid=(nb, nq, nk),
        in_specs=[pl.BlockSpec((qt.shape[0], tq), lambda b, qi, ki: (0, b * nq + qi)),
                  pl.BlockSpec((tk, k.shape[1]), lambda b, qi, ki: (kblk(b, qi, ki), 0)),
                  pl.BlockSpec((ct.shape[0], tk), lambda b, qi, ki: (0, kblk(b, qi, ki))),
                  _full(wuvt)],
        out_specs=pl.BlockSpec((tq, MLA_HEADS * MLA_V), lambda b, qi, ki: (b * nq + qi, 0)),
        out_shape=jax.ShapeDtypeStruct((m, MLA_HEADS * MLA_V), BF16),
        scratch_shapes=[pltpu.VMEM((MLA_QK, cols), BF16), pltpu.VMEM((1, cols), F32), pltpu.VMEM((1, cols), F32),
                        pltpu.VMEM((MLA_KV_LORA, cols), F32)],
        compiler_params=_cparams("parallel", "parallel", "arbitrary"),
        name="mla_prompt",
    )(qt, k, ct, wuvt)


def _mla_sample_kernel(pt_ref, q_ref, kn_ref, wuv_ref, *rest, npg, valid):
    ckv_pages, kpet_pages = rest[:npg], rest[npg:2 * npg]
    o_ref, qs_ref, m_ref, l_ref, acc_ref = rest[2 * npg:]
    g = pl.program_id(1)
    sp = q_ref.shape[0]

    @pl.when(g == 0)
    def _():
        qs_ref[...] = _stack_heads(q_ref[...], MLA_QK)
        _softmax_init(m_ref, l_ref, acc_ref)

    qs = qs_ref[...]
    ckv = jnp.concatenate([r[...].astype(BF16) for r in ckv_pages], axis=0)
    kpet = jnp.concatenate([r[...].astype(BF16) for r in kpet_pages], axis=1)
    s = _dot_nt(qs[:, :MLA_KV_LORA], ckv) + _dot(qs[:, MLA_KV_LORA:MLA_KV_LORA + MLA_ROPE], kpet)
    _softmax_update(s, ckv, m_ref, l_ref, acc_ref)

    @pl.when(g == pl.num_programs(1) - 1)
    def _():
        kn = kn_ref[...]
        ckvn = kn[:, :MLA_KV_LORA]
        s = _dot_nt(qs, kn)
        rows = MLA_HEADS * sp
        qpos = lax.broadcasted_iota(jnp.int32, (rows, sp), 0) % sp
        kpos = lax.broadcasted_iota(jnp.int32, (rows, sp), 1)
        s = jnp.where(jnp.logical_and(kpos <= qpos, kpos < valid), s, NEG_INF)
        _softmax_update(s, ckvn, m_ref, l_ref, acc_ref)
        o_ref[...] = _mla_out(acc_ref, l_ref, wuv_ref, sp).astype(o_ref.dtype)


def _mla_sample(q, kn, wuv, cache_ckv, cache_kpet, layer, page_table, valid):
    m = q.shape[0]
    db, n_pages = page_table.shape
    sp = m // db
    npg = math.gcd(n_pages, MLA_PAGES_PER_STEP)
    rows = MLA_HEADS * sp
    qmap = lambda b, g, pt: (b, 0)

    def page_spec(shape, j):
        return pl.BlockSpec((None, None) + shape, lambda b, g, pt: (layer, pt[b * n_pages + g * npg + j], 0, 0))

    grid_spec = pltpu.PrefetchScalarGridSpec(
        num_scalar_prefetch=1,
        grid=(db, n_pages // npg),
        in_specs=[pl.BlockSpec((sp, q.shape[1]), qmap), pl.BlockSpec((sp, kn.shape[1]), qmap),
                  pl.BlockSpec(wuv.shape, lambda b, g, pt: (0, 0, 0))]
                 + [page_spec((PAGE_SIZE, MLA_KV_LORA), j) for j in range(npg)]
                 + [page_spec((MLA_ROPE, PAGE_SIZE), j) for j in range(npg)],
        out_specs=pl.BlockSpec((sp, MLA_HEADS * MLA_V), qmap),
        scratch_shapes=[pltpu.VMEM((rows, MLA_QK), BF16), pltpu.VMEM((rows, 1), F32), pltpu.VMEM((rows, 1), F32),
                        pltpu.VMEM((rows, MLA_KV_LORA), F32)],
    )
    return pl.pallas_call(
        functools.partial(_mla_sample_kernel, npg=npg, valid=valid),
        grid_spec=grid_spec,
        out_shape=jax.ShapeDtypeStruct((m, MLA_HEADS * MLA_V), BF16),
        compiler_params=_cparams("parallel", "arbitrary"),
        name="mla_sample",
    )(page_table.reshape(-1), q, kn, wuv, *([cache_ckv] * npg), *([cache_kpet] * npg))


def _moba_prep_kernel(q_ref, k_ref, v_ref, cos_ref, sin_ref, qr_ref, kr_ref, *rest, nblk):
    cos, sin = cos_ref[...], sin_ref[...]
    qr_ref[...] = _rope(q_ref[...], cos, sin, MOBA_HD)
    kr = _rope(k_ref[...], cos, sin, MOBA_HD)
    kr_ref[...] = kr
    if nblk:
        kb_ref, vt_ref, km_ref = rest
        kb_ref[...] = kr.astype(BF16)
        vt_ref[...] = v_ref[...].T.astype(BF16)
        for j in range(nblk):
            km_ref[j] = jnp.mean(kr[j * MOBA_BLOCK:(j + 1) * MOBA_BLOCK], axis=0, keepdims=True)


def _moba_prep(q, k, v, cos, sin, prompt):
    m, w = q.shape
    p = cos.shape[0]
    tm = _row_tile(math.gcd(m, p), 512)
    npos = p // tm
    if prompt:
        assert tm % MOBA_BLOCK == 0
    nblk = tm // MOBA_BLOCK if prompt else 0
    row = pl.BlockSpec((tm, w), lambda i: (i, 0))
    pos = pl.BlockSpec((tm, w), lambda i: (i % npos, 0))
    out_specs = [row, row]
    out_shape = [jax.ShapeDtypeStruct((m, w), F32), jax.ShapeDtypeStruct((m, w), F32)]
    if prompt:
        out_specs += [row, pl.BlockSpec((w, tm), lambda i: (0, i)), pl.BlockSpec((nblk, 1, w), lambda i: (i, 0, 0))]
        out_shape += [jax.ShapeDtypeStruct((m, w), BF16), jax.ShapeDtypeStruct((w, m), BF16),
                      jax.ShapeDtypeStruct((m // MOBA_BLOCK, 1, w), F32)]
    return pl.pallas_call(
        functools.partial(_moba_prep_kernel, nblk=nblk),
        grid=(m // tm,),
        in_specs=[row, row, row, pos, pos],
        out_specs=out_specs,
        out_shape=out_shape,
        compiler_params=_cparams("parallel"),
        name="moba_prep",
    )(q, k, v, cos, sin)


def _top_rows(gate, idx, count, limit, axis):
    sel = jnp.zeros(gate.shape, F32)
    for _ in range(count):
        mx = jnp.max(gate, axis=axis, keepdims=True)
        first = jnp.min(jnp.where(gate == mx, idx, limit), axis=axis, keepdims=True)
        pick = idx == first
        sel = jnp.where(pick, 1.0, sel)
        gate = jnp.where(pick, BELOW_NEG_INF, gate)
    return sel


def _moba_prompt_kernel(q_ref, k_ref, vt_ref, km_ref, o_ref, sel_ref, *, nb, topn):
    qi = pl.program_id(2)
    blk = MOBA_BLOCK
    q = q_ref[...]
    qt = (q * (MOBA_HD ** -0.5 * LOG2E)).T.astype(BF16)
    b_idx = lax.broadcasted_iota(jnp.int32, (nb, blk), 0)
    past = b_idx < qi
    gate = jnp.where(past, _dot_nt_f32(km_ref[...], q), NEG_INF)
    sel_ref[...] = jnp.where(past, _top_rows(gate, b_idx, topn, nb, 0), 0.0)

    r0 = pl.multiple_of(qi * blk, blk)
    st = _dot(k_ref[pl.ds(r0, blk), :], qt)
    causal = lax.broadcasted_iota(jnp.int32, (blk, blk), 0) <= lax.broadcasted_iota(jnp.int32, (blk, blk), 1)
    st = jnp.where(causal, st, NEG_INF)
    m0 = jnp.max(st, axis=0, keepdims=True)
    p0 = jnp.exp2(st - m0)
    l0 = jnp.sum(p0, axis=0, keepdims=True)
    acc0 = _dot(vt_ref[:, pl.ds(r0, blk)], p0.astype(BF16))

    def body(n, carry):
        m, l, acc = carry
        rn = pl.multiple_of(n * blk, blk)
        allowed = sel_ref[pl.ds(n, 1), :] > 0.0
        sn = jnp.where(allowed, _dot(k_ref[pl.ds(rn, blk), :], qt), NEG_INF)
        m_new = jnp.maximum(m, jnp.max(sn, axis=0, keepdims=True))
        alpha = jnp.exp2(m - m_new)
        p = jnp.exp2(sn - m_new)
        l = alpha * l + jnp.sum(p, axis=0, keepdims=True)
        acc = alpha * acc + _dot(vt_ref[:, pl.ds(rn, blk)], p.astype(BF16))
        return m_new, l, acc

    _, l, acc = lax.fori_loop(0, qi, body, (m0, l0, acc0))
    o_ref[...] = (acc / l).T.astype(o_ref.dtype)


def _moba_prompt(qr, kb, vt, km, seq):
    m = qr.shape[0]
    nbatch = m // seq
    assert seq % MOBA_BLOCK == 0
    nb = seq // MOBA_BLOCK
    topn = min(MOBA_TOPK, nb)
    hd = MOBA_HD
    qmap = lambda b, h, qi: (b * nb + qi, h)
    return pl.pallas_call(
        functools.partial(_moba_prompt_kernel, nb=nb, topn=topn),
        grid=(nbatch, MOBA_HEADS, nb),
        in_specs=[pl.BlockSpec((MOBA_BLOCK, hd), qmap),
                  pl.BlockSpec((seq, hd), lambda b, h, qi: (b, h)),
                  pl.BlockSpec((hd, seq), lambda b, h, qi: (h, b)),
                  pl.BlockSpec((None, nb, hd), lambda b, h, qi: (b, 0, h))],
        out_specs=pl.BlockSpec((MOBA_BLOCK, hd), qmap),
        out_shape=jax.ShapeDtypeStruct((m, MOBA_HEADS * hd), BF16),
        scratch_shapes=[pltpu.VMEM((nb, MOBA_BLOCK), F32)],
        compiler_params=_cparams("parallel", "parallel", "arbitrary"),
        name="moba_prompt",
    )(qr, kb, vt, km)


def _moba_sample_kernel(pt_ref, q_ref, kn_ref, vn_ref, *rest, npg, ng, nb, topn, valid):
    k_pages, v_pages = rest[:npg], rest[npg:2 * npg]
    o_ref, qrow_ref, s_ref, p_ref, km_ref, acc_ref, pown_ref, l_ref = rest[2 * npg:]
    g = pl.program_id(1)
    sp = q_ref.shape[0]
    hw = MOBA_HEADS * MOBA_HD
    ppb = MOBA_BLOCK // PAGE_SIZE
    nbs = npg // ppb
    scale = MOBA_HD ** -0.5

    def query_rows():
        qt = jnp.concatenate([q_ref[...]] * (LANES // sp), axis=0)
        rh = lax.broadcasted_iota(jnp.int32, (LANES, hw), 0) // sp
        lh = lax.broadcasted_iota(jnp.int32, (LANES, hw), 1) // MOBA_HD
        return jnp.where(rh == lh, qt, 0.0)

    def page_heads(ref):
        return [ref[pl.ds(h, PAGE_SIZE, stride=MOBA_HEADS), :] for h in range(MOBA_HEADS)]

    @pl.when(g == 0)
    def _():
        qrow_ref[...] = (query_rows() * scale).astype(BF16)
        km_ref[...] = jnp.zeros_like(km_ref)
        acc_ref[...] = jnp.zeros_like(acc_ref)

    @pl.when(g < ng)
    def _():
        qrow = qrow_ref[...]
        base = pl.multiple_of(g * (npg * PAGE_SIZE), npg * PAGE_SIZE)
        sums = []
        for j in range(npg):
            xh = page_heads(k_pages[j])
            kall = jnp.concatenate([t.astype(BF16) for t in xh], axis=1)
            s_ref[pl.ds(base + j * PAGE_SIZE, PAGE_SIZE), :] = _dot_nt(kall, qrow)
            sums.append(jnp.concatenate([jnp.sum(t, axis=0, keepdims=True) for t in xh], axis=1))
        blk_row = lax.broadcasted_iota(jnp.int32, (nb, 1), 0)
        km = km_ref[...]
        for jb in range(nbs):
            mean = sum(sums[jb * ppb:(jb + 1) * ppb]) * (1.0 / MOBA_BLOCK)
            km = km + jnp.where(blk_row == g * nbs + jb, mean, 0.0)
        km_ref[...] = km

    @pl.when(g == ng - 1)
    def _():
        b_idx = lax.broadcasted_iota(jnp.int32, (nb, LANES), 0)
        sel = _top_rows(_dot_nt_f32(km_ref[...], query_rows()), b_idx, topn, nb, 0)
        s_own = _dot_nt(kn_ref[...].astype(BF16), qrow_ref[...])
        kt = lax.broadcasted_iota(jnp.int32, (sp, LANES), 0)
        qt = lax.broadcasted_iota(jnp.int32, (sp, LANES), 1) % sp
        s_own = jnp.where(jnp.logical_and(kt <= qt, kt < valid), s_own, NEG_INF)

        def masked(n):
            return jnp.where(sel[n:n + 1, :] > 0.0, s_ref[n * MOBA_BLOCK:(n + 1) * MOBA_BLOCK, :], NEG_INF)

        mx = jnp.max(s_own, axis=0, keepdims=True)
        for n in range(nb):
            mx = jnp.maximum(mx, jnp.max(masked(n), axis=0, keepdims=True))
        p_own = jnp.exp(s_own - mx)
        l = jnp.sum(p_own, axis=0, keepdims=True)
        for n in range(nb):
            pn = jnp.exp(masked(n) - mx)
            l = l + jnp.sum(pn, axis=0, keepdims=True)
            p_ref[n * MOBA_BLOCK:(n + 1) * MOBA_BLOCK, :] = pn.astype(BF16)
        pown_ref[...] = jnp.concatenate([p_own, jnp.zeros((LANES - sp, LANES), F32)], axis=0).astype(BF16)
        l_ref[...] = jnp.broadcast_to(l, l_ref.shape)

    @pl.when(g >= ng)
    def _():
        base = pl.multiple_of((g - ng) * (npg * PAGE_SIZE), npg * PAGE_SIZE)
        acc = acc_ref[...]
        for j in range(npg):
            vall = jnp.concatenate([t.astype(BF16) for t in page_heads(v_pages[j])], axis=1)
            acc = acc + _dot_tn(p_ref[pl.ds(base + j * PAGE_SIZE, PAGE_SIZE), :], vall)
        acc_ref[...] = acc

    @pl.when(g == 2 * ng - 1)
    def _():
        vn = jnp.concatenate([vn_ref[...], jnp.zeros((LANES - sp, hw), F32)], axis=0).astype(BF16)
        acc = acc_ref[...] + _dot_tn(pown_ref[...], vn)
        o = acc / l_ref[...].T[:, :1]
        o_ref[...] = jnp.concatenate(
            [o[h * sp:(h + 1) * sp, h * MOBA_HD:(h + 1) * MOBA_HD] for h in range(MOBA_HEADS)], axis=1
        ).astype(o_ref.dtype)


def _moba_sample(qr, kr, v, cache_k, cache_v, layer, page_table, valid):
    m, hw = qr.shape
    cache_k = cache_k.reshape(cache_k.shape[:2] + (PAGE_SIZE * MOBA_HEADS, MOBA_HD))
    cache_v = cache_v.reshape(cache_v.shape[:2] + (PAGE_SIZE * MOBA_HEADS, MOBA_HD))
    db, n_pages = page_table.shape
    sp = m // db
    past = n_pages * PAGE_SIZE
    assert past % MOBA_BLOCK == 0
    nb = past // MOBA_BLOCK
    topn = min(MOBA_TOPK, nb)
    ppb = MOBA_BLOCK // PAGE_SIZE
    npg = math.gcd(n_pages, MOBA_PAGES_PER_STEP)
    assert npg % ppb == 0
    ng = n_pages // npg
    qmap = lambda b, g, pt: (b, 0)
    blk = (None, None, PAGE_SIZE * MOBA_HEADS, MOBA_HD)

    def k_spec(j):
        return pl.BlockSpec(blk, lambda b, g, pt: (layer, pt[b * n_pages + jnp.minimum(g, ng - 1) * npg + j], 0, 0))

    def v_spec(j):
        return pl.BlockSpec(blk, lambda b, g, pt: (layer, pt[b * n_pages + jnp.maximum(g - ng, 0) * npg + j], 0, 0))

    grid_spec = pltpu.PrefetchScalarGridSpec(
        num_scalar_prefetch=1,
        grid=(db, 2 * ng),
        in_specs=[pl.BlockSpec((sp, hw), qmap)] * 3 + [k_spec(j) for j in range(npg)] + [v_spec(j) for j in range(npg)],
        out_specs=pl.BlockSpec((sp, hw), qmap),
        scratch_shapes=[pltpu.VMEM((LANES, hw), BF16), pltpu.VMEM((past, LANES), F32), pltpu.VMEM((past, LANES), BF16),
                        pltpu.VMEM((nb, hw), F32), pltpu.VMEM((LANES, hw), F32), pltpu.VMEM((LANES, LANES), BF16),
                        pltpu.VMEM((SUBLANES, LANES), F32)],
    )
    return pl.pallas_call(
        functools.partial(_moba_sample_kernel, npg=npg, ng=ng, nb=nb, topn=topn, valid=valid),
        grid_spec=grid_spec,
        out_shape=jax.ShapeDtypeStruct((m, hw), BF16),
        compiler_params=_cparams("parallel", "arbitrary"),
        name="moba_sample",
    )(page_table.reshape(-1), qr, kr, v, *([cache_k] * npg), *([cache_v] * npg))


def _s5_disc_kernel(lr_ref, li_ref, dt_ref, bre_ref, bim_ref, pwr_ref, pwi_ref, bbr_ref, bbi_ref):
    lr, li, dt = lr_ref[...], li_ref[...], dt_ref[...]
    k = (lax.broadcasted_iota(jnp.int32, pwr_ref.shape, 0) + 1).astype(F32)
    mag = jnp.exp(k * (lr * dt))
    ang = k * (li * dt)
    pwr = mag * jnp.cos(ang)
    pwi = mag * jnp.sin(ang)
    pwr_ref[...] = pwr
    pwi_ref[...] = pwi
    nr, ni = pwr[0:1] - 1.0, pwi[0:1]
    den = lr * lr + li * li
    f_re = (nr * lr + ni * li) / den
    f_im = (ni * lr - nr * li) / den
    bre, bim = bre_ref[...], bim_ref[...]
    bbr_ref[...] = f_re * bre - f_im * bim
    bbi_ref[...] = f_re * bim + f_im * bre


def _s5_discretize(lam_re, lam_im, log_dt, b_re, b_im):
    n = S5_N
    flat = lambda a: a.astype(F32).reshape(1, n)
    dt = jnp.exp(jnp.repeat(log_dt.astype(F32), S5_STATE)).reshape(1, n)
    bt = lambda b: jnp.transpose(b.astype(F32), (2, 0, 1)).reshape(S5_GROUP, n)
    args = (flat(lam_re), flat(lam_im), dt, bt(b_re), bt(b_im))
    return pl.pallas_call(
        _s5_disc_kernel,
        in_specs=[_full(a) for a in args],
        out_shape=[jax.ShapeDtypeStruct((SUBLANES, n), F32)] * 2 + [jax.ShapeDtypeStruct((S5_GROUP, n), F32)] * 2,
        name="s5_discretize",
    )(*args)


def _s5_block_diag(pieces):
    r, g, c = pieces.shape
    eye = jnp.eye(g, dtype=pieces.dtype)
    return jnp.einsum('rgc,gh->grhc', pieces, eye).reshape(g * r, g * c)


def _s5_kernel(u_ref, bb_ref, pwr_ref, pwi_ref, cc_ref, d_ref, wg_ref, bg_ref, *rest, chain):
    n = S5_N
    u = u_ref[...]
    tt = u.shape[0]
    ngrp = tt // SUBLANES
    bu = _dot(u.astype(BF16), bb_ref[...])
    xr = bu[:, :n].reshape(ngrp, SUBLANES, n)
    xi = bu[:, n:].reshape(ngrp, SUBLANES, n)
    row = lax.broadcasted_iota(jnp.int32, (ngrp, SUBLANES, n), 1)
    for d in (1, 2, 4):
        pr, pi = pwr_ref[d - 1:d, :], pwi_ref[d - 1:d, :]
        sr, si = pltpu.roll(xr, d, 1), pltpu.roll(xi, d, 1)
        keep = row >= d
        xr, xi = (xr + jnp.where(keep, pr * sr - pi * si, 0.0),
                  xi + jnp.where(keep, pr * si + pi * sr, 0.0))
    pw_r, pw_i = pwr_ref[...], pwi_ref[...]

    if chain:
        o_ref, sr_ref, si_ref, xr_s, xi_s, cr_s, ci_s = rest
        t = pl.program_id(1)

        @pl.when(t == 0)
        def _():
            cr_s[...] = jnp.zeros_like(cr_s)
            ci_s[...] = jnp.zeros_like(ci_s)

        xr_s[...] = xr
        xi_s[...] = xi

        def body(i, carry):
            cr, ci = carry
            a = xr_s[i] + pw_r * cr - pw_i * ci
            b = xi_s[i] + pw_r * ci + pw_i * cr
            xr_s[i] = a
            xi_s[i] = b
            return a[SUBLANES - 1:SUBLANES], b[SUBLANES - 1:SUBLANES]

        cr, ci = lax.fori_loop(0, ngrp, body, (cr_s[...], ci_s[...]))
        cr_s[...] = cr
        ci_s[...] = ci
        xr, xi = xr_s[...], xi_s[...]

        @pl.when(t == pl.num_programs(1) - 1)
        def _():
            sr_ref[...] = cr
            si_ref[...] = ci
    else:
        s0r_ref, s0i_ref, o_ref, xr_ref, xi_ref = rest
        s0r = s0r_ref[...].reshape(ngrp, SUBLANES, n)
        s0i = s0i_ref[...].reshape(ngrp, SUBLANES, n)
        xr, xi = xr + pw_r * s0r - pw_i * s0i, xi + pw_r * s0i + pw_i * s0r
        xr_ref[...] = xr.reshape(tt, n)
        xi_ref[...] = xi.reshape(tt, n)

    xcat = jnp.concatenate([xr.reshape(tt, n), xi.reshape(tt, n)], axis=1).astype(BF16)
    y = _dot(xcat, cc_ref[...]) + d_ref[...] * u
    y = y * (0.5 * (1.0 + jnp.tanh(math.sqrt(2.0 / math.pi) * (y + 0.044715 * (y * y * y)))))
    z = _dot(y.astype(BF16), wg_ref[...]) + bg_ref[...]
    o_ref[...] = (y * _sigmoid(z)).astype(o_ref.dtype)


def _s5_prompt(u, bb, pwr, pwi, cc, d, wg, bg, seq):
    m, w = u.shape
    nbatch = m // seq
    tt = _row_tile(seq, 256)
    nt = seq // tt
    n = S5_N
    row = pl.BlockSpec((tt, w), lambda b, t: (b * nt + t, 0))
    st = pl.BlockSpec((None, 1, n), lambda b, t: (b, 0, 0))
    return pl.pallas_call(
        functools.partial(_s5_kernel, chain=True),
        grid=(nbatch, nt),
        in_specs=[row, _full(bb), _full(pwr), _full(pwi), _full(cc), _full(d), _full(wg), _full(bg)],
        out_specs=[row, st, st],
        out_shape=[jax.ShapeDtypeStruct((m, w), BF16), jax.ShapeDtypeStruct((nbatch, 1, n), F32),
                   jax.ShapeDtypeStruct((nbatch, 1, n), F32)],
        scratch_shapes=[pltpu.VMEM((tt // SUBLANES, SUBLANES, n), F32), pltpu.VMEM((tt // SUBLANES, SUBLANES, n), F32),
                        pltpu.VMEM((1, n), F32), pltpu.VMEM((1, n), F32)],
        compiler_params=_cparams("parallel", "arbitrary"),
        name="s5_prompt",
    )(u, bb, pwr, pwi, cc, d, wg, bg)


def _s5_sample(u, bb, pwr, pwi, cc, d, wg, bg, s0r, s0i):
    m, w = u.shape
    n = S5_N
    tt = _row_tile(m, 256)
    row = pl.BlockSpec((tt, w), lambda i: (i, 0))
    st = pl.BlockSpec((tt, n), lambda i: (i, 0))
    return pl.pallas_call(
        functools.partial(_s5_kernel, chain=False),
        grid=(m // tt,),
        in_specs=[row, _full(bb), _full(pwr), _full(pwi), _full(cc), _full(d), _full(wg), _full(bg), st, st],
        out_specs=[row, st, st],
        out_shape=[jax.ShapeDtypeStruct((m, w), BF16), jax.ShapeDtypeStruct((m, n), F32),
                   jax.ShapeDtypeStruct((m, n), F32)],
        compiler_params=_cparams("parallel"),
        name="s5_sample",
    )(u, bb, pwr, pwi, cc, d, wg, bg, s0r, s0i)


def kernel(x_prompt, x_sample, cache_mem_k, cache_mem_v, state_gla, cache_mla_ckv, cache_mla_kpe, cache_moba_k, cache_moba_v, state_s5, page_table, mem_prompt, g_ffn1, w_ffn1_gate, w_ffn1_up, w_ffn1_down, g_mix, w_in_a, w_gla_gate, b_gla_gate, g_gla_norm, g_mla_q, w_mla_qb, g_mla_kv, w_mla_kvb, w_out_a, w_in_b, s5_lam_re, s5_lam_im, s5_log_dt, s5_b_re, s5_b_im, s5_c_re, s5_c_im, s5_d, w_s5_glu, b_s5_glu, w_out_b, g_xattn, w_xq, w_xk, w_xv, w_xo, g_ffn2, w_ffn2_gate, w_ffn2_up, w_ffn2_down, g_final):
    B, T, D = x_prompt.shape
    DB, S, _ = x_sample.shape
    depth = g_ffn1.shape[0]
    n_pages = page_table.shape[1]
    past = n_pages * PAGE_SIZE
    SP = SAMPLE_PAD
    assert S <= SP
    n_mem = mem_prompt.shape[1]
    bf = lambda a: a.astype(BF16)

    xp = x_prompt.reshape(B * T, D)
    xs = jnp.pad(x_sample, ((0, 0), (0, SP - S), (0, 0))).reshape(DB * SP, D)
    pos_p = jnp.arange(T, dtype=jnp.int32)
    pos_s = jnp.tile(past + jnp.arange(SP, dtype=jnp.int32), DB)
    mla_tab_p = _rope_tables(pos_p, MLA_ROPE, LANES, MLA_HEADS)
    mla_tab_s = _rope_tables(pos_s, MLA_ROPE, LANES, MLA_HEADS)
    moba_tab_p = _rope_tables(pos_p, MOBA_HD, MOBA_HD, MOBA_HEADS)
    moba_tab_s = _rope_tables(pos_s, MOBA_HD, MOBA_HD, MOBA_HEADS)
    mem2d = mem_prompt.reshape(B * n_mem, D)
    cache_kpet = jnp.swapaxes(cache_mla_kpe, 2, 3)
    ones_d = jnp.ones((D,), F32)
    gla_chunk = math.gcd(T, GLA_CHUNK)

    def unpad(a):
        return a.reshape((DB, SP) + a.shape[1:])[:, :S]

    mem_k_p, mem_v_p = [], []
    gla_p, gla_s = [], []
    ckv_p, kpe_p, ckv_s, kpe_s = [], [], [], []
    mk_p, mv_p, mk_s, mv_s = [], [], [], []
    s5_p, s5_s = [], []

    for l in range(depth):
        i = l // 2
        wg1, wu1, wd1 = bf(w_ffn1_gate[l]), bf(w_ffn1_up[l]), bf(w_ffn1_down[l])
        xp = _ffn(xp, g_ffn1[l], wg1, wu1, wd1)
        xs = _ffn(xs, g_ffn1[l], wg1, wu1, wd1)

        if l % 2 == 0:
            qa_w, ka_w, va_w, ga_w, ra_w, ql_w, kvl_w, kr_w = jnp.split(
                w_in_a[i], [256, 512, 1024, 1040, 1552, 1936, 2192], axis=1)
            padc = lambda w: jnp.pad(w, ((0, 0), (0, LANES - w.shape[1])))
            w_in = bf(jnp.concatenate([va_w, ra_w, qa_w, ka_w, kvl_w, ql_w, padc(kr_w), padc(ga_w)], axis=1))
            widths = (512, 512, 256, 256, 256, 384, LANES, LANES)
            wgate = bf(jnp.pad(w_gla_gate[i], ((0, LANES - GLA_GATE_RANK), (0, 0))))
            bgate = b_gla_gate[i].reshape(1, -1)
            gnorm = g_gla_norm[i].reshape(1, -1)
            wqb = w_mla_qb[i].reshape(MLA_Q_LORA, MLA_HEADS, MLA_NOPE + MLA_ROPE)
            wqn = bf(wqb[:, :, :MLA_NOPE].reshape(MLA_Q_LORA, MLA_HEADS * MLA_NOPE))
            wqp = bf(jnp.pad(wqb[:, :, MLA_NOPE:], ((0, 0), (0, 0), (0, LANES - MLA_ROPE))).reshape(MLA_Q_LORA, MLA_HEADS * LANES))
            wkvb = w_mla_kvb[i].reshape(MLA_KV_LORA, MLA_HEADS, MLA_NOPE + MLA_V)
            wuk = bf(jnp.transpose(wkvb[:, :, :MLA_NOPE], (1, 2, 0)))
            wuv = bf(jnp.transpose(wkvb[:, :, MLA_NOPE:], (1, 0, 2)))
            wuvt = bf(jnp.transpose(wkvb[:, :, MLA_NOPE:], (1, 2, 0)))
            gq = g_mla_q[i].reshape(1, -1)
            gkv = g_mla_kv[i].reshape(1, -1)
            wo = bf(w_out_a[i])
            wo_a, wo_b = wo[:GLA_HEADS * GLA_DV], wo[GLA_HEADS * GLA_DV:]

            def even(x, tabs, s0, prompt):
                va, ra, qa, ka, kvl, ql, ksl, gsl = _proj(x, g_mix[l], w_in, widths)
                if prompt:
                    o_a, st = _gla(qa, ka, va, gsl, ra, s0, wgate, bgate, gnorm,
                                   seq=T, rows=gla_chunk, chunk=gla_chunk, valid=gla_chunk)
                else:
                    o_a, st = _gla(qa, ka, va, gsl, ra, s0, wgate, bgate, gnorm,
                                   seq=SP, rows=SP, chunk=2 * SP, valid=S)
                if prompt:
                    q_t, k_mla, ckv, kpe, c_t = _mla_prep(ql, kvl, ksl, tabs[0], tabs[1], gq, gkv, wqn, wqp, wuk, True)
                    o_b = _mla_prompt(q_t, k_mla, c_t, wuvt, T)
                else:
                    q_mla, k_mla, ckv, kpe = _mla_prep(ql, kvl, ksl, tabs[0], tabs[1], gq, gkv, wqn, wqp, wuk, False)
                    o_b = _mla_sample(q_mla, k_mla, wuv, cache_mla_ckv, cache_kpet, i, page_table, S)
                return _outproj(o_a, o_b, wo_a, wo_b, x), st, ckv, kpe

            xp, st, ckv, kpe = even(xp, mla_tab_p, jnp.zeros((B, GLA_HEADS, GLA_DK, GLA_DV), F32), True)
            gla_p.append(st)
            ckv_p.append(ckv.reshape(B, T, MLA_KV_LORA))
            kpe_p.append(kpe.reshape(B, T, MLA_ROPE))
            xs, st, ckv, kpe = even(xs, mla_tab_s, state_gla[i], False)
            gla_s.append(st)
            ckv_s.append(unpad(ckv))
            kpe_s.append(unpad(kpe))
        else:
            hw = MOBA_HEADS * MOBA_HD
            w_in = bf(w_in_b[i])
            widths = (hw, hw, hw, S5_WIDTH)
            pwr, pwi, bbr, bbi = _s5_discretize(s5_lam_re[i], s5_lam_im[i], s5_log_dt[i], s5_b_re[i], s5_b_im[i])
            to_blocks = lambda a: a.reshape(a.shape[0], S5_GROUPS, S5_STATE)
            bb = bf(jnp.concatenate([_s5_block_diag(to_blocks(bbr)), _s5_block_diag(to_blocks(bbi))], axis=1))
            c_t = lambda c: jnp.transpose(c.astype(F32), (2, 0, 1))
            cc = bf(jnp.concatenate([_s5_block_diag(c_t(s5_c_re[i])), -_s5_block_diag(c_t(s5_c_im[i]))], axis=0))
            dsk = s5_d[i].reshape(1, -1)
            wglu = bf(w_s5_glu[i])
            bglu = b_s5_glu[i].reshape(1, -1)
            wo = bf(w_out_b[i])
            wo_c, wo_d = wo[:hw], wo[hw:]

            q, k, v, u = _proj(xp, g_mix[l], w_in, widths)
            qr, kr, kb, vt, km = _moba_prep(q, k, v, moba_tab_p[0], moba_tab_p[1], True)
            o_c = _moba_prompt(qr, kb, vt, km.reshape(B, T // MOBA_BLOCK, hw), T)
            o_d, sr, si = _s5_prompt(u, bb, pwr, pwi, cc, dsk, wglu, bglu, T)
            xp = _outproj(o_c, o_d, wo_c, wo_d, xp)
            mk_p.append(kr.reshape(B, T, MOBA_HEADS, MOBA_HD))
            mv_p.append(v.reshape(B, T, MOBA_HEADS, MOBA_HD))
            s5_p.append(jnp.stack([sr.reshape(B, S5_GROUPS, S5_STATE), si.reshape(B, S5_GROUPS, S5_STATE)], axis=-1))

            q, k, v, u = _proj(xs, g_mix[l], w_in, widths)
            qr, kr = _moba_prep(q, k, v, moba_tab_s[0], moba_tab_s[1], False)
            o_c = _moba_sample(qr, kr, v, cache_moba_k, cache_moba_v, i, page_table, S)
            s0 = state_s5[i].astype(F32).reshape(DB, S5_N, 2)
            s0r = jnp.repeat(s0[..., 0], SP, axis=0)
            s0i = jnp.repeat(s0[..., 1], SP, axis=0)
            o_d, xr, xi = _s5_sample(u, bb, pwr, pwi, cc, dsk, wglu, bglu, s0r, s0i)
            xs = _outproj(o_c, o_d, wo_c, wo_d, xs)
            mk_s.append(unpad(kr).reshape(DB, S, MOBA_HEADS, MOBA_HD))
            mv_s.append(unpad(v).reshape(DB, S, MOBA_HEADS, MOBA_HD))
            last = lambda a: a.reshape(DB, SP, S5_GROUPS, S5_STATE)[:, S - 1]
            s5_s.append(jnp.stack([last(xr), last(xi)], axis=-1))

        mkp, mvp = _proj(mem2d, ones_d, bf(jnp.concatenate([w_xk[l], w_xv[l]], axis=1)), (D, D), norm=False)
        mem_k_p.append(mkp.reshape(B, n_mem, X_HEADS, D // X_HEADS))
        mem_v_p.append(mvp.reshape(B, n_mem, X_HEADS, D // X_HEADS))
        wq, wo = bf(w_xq[l]), bf(w_xo[l])
        xp = _xattn_prompt(xp, g_xattn[l], wq, wo, mkp.reshape(B, n_mem, D), mvp.reshape(B, n_mem, D), T)
        xs = _xattn_sample(xs, g_xattn[l], wq, wo, cache_mem_k, cache_mem_v, l)

        wg2, wu2, wd2 = bf(w_ffn2_gate[l]), bf(w_ffn2_up[l]), bf(w_ffn2_down[l])
        xp = _ffn(xp, g_ffn2[l], wg2, wu2, wd2)
        xs = _ffn(xs, g_ffn2[l], wg2, wu2, wd2)

    y_prompt = _final_norm(xp, g_final).reshape(B, T, D)
    y_sample = unpad(_final_norm(xs, g_final))
    return (y_prompt, y_sample,
            jnp.stack(mem_k_p), jnp.stack(mem_v_p),
            jnp.stack(gla_p), jnp.stack(gla_s),
            jnp.stack(ckv_p), jnp.stack(kpe_p), jnp.stack(ckv_s), jnp.stack(kpe_s),
            jnp.stack(mk_p), jnp.stack(mv_p), jnp.stack(mk_s), jnp.stack(mv_s),
            jnp.stack(s5_p), jnp.stack(s5_s))
```

```python
import functools
import math

import jax
import jax.numpy as jnp
from jax import lax
from jax.experimental import pallas as pl
from jax.experimental.pallas import tpu as pltpu

F32 = jnp.float32
BF16 = jnp.bfloat16

RMS_EPS = 1e-6
ROPE_THETA = 10000.0
NEG_INF = -1e30
BELOW_NEG_INF = -3.0e38
PAGE_SIZE = 128
GLA_HEADS, GLA_DK, GLA_DV, GLA_GATE_RANK, GLA_TAU, GLA_CHUNK = 4, 64, 128, 16, 16.0, 64
MLA_HEADS, MLA_Q_LORA, MLA_KV_LORA, MLA_NOPE, MLA_ROPE, MLA_V = 4, 384, 256, 128, 64, 128
MOBA_HEADS, MOBA_HD, MOBA_BLOCK, MOBA_TOPK = 4, 128, 256, 3
S5_WIDTH, S5_GROUP, S5_STATE = 512, 16, 64
S5_GROUPS = S5_WIDTH // S5_GROUP
S5_N = S5_GROUPS * S5_STATE
X_HEADS = 4
LANES = 128
SUBLANES = 8
SAMPLE_PAD = SUBLANES
VMEM_LIMIT = 56 * 1024 * 1024
MLA_PAGES_PER_STEP = 32
MOBA_PAGES_PER_STEP = 16
LOG2E = math.log2(math.e)


def _cparams(*sem):
    return pltpu.CompilerParams(dimension_semantics=sem, vmem_limit_bytes=VMEM_LIMIT)


def _row_tile(m, target):
    t = min(m, target)
    while m % t or t % SUBLANES:
        t -= 1
    return t


def _rms(x, g):
    return x * lax.rsqrt(jnp.mean(x * x, axis=-1, keepdims=True) + RMS_EPS) * g


def _sigmoid(x):
    return 1.0 / (1.0 + jnp.exp(-x))


def _dot(a, b):
    return jnp.dot(a, b, preferred_element_type=F32)


def _dot_nt(a, b):
    return lax.dot_general(a, b, (((1,), (1,)), ((), ())), preferred_element_type=F32)


def _dot_nt_f32(a, b):
    return lax.dot_general(a, b, (((1,), (1,)), ((), ())), precision=lax.Precision.HIGHEST,
                           preferred_element_type=F32)


def _dot_tn(a, b):
    return lax.dot_general(a, b, (((0,), (0,)), ((), ())), preferred_element_type=F32)


def _full(a):
    return pl.BlockSpec(a.shape, lambda *_: (0,) * a.ndim)


def _ffn_kernel(x_ref, g_ref, wg_ref, wu_ref, wd_ref, o_ref, h_ref):
    j = pl.program_id(1)

    @pl.when(j == 0)
    def _():
        h_ref[...] = _rms(x_ref[...], g_ref[...]).astype(BF16)
        o_ref[...] = jnp.zeros_like(o_ref)

    h = h_ref[...]
    a = _dot(h, wg_ref[...])
    z = a * _sigmoid(a) * _dot(h, wu_ref[...])
    o_ref[...] += _dot(z.astype(BF16), wd_ref[...])

    @pl.when(j == pl.num_programs(1) - 1)
    def _():
        o_ref[...] = x_ref[...] + 0.5 * o_ref[...]


def _ffn(x, g, wg, wu, wd):
    m, d = x.shape
    f = wg.shape[1]
    tm = _row_tile(m, 1024)
    tf = 256 if f % 256 == 0 else f
    return pl.pallas_call(
        _ffn_kernel,
        grid=(m // tm, f // tf),
        in_specs=[
            pl.BlockSpec((tm, d), lambda i, j: (i, 0)),
            pl.BlockSpec((1, d), lambda i, j: (0, 0)),
            pl.BlockSpec((d, tf), lambda i, j: (0, j)),
            pl.BlockSpec((d, tf), lambda i, j: (0, j)),
            pl.BlockSpec((tf, d), lambda i, j: (j, 0)),
        ],
        out_specs=pl.BlockSpec((tm, d), lambda i, j: (i, 0)),
        out_shape=jax.ShapeDtypeStruct((m, d), F32),
        scratch_shapes=[pltpu.VMEM((tm, d), BF16)],
        compiler_params=_cparams("parallel", "arbitrary"),
        name="ffn",
    )(x, g.reshape(1, d), wg, wu, wd)


def _proj_kernel(x_ref, g_ref, w_ref, *o_refs, widths, norm):
    x = x_ref[...]
    h = _rms(x, g_ref[...]) if norm else x
    y = _dot(h.astype(BF16), w_ref[...])
    off = 0
    for o_ref, wd in zip(o_refs, widths):
        o_ref[...] = y[:, off:off + wd].astype(o_ref.dtype)
        off += wd


def _proj(x, g, w, widths, norm=True):
    m, d = x.shape
    n = w.shape[1]
    assert sum(widths) == n
    tm = _row_tile(m, 512)
    return pl.pallas_call(
        functools.partial(_proj_kernel, widths=widths, norm=norm),
        grid=(m // tm,),
        in_specs=[pl.BlockSpec((tm, d), lambda i: (i, 0)), pl.BlockSpec((1, d), lambda i: (0, 0)), _full(w)],
        out_specs=[pl.BlockSpec((tm, wd), lambda i: (i, 0)) for wd in widths],
        out_shape=[jax.ShapeDtypeStruct((m, wd), F32) for wd in widths],
        compiler_params=_cparams("parallel"),
        name="proj",
    )(x, g.reshape(1, d), w)


def _outproj_kernel(a_ref, b_ref, wa_ref, wb_ref, x_ref, o_ref):
    o_ref[...] = x_ref[...] + _dot(a_ref[...], wa_ref[...]) + _dot(b_ref[...], wb_ref[...])


def _outproj(a, b, wa, wb, x):
    m, d = x.shape
    tm = _row_tile(m, 512)
    return pl.pallas_call(
        _outproj_kernel,
        grid=(m // tm,),
        in_specs=[
            pl.BlockSpec((tm, a.shape[1]), lambda i: (i, 0)),
            pl.BlockSpec((tm, b.shape[1]), lambda i: (i, 0)),
            _full(wa), _full(wb),
            pl.BlockSpec((tm, d), lambda i: (i, 0)),
        ],
        out_specs=pl.BlockSpec((tm, d), lambda i: (i, 0)),
        out_shape=jax.ShapeDtypeStruct((m, d), F32),
        compiler_params=_cparams("parallel"),
        name="outproj",
    )(a, b, wa, wb, x)


def _norm_kernel(x_ref, g_ref, o_ref):
    o_ref[...] = _rms(x_ref[...], g_ref[...])


def _final_norm(x, g):
    m, d = x.shape
    tm = _row_tile(m, 1024)
    return pl.pallas_call(
        _norm_kernel,
        grid=(m // tm,),
        in_specs=[pl.BlockSpec((tm, d), lambda i: (i, 0)), pl.BlockSpec((1, d), lambda i: (0, 0))],
        out_specs=pl.BlockSpec((tm, d), lambda i: (i, 0)),
        out_shape=jax.ShapeDtypeStruct((m, d), F32),
        compiler_params=_cparams("parallel"),
        name="final_norm",
    )(x, g.reshape(1, d))


def _xattn_core(x, g, wq, wo, mem_k, mem_v, seq_rows):
    d = x.shape[1]
    hd = d // X_HEADS
    q = (_dot(_rms(x, g).astype(BF16), wq) * (hd ** -0.5)).astype(BF16)
    nseq = x.shape[0] // seq_rows
    pairs = [(s, h) for s in range(nseq) for h in range(X_HEADS)]
    sc = [_dot_nt(q[s * seq_rows:(s + 1) * seq_rows, h * hd:(h + 1) * hd], mem_k(s, h)) for s, h in pairs]
    p = [jnp.exp(t - jnp.max(t, axis=-1, keepdims=True)) for t in sc]
    p = [t / jnp.sum(t, axis=-1, keepdims=True) for t in p]
    pv = [_dot(t.astype(BF16), mem_v(s, h)) for t, (s, h) in zip(p, pairs)]
    outs = [jnp.concatenate(pv[s * X_HEADS:(s + 1) * X_HEADS], axis=-1) for s in range(nseq)]
    o = outs[0] if nseq == 1 else jnp.concatenate(outs, axis=0)
    return x + _dot(o.astype(BF16), wo)


def _xattn_prompt_kernel(x_ref, g_ref, wq_ref, wo_ref, mk_ref, mv_ref, o_ref):
    x = x_ref[...]
    hd = x.shape[1] // X_HEADS
    mk = mk_ref[...].astype(BF16)
    mv = mv_ref[...].astype(BF16)
    o_ref[...] = _xattn_core(x, g_ref[...], wq_ref[...], wo_ref[...],
                             lambda s, h: mk[:, h * hd:(h + 1) * hd], lambda s, h: mv[:, h * hd:(h + 1) * hd],
                             x.shape[0])


def _xattn_prompt(x, g, wq, wo, mk, mv, seq):
    m, d = x.shape
    nm = mk.shape[1]
    tm = _row_tile(seq, 512)
    nt = seq // tm
    return pl.pallas_call(
        _xattn_prompt_kernel,
        grid=(m // tm,),
        in_specs=[
            pl.BlockSpec((tm, d), lambda i: (i, 0)),
            pl.BlockSpec((1, d), lambda i: (0, 0)),
            _full(wq), _full(wo),
            pl.BlockSpec((None, nm, d), lambda i: (i // nt, 0, 0)),
            pl.BlockSpec((None, nm, d), lambda i: (i // nt, 0, 0)),
        ],
        out_specs=pl.BlockSpec((tm, d), lambda i: (i, 0)),
        out_shape=jax.ShapeDtypeStruct((m, d), F32),
        compiler_params=_cparams("parallel"),
        name="xattn_prompt",
    )(x, g.reshape(1, d), wq, wo, mk, mv)


def _xattn_sample_kernel(x_ref, g_ref, wq_ref, wo_ref, mk_ref, mv_ref, o_ref, *, nseq):
    nc = x_ref.shape[1] // X_HEADS // LANES
    nm = mk_ref.shape[1] // (X_HEADS * nc)

    def head(ref, s, h):
        tiles = [ref[s, pl.ds(h + X_HEADS * c, nm, stride=X_HEADS * nc), :] for c in range(nc)]
        return jnp.concatenate(tiles, axis=1).astype(BF16)

    o_ref[...] = _xattn_core(x_ref[...], g_ref[...], wq_ref[...], wo_ref[...],
                             lambda s, h: head(mk_ref, s, h), lambda s, h: head(mv_ref, s, h),
                             x_ref.shape[0] // nseq)


def _xattn_sample(x, g, wq, wo, cache_k, cache_v, layer):
    m, d = x.shape
    depth, db, nm, nh, hd = cache_k.shape
    assert nh == X_HEADS and hd % LANES == 0
    nc = hd // LANES

    def rows_view(c):
        c = c.reshape(depth, db, nm, nh, nc, LANES)
        return jnp.transpose(c, (0, 1, 2, 4, 3, 5)).reshape(depth, db, nm * nc * nh, LANES)

    cache_k, cache_v = rows_view(cache_k), rows_view(cache_v)
    nseq = 4 if db % 4 == 0 else 1
    tm = nseq * (m // db)
    cache_spec = pl.BlockSpec((None, nseq, nm * nc * nh, LANES), lambda i: (layer, i, 0, 0))
    return pl.pallas_call(
        functools.partial(_xattn_sample_kernel, nseq=nseq),
        grid=(db // nseq,),
        in_specs=[
            pl.BlockSpec((tm, d), lambda i: (i, 0)),
            pl.BlockSpec((1, d), lambda i: (0, 0)),
            _full(wq), _full(wo), cache_spec, cache_spec,
        ],
        out_specs=pl.BlockSpec((tm, d), lambda i: (i, 0)),
        out_shape=jax.ShapeDtypeStruct((m, d), F32),
        compiler_params=_cparams("parallel"),
        name="xattn_sample",
    )(x, g.reshape(1, d), wq, wo, cache_k, cache_v)


def _split3(x):
    hi = x.astype(BF16)
    r = x - hi.astype(F32)
    mid = r.astype(BF16)
    lo = (r - mid.astype(F32)).astype(BF16)
    return hi, mid, lo


def _gla_kernel(q_ref, k_ref, v_ref, gs_ref, ra_ref, s0_ref, wg_ref, bg_ref, gn_ref, o_ref, sout_ref, st_ref,
                *, rows, chunk, nchunks, valid):
    t = pl.program_id(1)

    @pl.when(t == 0)
    def _():
        st_ref[...] = s0_ref[...]

    L = chunk
    r_io = lax.broadcasted_iota(jnp.int32, (L, L), 0)
    c_io = lax.broadcasted_iota(jnp.int32, (L, L), 1)
    tril = r_io >= c_io
    tril_bf = jnp.where(tril, 1.0, 0.0).astype(BF16)
    ones_bf = jnp.ones((L, GLA_DV), BF16)
    live = lax.broadcasted_iota(jnp.int32, (L, 1), 0) < valid

    def load(ref, r0):
        x = ref[pl.ds(r0, rows), :]
        if rows < L:
            x = jnp.concatenate([x, jnp.zeros((L - rows, x.shape[1]), x.dtype)], axis=0)
        return x

    def body(c, carry):
        r0 = pl.multiple_of(c * rows, rows)
        q = load(q_ref, r0) * (GLA_DK ** -0.5)
        k = load(k_ref, r0)
        v = load(v_ref, r0)
        ra = load(ra_ref, r0)
        z = _dot(load(gs_ref, r0).astype(BF16), wg_ref[...]) + bg_ref[...]
        lg = (jnp.minimum(z, 0.0) - jnp.log1p(jnp.exp(-jnp.abs(z)))) * (1.0 / GLA_TAU)
        if valid < L:
            lg = jnp.where(live, lg, 0.0)
            k = jnp.where(live, k, 0.0)
            v = jnp.where(live, v, 0.0)
        hi, mid, lo = _split3(lg)
        b = _dot(tril_bf, hi) + _dot(tril_bf, mid) + _dot(tril_bf, lo)
        bl = b[L - 1:L, :]
        qe = (q * jnp.exp(b)).astype(BF16)
        ke = (k * jnp.exp(-b)).astype(BF16)
        kd = (k * jnp.exp(bl - b)).astype(BF16)
        vb = v.astype(BF16)
        heads = range(GLA_HEADS)
        ks = [slice(h * GLA_DK, (h + 1) * GLA_DK) for h in heads]
        vs = [slice(h * GLA_DV, (h + 1) * GLA_DV) for h in heads]
        att = [jnp.where(tril, _dot_nt(qe[:, ks[h]], ke[:, ks[h]]), 0.0).astype(BF16) for h in heads]
        s_old = [st_ref[h] for h in heads]
        o_h = [_dot(qe[:, ks[h]], s_old[h].astype(BF16)) + _dot(att[h], vb[:, vs[h]]) for h in heads]
        dec = [_dot_tn(hi[:, ks[h]], ones_bf) + _dot_tn(mid[:, ks[h]], ones_bf) + _dot_tn(lo[:, ks[h]], ones_bf)
               for h in heads]
        upd = [_dot_tn(kd[:, ks[h]], vb[:, vs[h]]) for h in heads]
        for h in heads:
            st_ref[h] = s_old[h] * jnp.exp(dec[h]) + upd[h]
        gate = ra * _sigmoid(ra)
        o = jnp.concatenate([_rms(o_h[h], gn_ref[...]) for h in heads], axis=-1) * gate
        o_ref[pl.ds(r0, rows), :] = o[:rows].astype(o_ref.dtype)
        return carry

    if nchunks == 1:
        body(0, 0)
    else:
        lax.fori_loop(0, nchunks, body, 0)

    @pl.when(t == pl.num_programs(1) - 1)
    def _():
        sout_ref[...] = st_ref[...]


def _gla(q, k, v, gs, ra, s0, wg, bg, gn, *, seq, rows, chunk, valid):
    m = q.shape[0]
    nb = m // seq
    tt = _row_tile(seq, 512) if rows == chunk else rows
    nchunks = tt // rows
    nt = seq // tt
    row_spec = lambda w: pl.BlockSpec((tt, w), lambda b, t: (b * nt + t, 0))
    st_spec = pl.BlockSpec((None, GLA_HEADS, GLA_DK, GLA_DV), lambda b, t: (b, 0, 0, 0))
    return pl.pallas_call(
        functools.partial(_gla_kernel, rows=rows, chunk=chunk, nchunks=nchunks, valid=valid),
        grid=(nb, nt),
        in_specs=[row_spec(q.shape[1]), row_spec(k.shape[1]), row_spec(v.shape[1]), row_spec(gs.shape[1]),
                  row_spec(ra.shape[1]), st_spec, _full(wg), _full(bg), _full(gn)],
        out_specs=[row_spec(v.shape[1]), st_spec],
        out_shape=[jax.ShapeDtypeStruct((m, v.shape[1]), BF16),
                   jax.ShapeDtypeStruct((nb, GLA_HEADS, GLA_DK, GLA_DV), F32)],
        scratch_shapes=[pltpu.VMEM((GLA_HEADS, GLA_DK, GLA_DV), F32)],
        compiler_params=_cparams("parallel", "arbitrary"),
        name="gla",
    )(q, k, v, gs, ra, s0, wg, bg, gn)


def _rope(x, cos, sin_signed, group):
    w = x.shape[-1]
    ax = x.ndim - 1
    lane = lax.broadcasted_iota(jnp.int32, x.shape, ax)
    first = (lane % group) < (group // 2)
    partner = jnp.where(first, pltpu.roll(x, w - group // 2, ax), pltpu.roll(x, group // 2, ax))
    return x * cos + partner * sin_signed


def _rope_tables(pos, dim, width, reps):
    half = dim // 2
    inv = ROPE_THETA ** (-jnp.arange(half, dtype=F32) / half)
    ang = pos.astype(F32)[:, None] * inv[None, :]
    cos, sin = jnp.cos(ang), jnp.sin(ang)
    pad = jnp.zeros((pos.shape[0], width - dim), F32)
    c = jnp.concatenate([cos, cos, pad], axis=-1)
    s = jnp.concatenate([-sin, sin, pad], axis=-1)
    return jnp.tile(c, (1, reps)), jnp.tile(s, (1, reps))


MLA_QK = MLA_KV_LORA + LANES


def _mla_prep_kernel(ql_ref, kvl_ref, ks_ref, cos_ref, sin_ref, gq_ref, gkv_ref, wqn_ref, wqp_ref, wuk_ref,
                     q_ref, k_ref, ckv_ref, kpe_ref, *ct_ref, transposed):
    scale = (MLA_NOPE + MLA_ROPE) ** -0.5 * LOG2E
    hq = _rms(ql_ref[...], gq_ref[...]).astype(BF16)
    qn = _dot(hq, wqn_ref[...]).astype(BF16)
    cos, sin = cos_ref[...], sin_ref[...]
    qp = _rope(_dot(hq, wqp_ref[...]), cos, sin, MLA_ROPE)
    parts = []
    for h in range(MLA_HEADS):
        parts.append(_dot(qn[:, h * MLA_NOPE:(h + 1) * MLA_NOPE], wuk_ref[h]))
        parts.append(qp[:, h * LANES:(h + 1) * LANES])
    q = jnp.concatenate(parts, axis=-1) * scale
    ckv = _rms(kvl_ref[...], gkv_ref[...])
    ckv_ref[...] = ckv
    kpe = _rope(ks_ref[...], cos[:, :LANES], sin[:, :LANES], MLA_ROPE)
    kpe_ref[...] = kpe[:, :MLA_ROPE]
    k_ref[...] = jnp.concatenate([ckv, kpe], axis=-1).astype(BF16)
    if transposed:
        q_ref[...] = q.T.astype(BF16)
        ct_ref[0][...] = ckv.T.astype(BF16)
    else:
        q_ref[...] = q.astype(BF16)


def _mla_prep(q_lat, kv_lat, ks, cos, sin, gq, gkv, wqn, wqp, wuk, transposed):
    m = q_lat.shape[0]
    p = cos.shape[0]
    tm = _row_tile(math.gcd(m, p), 512)
    npos = p // tm
    row = lambda w: pl.BlockSpec((tm, w), lambda i: (i, 0))
    col = lambda w: pl.BlockSpec((w, tm), lambda i: (0, i))
    pos = lambda w: pl.BlockSpec((tm, w), lambda i: (i % npos, 0))
    hq, hp = MLA_HEADS * MLA_QK, MLA_HEADS * LANES
    out_specs = [col(hq) if transposed else row(hq), row(MLA_QK), row(MLA_KV_LORA), row(MLA_ROPE)]
    out_shape = [jax.ShapeDtypeStruct((hq, m) if transposed else (m, hq), BF16),
                 jax.ShapeDtypeStruct((m, MLA_QK), BF16),
                 jax.ShapeDtypeStruct((m, MLA_KV_LORA), F32), jax.ShapeDtypeStruct((m, MLA_ROPE), F32)]
    if transposed:
        out_specs.append(col(MLA_KV_LORA))
        out_shape.append(jax.ShapeDtypeStruct((MLA_KV_LORA, m), BF16))
    return pl.pallas_call(
        functools.partial(_mla_prep_kernel, transposed=transposed),
        grid=(m // tm,),
        in_specs=[row(MLA_Q_LORA), row(MLA_KV_LORA), row(LANES), pos(hp), pos(hp),
                  _full(gq), _full(gkv), _full(wqn), _full(wqp), _full(wuk)],
        out_specs=out_specs,
        out_shape=out_shape,
        compiler_params=_cparams("parallel"),
        name="mla_prep",
    )(q_lat, kv_lat, ks, cos, sin, gq, gkv, wqn, wqp, wuk)


def _stack_heads(x, width):
    return jnp.concatenate([x[:, h * width:(h + 1) * width] for h in range(MLA_HEADS)], axis=0)


def _softmax_update(s, v, m_ref, l_ref, acc_ref):
    m_prev = m_ref[...]
    m_new = jnp.maximum(m_prev, jnp.max(s, axis=-1, keepdims=True))
    alpha = jnp.exp2(m_prev - m_new)
    p = jnp.exp2(s - m_new)
    l_ref[...] = alpha * l_ref[...] + jnp.sum(p, axis=-1, keepdims=True)
    acc_ref[...] = alpha * acc_ref[...] + _dot(p.astype(BF16), v)
    m_ref[...] = m_new


def _softmax_init(m_ref, l_ref, acc_ref):
    m_ref[...] = jnp.full_like(m_ref, NEG_INF)
    l_ref[...] = jnp.zeros_like(l_ref)
    acc_ref[...] = jnp.zeros_like(acc_ref)


def _mla_out(acc_ref, l_ref, wuv_ref, rows):
    o_lat = (acc_ref[...] / l_ref[...]).astype(BF16)
    return jnp.concatenate([_dot(o_lat[h * rows:(h + 1) * rows], wuv_ref[h]) for h in range(MLA_HEADS)], axis=-1)


def _mla_prompt_kernel(qt_ref, k_ref, ct_ref, wuvt_ref, o_ref, qs_ref, m_ref, l_ref, acc_ref, *, tq, tk):
    qi, ki = pl.program_id(1), pl.program_id(2)
    last = ((qi + 1) * tq - 1) // tk
    cols = MLA_HEADS * tq

    @pl.when(ki == 0)
    def _():
        qt = qt_ref[...]
        qs_ref[...] = jnp.concatenate([qt[h * MLA_QK:(h + 1) * MLA_QK] for h in range(MLA_HEADS)], axis=1)
        _softmax_init(m_ref, l_ref, acc_ref)

    def step(masked):
        st = _dot(k_ref[...], qs_ref[...])
        if masked:
            kpos = ki * tk + lax.broadcasted_iota(jnp.int32, (tk, cols), 0)
            qpos = qi * tq + lax.broadcasted_iota(jnp.int32, (tk, cols), 1) % tq
            st = jnp.where(kpos <= qpos, st, NEG_INF)
        m_prev = m_ref[...]
        m_new = jnp.maximum(m_prev, jnp.max(st, axis=0, keepdims=True))
        alpha = jnp.exp2(m_prev - m_new)
        p = jnp.exp2(st - m_new)
        l_ref[...] = alpha * l_ref[...] + jnp.sum(p, axis=0, keepdims=True)
        acc_ref[...] = alpha * acc_ref[...] + _dot(ct_ref[...], p.astype(BF16))
        m_ref[...] = m_new

    crosses = (ki + 1) * tk - 1 > qi * tq

    @pl.when(jnp.logical_and(ki <= last, crosses))
    def _():
        step(True)

    @pl.when(jnp.logical_and(ki <= last, jnp.logical_not(crosses)))
    def _():
        step(False)

    @pl.when(ki == last)
    def _():
        o_lat = (acc_ref[...] / l_ref[...]).astype(BF16)
        ot = jnp.concatenate([_dot(wuvt_ref[h], o_lat[:, h * tq:(h + 1) * tq]) for h in range(MLA_HEADS)], axis=0)
        o_ref[...] = ot.T.astype(o_ref.dtype)


def _mla_prompt(qt, k, ct, wuvt, seq):
    m = k.shape[0]
    nb = m // seq
    tq = tk = _row_tile(seq, 512)
    nq, nk = seq // tq, seq // tk
    kblk = lambda b, qi, ki: b * nk + jnp.minimum(ki, ((qi + 1) * tq - 1) // tk)
    cols = MLA_HEADS * tq
    return pl.pallas_call(
        functools.partial(_mla_prompt_kernel, tq=tq, tk=tk),
        grid=(nb, nq, nk),
        in_specs=[pl.BlockSpec((qt.shape[0], tq), lambda b, qi, ki: (0, b * nq + qi)),
                  pl.BlockSpec((tk, k.shape[1]), lambda b, qi, ki: (kblk(b, qi, ki), 0)),
                  pl.BlockSpec((ct.shape[0], tk), lambda b, qi, ki: (0, kblk(b, qi, ki))),
                  _full(wuvt)],
        out_specs=pl.BlockSpec((tq, MLA_HEADS * MLA_V), lambda b, qi, ki: (b * nq + qi, 0)),
        out_shape=jax.ShapeDtypeStruct((m, MLA_HEADS * MLA_V), BF16),
        scratch_shapes=[pltpu.VMEM((MLA_QK, cols), BF16), pltpu.VMEM((1, cols), F32), pltpu.VMEM((1, cols), F32),
                        pltpu.VMEM((MLA_KV_LORA, cols), F32)],
        compiler_params=_cparams("parallel", "parallel", "arbitrary"),
        name="mla_prompt",
    )(qt, k, ct, wuvt)


def _mla_sample_kernel(pt_ref, q_ref, kn_ref, wuv_ref, *rest, npg, valid):
    ckv_pages, kpet_pages = rest[:npg], rest[npg:2 * npg]
    o_ref, qs_ref, m_ref, l_ref, acc_ref = rest[2 * npg:]
    g = pl.program_id(1)
    sp = q_ref.shape[0]

    @pl.when(g == 0)
    def _():
        qs_ref[...] = _stack_heads(q_ref[...], MLA_QK)
        _softmax_init(m_ref, l_ref, acc_ref)

    qs = qs_ref[...]
    qc, qp = qs[:, :MLA_KV_LORA], qs[:, MLA_KV_LORA:MLA_KV_LORA + MLA_ROPE]
    ngrp = math.gcd(npg, 4)
    per = npg // ngrp
    grp = range(ngrp)
    ckv = [jnp.concatenate([r[...].astype(BF16) for r in ckv_pages[g * per:(g + 1) * per]], axis=0) for g in grp]
    kpet = [jnp.concatenate([r[...].astype(BF16) for r in kpet_pages[g * per:(g + 1) * per]], axis=1) for g in grp]
    s = [_dot_nt(qc, ckv[g]) + _dot(qp, kpet[g]) for g in grp]
    mg = [jnp.max(s[g], axis=-1, keepdims=True) for g in grp]
    p = [jnp.exp2(s[g] - mg[g]) for g in grp]
    lg = [jnp.sum(p[g], axis=-1, keepdims=True) for g in grp]
    pv = [_dot(p[g].astype(BF16), ckv[g]) for g in grp]
    m_prev = m_ref[...]
    m_new = functools.reduce(jnp.maximum, mg, m_prev)
    alpha = jnp.exp2(m_prev - m_new)
    w = [jnp.exp2(mg[g] - m_new) for g in grp]
    l_ref[...] = alpha * l_ref[...] + sum(w[g] * lg[g] for g in grp)
    acc_ref[...] = alpha * acc_ref[...] + sum(w[g] * pv[g] for g in grp)
    m_ref[...] = m_new

    @pl.when(g == pl.num_programs(1) - 1)
    def _():
        kn = kn_ref[...]
        ckvn = kn[:, :MLA_KV_LORA]
        s = _dot_nt(qs, kn)
        rows = MLA_HEADS * sp
        qpos = lax.broadcasted_iota(jnp.int32, (rows, sp), 0) % sp
        kpos = lax.broadcasted_iota(jnp.int32, (rows, sp), 1)
        s = jnp.where(jnp.logical_and(kpos <= qpos, kpos < valid), s, NEG_INF)
        _softmax_update(s, ckvn, m_ref, l_ref, acc_ref)
        o_ref[...] = _mla_out(acc_ref, l_ref, wuv_ref, sp).astype(o_ref.dtype)


def _mla_sample(q, kn, wuv, cache_ckv, cache_kpet, layer, page_table, valid):
    m = q.shape[0]
    db, n_pages = page_table.shape
    sp = m // db
    npg = math.gcd(n_pages, MLA_PAGES_PER_STEP)
    rows = MLA_HEADS * sp
    qmap = lambda b, g, pt: (b, 0)

    def page_spec(shape, j):
        return pl.BlockSpec((None, None) + shape, lambda b, g, pt: (layer, pt[b * n_pages + g * npg + j], 0, 0))

    grid_spec = pltpu.PrefetchScalarGridSpec(
        num_scalar_prefetch=1,
        grid=(db, n_pages // npg),
        in_specs=[pl.BlockSpec((sp, q.shape[1]), qmap), pl.BlockSpec((sp, kn.shape[1]), qmap),
                  pl.BlockSpec(wuv.shape, lambda b, g, pt: (0, 0, 0))]
                 + [page_spec((PAGE_SIZE, MLA_KV_LORA), j) for j in range(npg)]
                 + [page_spec((MLA_ROPE, PAGE_SIZE), j) for j in range(npg)],
        out_specs=pl.BlockSpec((sp, MLA_HEADS * MLA_V), qmap),
        scratch_shapes=[pltpu.VMEM((rows, MLA_QK), BF16), pltpu.VMEM((rows, 1), F32), pltpu.VMEM((rows, 1), F32),
                        pltpu.VMEM((rows, MLA_KV_LORA), F32)],
    )
    return pl.pallas_call(
        functools.partial(_mla_sample_kernel, npg=npg, valid=valid),
        grid_spec=grid_spec,
        out_shape=jax.ShapeDtypeStruct((m, MLA_HEADS * MLA_V), BF16),
        compiler_params=_cparams("parallel", "arbitrary"),
        name="mla_sample",
    )(page_table.reshape(-1), q, kn, wuv, *([cache_ckv] * npg), *([cache_kpet] * npg))


def _moba_prep_kernel(q_ref, k_ref, v_ref, cos_ref, sin_ref, qr_ref, kr_ref, *rest, nblk):
    cos, sin = cos_ref[...], sin_ref[...]
    qr_ref[...] = _rope(q_ref[...], cos, sin, MOBA_HD)
    kr = _rope(k_ref[...], cos, sin, MOBA_HD)
    kr_ref[...] = kr
    if nblk:
        kb_ref, vt_ref, km_ref = rest
        kb_ref[...] = kr.astype(BF16)
        vt_ref[...] = v_ref[...].T.astype(BF16)
        for j in range(nblk):
            km_ref[j] = jnp.mean(kr[j * MOBA_BLOCK:(j + 1) * MOBA_BLOCK], axis=0, keepdims=True)


def _moba_prep(q, k, v, cos, sin, prompt):
    m, w = q.shape
    p = cos.shape[0]
    tm = _row_tile(math.gcd(m, p), 512)
    npos = p // tm
    if prompt:
        assert tm % MOBA_BLOCK == 0
    nblk = tm // MOBA_BLOCK if prompt else 0
    row = pl.BlockSpec((tm, w), lambda i: (i, 0))
    pos = pl.BlockSpec((tm, w), lambda i: (i % npos, 0))
    out_specs = [row, row]
    out_shape = [jax.ShapeDtypeStruct((m, w), F32), jax.ShapeDtypeStruct((m, w), F32)]
    if prompt:
        out_specs += [row, pl.BlockSpec((w, tm), lambda i: (0, i)), pl.BlockSpec((nblk, 1, w), lambda i: (i, 0, 0))]
        out_shape += [jax.ShapeDtypeStruct((m, w), BF16), jax.ShapeDtypeStruct((w, m), BF16),
                      jax.ShapeDtypeStruct((m // MOBA_BLOCK, 1, w), F32)]
    return pl.pallas_call(
        functools.partial(_moba_prep_kernel, nblk=nblk),
        grid=(m // tm,),
        in_specs=[row, row, row, pos, pos],
        out_specs=out_specs,
        out_shape=out_shape,
        compiler_params=_cparams("parallel"),
        name="moba_prep",
    )(q, k, v, cos, sin)


def _top_rows(gate, idx, count, limit, axis):
    sel = jnp.zeros(gate.shape, F32)
    for _ in range(count):
        mx = jnp.max(gate, axis=axis, keepdims=True)
        first = jnp.min(jnp.where(gate == mx, idx, limit), axis=axis, keepdims=True)
        pick = idx == first
        sel = jnp.where(pick, 1.0, sel)
        gate = jnp.where(pick, BELOW_NEG_INF, gate)
    return sel


def _moba_prompt_kernel(q_ref, k_ref, vt_ref, km_ref, o_ref, qt_ref, sel_ref, m_ref, l_ref, acc_ref, *, nb, topn):
    qi = pl.program_id(1)
    blk, hd = MOBA_BLOCK, MOBA_HD
    b_idx = lax.broadcasted_iota(jnp.int32, (nb, blk), 0)
    past = b_idx < qi
    causal = lax.broadcasted_iota(jnp.int32, (blk, blk), 0) <= lax.broadcasted_iota(jnp.int32, (blk, blk), 1)
    r0 = pl.multiple_of(qi * blk, blk)

    heads = range(MOBA_HEADS)
    hsl = [slice(h * hd, (h + 1) * hd) for h in heads]
    q = [q_ref[:, hsl[h]] for h in heads]
    qt = [(q[h] * (hd ** -0.5 * LOG2E)).T.astype(BF16) for h in heads]
    st = [jnp.where(causal, _dot(k_ref[pl.ds(r0, blk), hsl[h]], qt[h]), NEG_INF) for h in heads]
    m0 = [jnp.max(st[h], axis=0, keepdims=True) for h in heads]
    p0 = [jnp.exp2(st[h] - m0[h]) for h in heads]
    pv0 = [_dot(vt_ref[hsl[h], pl.ds(r0, blk)], p0[h].astype(BF16)) for h in heads]
    gate = [jnp.where(past, _dot_nt_f32(km_ref[:, hsl[h]], q[h]), NEG_INF) for h in heads]
    for h in heads:
        qt_ref[h] = qt[h]
        m_ref[h] = m0[h]
        l_ref[h] = jnp.sum(p0[h], axis=0, keepdims=True)
        acc_ref[h] = pv0[h]
        sel_ref[h] = jnp.where(past, _top_rows(gate[h], b_idx, topn, nb, 0), 0.0)

    def body(n, carry):
        rn = pl.multiple_of(n * blk, blk)
        s = [_dot(k_ref[pl.ds(rn, blk), hsl[h]], qt_ref[h]) for h in heads]
        s = [jnp.where(sel_ref[h, pl.ds(n, 1), :] > 0.0, s[h], NEG_INF) for h in heads]
        m_old = [m_ref[h] for h in heads]
        m_new = [jnp.maximum(m_old[h], jnp.max(s[h], axis=0, keepdims=True)) for h in heads]
        alpha = [jnp.exp2(m_old[h] - m_new[h]) for h in heads]
        p = [jnp.exp2(s[h] - m_new[h]) for h in heads]
        pv = [_dot(vt_ref[hsl[h], pl.ds(rn, blk)], p[h].astype(BF16)) for h in heads]
        for h in heads:
            m_ref[h] = m_new[h]
            l_ref[h] = alpha[h] * l_ref[h] + jnp.sum(p[h], axis=0, keepdims=True)
            acc_ref[h] = alpha[h] * acc_ref[h] + pv[h]
        return carry

    lax.fori_loop(0, qi, body, 0)
    o_ref[...] = jnp.concatenate([(acc_ref[h] / l_ref[h]).T for h in range(MOBA_HEADS)], axis=1).astype(o_ref.dtype)


def _moba_prompt(qr, kb, vt, km, seq):
    m, hw = qr.shape
    nbatch = m // seq
    assert seq % MOBA_BLOCK == 0
    nb = seq // MOBA_BLOCK
    topn = min(MOBA_TOPK, nb)
    qmap = lambda b, qi: (b * nb + qi, 0)
    once = pl.Buffered(1)
    return pl.pallas_call(
        functools.partial(_moba_prompt_kernel, nb=nb, topn=topn),
        grid=(nbatch, nb),
        in_specs=[pl.BlockSpec((MOBA_BLOCK, hw), qmap),
                  pl.BlockSpec((seq, hw), lambda b, qi: (b, 0), pipeline_mode=once),
                  pl.BlockSpec((hw, seq), lambda b, qi: (0, b), pipeline_mode=once),
                  pl.BlockSpec((None, nb, hw), lambda b, qi: (b, 0, 0))],
        out_specs=pl.BlockSpec((MOBA_BLOCK, hw), qmap),
        out_shape=jax.ShapeDtypeStruct((m, hw), BF16),
        scratch_shapes=[pltpu.VMEM((MOBA_HEADS, MOBA_HD, MOBA_BLOCK), BF16),
                        pltpu.VMEM((MOBA_HEADS, nb, MOBA_BLOCK), F32),
                        pltpu.VMEM((MOBA_HEADS, 1, MOBA_BLOCK), F32),
                        pltpu.VMEM((MOBA_HEADS, 1, MOBA_BLOCK), F32),
                        pltpu.VMEM((MOBA_HEADS, MOBA_HD, MOBA_BLOCK), F32)],
        compiler_params=_cparams("parallel", "arbitrary"),
        name="moba_prompt",
    )(qr, kb, vt, km)


def _moba_sample_kernel(pt_ref, q_ref, kn_ref, vn_ref, *rest, npg, ng, nb, topn, valid):
    k_pages, v_pages = rest[:npg], rest[npg:2 * npg]
    o_ref, qrow_ref, s_ref, p_ref, km_ref, acc_ref, pown_ref, l_ref = rest[2 * npg:]
    g = pl.program_id(1)
    sp = q_ref.shape[0]
    hw = MOBA_HEADS * MOBA_HD
    ppb = MOBA_BLOCK // PAGE_SIZE
    nbs = npg // ppb
    scale = MOBA_HD ** -0.5

    def query_rows():
        qt = jnp.concatenate([q_ref[...]] * (LANES // sp), axis=0)
        rh = lax.broadcasted_iota(jnp.int32, (LANES, hw), 0) // sp
        lh = lax.broadcasted_iota(jnp.int32, (LANES, hw), 1) // MOBA_HD
        return jnp.where(rh == lh, qt, 0.0)

    def page_heads(ref):
        return [ref[pl.ds(h, PAGE_SIZE, stride=MOBA_HEADS), :] for h in range(MOBA_HEADS)]

    @pl.when(g == 0)
    def _():
        qrow_ref[...] = (query_rows() * scale).astype(BF16)
        km_ref[...] = jnp.zeros_like(km_ref)
        acc_ref[...] = jnp.zeros_like(acc_ref)

    @pl.when(g < ng)
    def _():
        qrow = qrow_ref[...]
        base = pl.multiple_of(g * (npg * PAGE_SIZE), npg * PAGE_SIZE)
        sums, kall = [], []
        for j in range(npg):
            xh = page_heads(k_pages[j])
            kall.append(jnp.concatenate([t.astype(BF16) for t in xh], axis=1))
            sums.append(jnp.concatenate([jnp.sum(t, axis=0, keepdims=True) for t in xh], axis=1))
        s_ref[pl.ds(base, npg * PAGE_SIZE), :] = _dot_nt(jnp.concatenate(kall, axis=0), qrow)
        blk_row = lax.broadcasted_iota(jnp.int32, (nb, 1), 0)
        km = km_ref[...]
        for jb in range(nbs):
            mean = sum(sums[jb * ppb:(jb + 1) * ppb]) * (1.0 / MOBA_BLOCK)
            km = km + jnp.where(blk_row == g * nbs + jb, mean, 0.0)
        km_ref[...] = km

    @pl.when(g == ng - 1)
    def _():
        b_idx = lax.broadcasted_iota(jnp.int32, (nb, LANES), 0)
        sel = _top_rows(_dot_nt_f32(km_ref[...], query_rows()), b_idx, topn, nb, 0)
        s_own = _dot_nt(kn_ref[...].astype(BF16), qrow_ref[...])
        kt = lax.broadcasted_iota(jnp.int32, (sp, LANES), 0)
        qt = lax.broadcasted_iota(jnp.int32, (sp, LANES), 1) % sp
        s_own = jnp.where(jnp.logical_and(kt <= qt, kt < valid), s_own, NEG_INF)

        def masked(n):
            return jnp.where(sel[n:n + 1, :] > 0.0, s_ref[n * MOBA_BLOCK:(n + 1) * MOBA_BLOCK, :], NEG_INF)

        mx = jnp.max(s_own, axis=0, keepdims=True)
        for n in range(nb):
            mx = jnp.maximum(mx, jnp.max(masked(n), axis=0, keepdims=True))
        p_own = jnp.exp(s_own - mx)
        l = jnp.sum(p_own, axis=0, keepdims=True)
        for n in range(nb):
            pn = jnp.exp(masked(n) - mx)
            l = l + jnp.sum(pn, axis=0, keepdims=True)
            p_ref[n * MOBA_BLOCK:(n + 1) * MOBA_BLOCK, :] = pn.astype(BF16)
        pown_ref[...] = jnp.concatenate([p_own, jnp.zeros((LANES - sp, LANES), F32)], axis=0).astype(BF16)
        l_ref[...] = jnp.broadcast_to(l, l_ref.shape)

    @pl.when(g >= ng)
    def _():
        base = pl.multiple_of((g - ng) * (npg * PAGE_SIZE), npg * PAGE_SIZE)
        vall = jnp.concatenate(
            [jnp.concatenate([t.astype(BF16) for t in page_heads(v_pages[j])], axis=1) for j in range(npg)], axis=0)
        acc_ref[...] += _dot_tn(p_ref[pl.ds(base, npg * PAGE_SIZE), :], vall)

    @pl.when(g == 2 * ng - 1)
    def _():
        vn = jnp.concatenate([vn_ref[...], jnp.zeros((LANES - sp, hw), F32)], axis=0).astype(BF16)
        acc = acc_ref[...] + _dot_tn(pown_ref[...], vn)
        o = acc / l_ref[...].T[:, :1]
        o_ref[...] = jnp.concatenate(
            [o[h * sp:(h + 1) * sp, h * MOBA_HD:(h + 1) * MOBA_HD] for h in range(MOBA_HEADS)], axis=1
        ).astype(o_ref.dtype)


def _moba_sample(qr, kr, v, cache_k, cache_v, layer, page_table, valid):
    m, hw = qr.shape
    cache_k = cache_k.reshape(cache_k.shape[:2] + (PAGE_SIZE * MOBA_HEADS, MOBA_HD))
    cache_v = cache_v.reshape(cache_v.shape[:2] + (PAGE_SIZE * MOBA_HEADS, MOBA_HD))
    db, n_pages = page_table.shape
    sp = m // db
    past = n_pages * PAGE_SIZE
    assert past % MOBA_BLOCK == 0
    nb = past // MOBA_BLOCK
    topn = min(MOBA_TOPK, nb)
    ppb = MOBA_BLOCK // PAGE_SIZE
    npg = math.gcd(n_pages, MOBA_PAGES_PER_STEP)
    assert npg % ppb == 0
    ng = n_pages // npg
    qmap = lambda b, g, pt: (b, 0)
    blk = (None, None, PAGE_SIZE * MOBA_HEADS, MOBA_HD)

    def k_spec(j):
        return pl.BlockSpec(blk, lambda b, g, pt: (layer, pt[b * n_pages + jnp.minimum(g, ng - 1) * npg + j], 0, 0))

    def v_spec(j):
        return pl.BlockSpec(blk, lambda b, g, pt: (layer, pt[b * n_pages + jnp.maximum(g - ng, 0) * npg + j], 0, 0))

    grid_spec = pltpu.PrefetchScalarGridSpec(
        num_scalar_prefetch=1,
        grid=(db, 2 * ng),
        in_specs=[pl.BlockSpec((sp, hw), qmap)] * 3 + [k_spec(j) for j in range(npg)] + [v_spec(j) for j in range(npg)],
        out_specs=pl.BlockSpec((sp, hw), qmap),
        scratch_shapes=[pltpu.VMEM((LANES, hw), BF16), pltpu.VMEM((past, LANES), F32), pltpu.VMEM((past, LANES), BF16),
                        pltpu.VMEM((nb, hw), F32), pltpu.VMEM((LANES, hw), F32), pltpu.VMEM((LANES, LANES), BF16),
                        pltpu.VMEM((SUBLANES, LANES), F32)],
    )
    return pl.pallas_call(
        functools.partial(_moba_sample_kernel, npg=npg, ng=ng, nb=nb, topn=topn, valid=valid),
        grid_spec=grid_spec,
        out_shape=jax.ShapeDtypeStruct((m, hw), BF16),
        compiler_params=_cparams("parallel", "arbitrary"),
        name="moba_sample",
    )(page_table.reshape(-1), qr, kr, v, *([cache_k] * npg), *([cache_v] * npg))


def _s5_disc_kernel(lr_ref, li_ref, dt_ref, bre_ref, bim_ref, pwr_ref, pwi_ref, bbr_ref, bbi_ref):
    lr, li, dt = lr_ref[...], li_ref[...], dt_ref[...]
    k = (lax.broadcasted_iota(jnp.int32, pwr_ref.shape, 0) + 1).astype(F32)
    mag = jnp.exp(k * (lr * dt))
    ang = k * (li * dt)
    pwr = mag * jnp.cos(ang)
    pwi = mag * jnp.sin(ang)
    pwr_ref[...] = pwr
    pwi_ref[...] = pwi
    nr, ni = pwr[0:1] - 1.0, pwi[0:1]
    den = lr * lr + li * li
    f_re = (nr * lr + ni * li) / den
    f_im = (ni * lr - nr * li) / den
    bre, bim = bre_ref[...], bim_ref[...]
    bbr_ref[...] = f_re * bre - f_im * bim
    bbi_ref[...] = f_re * bim + f_im * bre


def _s5_discretize(lam_re, lam_im, log_dt, b_re, b_im):
    n = S5_N
    flat = lambda a: a.astype(F32).reshape(1, n)
    dt = jnp.exp(jnp.repeat(log_dt.astype(F32), S5_STATE)).reshape(1, n)
    bt = lambda b: jnp.transpose(b.astype(F32), (2, 0, 1)).reshape(S5_GROUP, n)
    args = (flat(lam_re), flat(lam_im), dt, bt(b_re), bt(b_im))
    return pl.pallas_call(
        _s5_disc_kernel,
        in_specs=[_full(a) for a in args],
        out_shape=[jax.ShapeDtypeStruct((SUBLANES, n), F32)] * 2 + [jax.ShapeDtypeStruct((S5_GROUP, n), F32)] * 2,
        name="s5_discretize",
    )(*args)


def _s5_block_diag(pieces):
    r, g, c = pieces.shape
    eye = jnp.eye(g, dtype=pieces.dtype)
    return jnp.einsum('rgc,gh->grhc', pieces, eye).reshape(g * r, g * c)


def _s5_kernel(u_ref, bb_ref, pwr_ref, pwi_ref, cc_ref, d_ref, wg_ref, bg_ref, *rest, chain):
    n = S5_N
    u = u_ref[...]
    tt = u.shape[0]
    ngrp = tt // SUBLANES
    bu = _dot(u.astype(BF16), bb_ref[...])
    xr = bu[:, :n].reshape(ngrp, SUBLANES, n)
    xi = bu[:, n:].reshape(ngrp, SUBLANES, n)
    row = lax.broadcasted_iota(jnp.int32, (ngrp, SUBLANES, n), 1)
    for d in (1, 2, 4):
        pr, pi = pwr_ref[d - 1:d, :], pwi_ref[d - 1:d, :]
        sr, si = pltpu.roll(xr, d, 1), pltpu.roll(xi, d, 1)
        keep = row >= d
        xr, xi = (xr + jnp.where(keep, pr * sr - pi * si, 0.0),
                  xi + jnp.where(keep, pr * si + pi * sr, 0.0))
    pw_r, pw_i = pwr_ref[...], pwi_ref[...]

    if chain:
        o_ref, sr_ref, si_ref, xr_s, xi_s, cr_s, ci_s = rest
        t = pl.program_id(1)

        @pl.when(t == 0)
        def _():
            cr_s[...] = jnp.zeros_like(cr_s)
            ci_s[...] = jnp.zeros_like(ci_s)

        xr_s[...] = xr
        xi_s[...] = xi

        def body(i, carry):
            cr, ci = carry
            a = xr_s[i] + pw_r * cr - pw_i * ci
            b = xi_s[i] + pw_r * ci + pw_i * cr
            xr_s[i] = a
            xi_s[i] = b
            return a[SUBLANES - 1:SUBLANES], b[SUBLANES - 1:SUBLANES]

        cr, ci = lax.fori_loop(0, ngrp, body, (cr_s[...], ci_s[...]))
        cr_s[...] = cr
        ci_s[...] = ci
        xr, xi = xr_s[...], xi_s[...]

        @pl.when(t == pl.num_programs(1) - 1)
        def _():
            sr_ref[...] = cr
            si_ref[...] = ci
    else:
        s0r_ref, s0i_ref, o_ref, xr_ref, xi_ref = rest
        s0r = s0r_ref[...].reshape(ngrp, SUBLANES, n)
        s0i = s0i_ref[...].reshape(ngrp, SUBLANES, n)
        xr, xi = xr + pw_r * s0r - pw_i * s0i, xi + pw_r * s0i + pw_i * s0r
        xr_ref[...] = xr.reshape(tt, n)
        xi_ref[...] = xi.reshape(tt, n)

    xcat = jnp.concatenate([xr.reshape(tt, n), xi.reshape(tt, n)], axis=1).astype(BF16)
    y = _dot(xcat, cc_ref[...]) + d_ref[...] * u
    y = y * (0.5 * (1.0 + jnp.tanh(math.sqrt(2.0 / math.pi) * (y + 0.044715 * (y * y * y)))))
    z = _dot(y.astype(BF16), wg_ref[...]) + bg_ref[...]
    o_ref[...] = (y * _sigmoid(z)).astype(o_ref.dtype)


def _s5_prompt(u, bb, pwr, pwi, cc, d, wg, bg, seq):
    m, w = u.shape
    nbatch = m // seq
    tt = _row_tile(seq, 256)
    nt = seq // tt
    n = S5_N
    row = pl.BlockSpec((tt, w), lambda b, t: (b * nt + t, 0))
    st = pl.BlockSpec((None, 1, n), lambda b, t: (b, 0, 0))
    return pl.pallas_call(
        functools.partial(_s5_kernel, chain=True),
        grid=(nbatch, nt),
        in_specs=[row, _full(bb), _full(pwr), _full(pwi), _full(cc), _full(d), _full(wg), _full(bg)],
        out_specs=[row, st, st],
        out_shape=[jax.ShapeDtypeStruct((m, w), BF16), jax.ShapeDtypeStruct((nbatch, 1, n), F32),
                   jax.ShapeDtypeStruct((nbatch, 1, n), F32)],
        scratch_shapes=[pltpu.VMEM((tt // SUBLANES, SUBLANES, n), F32), pltpu.VMEM((tt // SUBLANES, SUBLANES, n), F32),
                        pltpu.VMEM((1, n), F32), pltpu.VMEM((1, n), F32)],
        compiler_params=_cparams("parallel", "arbitrary"),
        name="s5_prompt",
    )(u, bb, pwr, pwi, cc, d, wg, bg)


def _s5_sample(u, bb, pwr, pwi, cc, d, wg, bg, s0r, s0i):
    m, w = u.shape
    n = S5_N
    tt = _row_tile(m, 256)
    row = pl.BlockSpec((tt, w), lambda i: (i, 0))
    st = pl.BlockSpec((tt, n), lambda i: (i, 0))
    return pl.pallas_call(
        functools.partial(_s5_kernel, chain=False),
        grid=(m // tt,),
        in_specs=[row, _full(bb), _full(pwr), _full(pwi), _full(cc), _full(d), _full(wg), _full(bg), st, st],
        out_specs=[row, st, st],
        out_shape=[jax.ShapeDtypeStruct((m, w), BF16), jax.ShapeDtypeStruct((m, n), F32),
                   jax.ShapeDtypeStruct((m, n), F32)],
        compiler_params=_cparams("parallel"),
        name="s5_sample",
    )(u, bb, pwr, pwi, cc, d, wg, bg, s0r, s0i)


def kernel(x_prompt, x_sample, cache_mem_k, cache_mem_v, state_gla, cache_mla_ckv, cache_mla_kpe, cache_moba_k, cache_moba_v, state_s5, page_table, mem_prompt, g_ffn1, w_ffn1_gate, w_ffn1_up, w_ffn1_down, g_mix, w_in_a, w_gla_gate, b_gla_gate, g_gla_norm, g_mla_q, w_mla_qb, g_mla_kv, w_mla_kvb, w_out_a, w_in_b, s5_lam_re, s5_lam_im, s5_log_dt, s5_b_re, s5_b_im, s5_c_re, s5_c_im, s5_d, w_s5_glu, b_s5_glu, w_out_b, g_xattn, w_xq, w_xk, w_xv, w_xo, g_ffn2, w_ffn2_gate, w_ffn2_up, w_ffn2_down, g_final):
    B, T, D = x_prompt.shape
    DB, S, _ = x_sample.shape
    depth = g_ffn1.shape[0]
    n_pages = page_table.shape[1]
    past = n_pages * PAGE_SIZE
    SP = SAMPLE_PAD
    assert S <= SP
    n_mem = mem_prompt.shape[1]
    bf = lambda a: a.astype(BF16)

    xp = x_prompt.reshape(B * T, D)
    xs = jnp.pad(x_sample, ((0, 0), (0, SP - S), (0, 0))).reshape(DB * SP, D)
    pos_p = jnp.arange(T, dtype=jnp.int32)
    pos_s = jnp.tile(past + jnp.arange(SP, dtype=jnp.int32), DB)
    mla_tab_p = _rope_tables(pos_p, MLA_ROPE, LANES, MLA_HEADS)
    mla_tab_s = _rope_tables(pos_s, MLA_ROPE, LANES, MLA_HEADS)
    moba_tab_p = _rope_tables(pos_p, MOBA_HD, MOBA_HD, MOBA_HEADS)
    moba_tab_s = _rope_tables(pos_s, MOBA_HD, MOBA_HD, MOBA_HEADS)
    mem2d = mem_prompt.reshape(B * n_mem, D)
    cache_kpet = jnp.swapaxes(cache_mla_kpe, 2, 3)
    ones_d = jnp.ones((D,), F32)
    gla_chunk = math.gcd(T, GLA_CHUNK)

    def unpad(a):
        return a.reshape((DB, SP) + a.shape[1:])[:, :S]

    mem_k_p, mem_v_p = [], []
    gla_p, gla_s = [], []
    ckv_p, kpe_p, ckv_s, kpe_s = [], [], [], []
    mk_p, mv_p, mk_s, mv_s = [], [], [], []
    s5_p, s5_s = [], []

    for l in range(depth):
        i = l // 2
        wg1, wu1, wd1 = bf(w_ffn1_gate[l]), bf(w_ffn1_up[l]), bf(w_ffn1_down[l])
        xp = _ffn(xp, g_ffn1[l], wg1, wu1, wd1)
        xs = _ffn(xs, g_ffn1[l], wg1, wu1, wd1)

        if l % 2 == 0:
            qa_w, ka_w, va_w, ga_w, ra_w, ql_w, kvl_w, kr_w = jnp.split(
                w_in_a[i], [256, 512, 1024, 1040, 1552, 1936, 2192], axis=1)
            padc = lambda w: jnp.pad(w, ((0, 0), (0, LANES - w.shape[1])))
            w_in = bf(jnp.concatenate([va_w, ra_w, qa_w, ka_w, kvl_w, ql_w, padc(kr_w), padc(ga_w)], axis=1))
            widths = (512, 512, 256, 256, 256, 384, LANES, LANES)
            wgate = bf(jnp.pad(w_gla_gate[i], ((0, LANES - GLA_GATE_RANK), (0, 0))))
            bgate = b_gla_gate[i].reshape(1, -1)
            gnorm = g_gla_norm[i].reshape(1, -1)
            wqb = w_mla_qb[i].reshape(MLA_Q_LORA, MLA_HEADS, MLA_NOPE + MLA_ROPE)
            wqn = bf(wqb[:, :, :MLA_NOPE].reshape(MLA_Q_LORA, MLA_HEADS * MLA_NOPE))
            wqp = bf(jnp.pad(wqb[:, :, MLA_NOPE:], ((0, 0), (0, 0), (0, LANES - MLA_ROPE))).reshape(MLA_Q_LORA, MLA_HEADS * LANES))
            wkvb = w_mla_kvb[i].reshape(MLA_KV_LORA, MLA_HEADS, MLA_NOPE + MLA_V)
            wuk = bf(jnp.transpose(wkvb[:, :, :MLA_NOPE], (1, 2, 0)))
            wuv = bf(jnp.transpose(wkvb[:, :, MLA_NOPE:], (1, 0, 2)))
            wuvt = bf(jnp.transpose(wkvb[:, :, MLA_NOPE:], (1, 2, 0)))
            gq = g_mla_q[i].reshape(1, -1)
            gkv = g_mla_kv[i].reshape(1, -1)
            wo = bf(w_out_a[i])
            wo_a, wo_b = wo[:GLA_HEADS * GLA_DV], wo[GLA_HEADS * GLA_DV:]

            def even(x, tabs, s0, prompt):
                va, ra, qa, ka, kvl, ql, ksl, gsl = _proj(x, g_mix[l], w_in, widths)
                if prompt:
                    o_a, st = _gla(qa, ka, va, gsl, ra, s0, wgate, bgate, gnorm,
                                   seq=T, rows=gla_chunk, chunk=gla_chunk, valid=gla_chunk)
                else:
                    o_a, st = _gla(qa, ka, va, gsl, ra, s0, wgate, bgate, gnorm,
                                   seq=SP, rows=SP, chunk=2 * SP, valid=S)
                if prompt:
                    q_t, k_mla, ckv, kpe, c_t = _mla_prep(ql, kvl, ksl, tabs[0], tabs[1], gq, gkv, wqn, wqp, wuk, True)
                    o_b = _mla_prompt(q_t, k_mla, c_t, wuvt, T)
                else:
                    q_mla, k_mla, ckv, kpe = _mla_prep(ql, kvl, ksl, tabs[0], tabs[1], gq, gkv, wqn, wqp, wuk, False)
                    o_b = _mla_sample(q_mla, k_mla, wuv, cache_mla_ckv, cache_kpet, i, page_table, S)
                return _outproj(o_a, o_b, wo_a, wo_b, x), st, ckv, kpe

            xp, st, ckv, kpe = even(xp, mla_tab_p, jnp.zeros((B, GLA_HEADS, GLA_DK, GLA_DV), F32), True)
            gla_p.append(st)
            ckv_p.append(ckv.reshape(B, T, MLA_KV_LORA))
            kpe_p.append(kpe.reshape(B, T, MLA_ROPE))
            xs, st, ckv, kpe = even(xs, mla_tab_s, state_gla[i], False)
            gla_s.append(st)
            ckv_s.append(unpad(ckv))
            kpe_s.append(unpad(kpe))
        else:
            hw = MOBA_HEADS * MOBA_HD
            w_in = bf(w_in_b[i])
            widths = (hw, hw, hw, S5_WIDTH)
            pwr, pwi, bbr, bbi = _s5_discretize(s5_lam_re[i], s5_lam_im[i], s5_log_dt[i], s5_b_re[i], s5_b_im[i])
            to_blocks = lambda a: a.reshape(a.shape[0], S5_GROUPS, S5_STATE)
            bb = bf(jnp.concatenate([_s5_block_diag(to_blocks(bbr)), _s5_block_diag(to_blocks(bbi))], axis=1))
            c_t = lambda c: jnp.transpose(c.astype(F32), (2, 0, 1))
            cc = bf(jnp.concatenate([_s5_block_diag(c_t(s5_c_re[i])), -_s5_block_diag(c_t(s5_c_im[i]))], axis=0))
            dsk = s5_d[i].reshape(1, -1)
            wglu = bf(w_s5_glu[i])
            bglu = b_s5_glu[i].reshape(1, -1)
            wo = bf(w_out_b[i])
            wo_c, wo_d = wo[:hw], wo[hw:]

            q, k, v, u = _proj(xp, g_mix[l], w_in, widths)
            qr, kr, kb, vt, km = _moba_prep(q, k, v, moba_tab_p[0], moba_tab_p[1], True)
            o_c = _moba_prompt(qr, kb, vt, km.reshape(B, T // MOBA_BLOCK, hw), T)
            o_d, sr, si = _s5_prompt(u, bb, pwr, pwi, cc, dsk, wglu, bglu, T)
            xp = _outproj(o_c, o_d, wo_c, wo_d, xp)
            mk_p.append(kr.reshape(B, T, MOBA_HEADS, MOBA_HD))
            mv_p.append(v.reshape(B, T, MOBA_HEADS, MOBA_HD))
            s5_p.append(jnp.stack([sr.reshape(B, S5_GROUPS, S5_STATE), si.reshape(B, S5_GROUPS, S5_STATE)], axis=-1))

            q, k, v, u = _proj(xs, g_mix[l], w_in, widths)
            qr, kr = _moba_prep(q, k, v, moba_tab_s[0], moba_tab_s[1], False)
            o_c = _moba_sample(qr, kr, v, cache_moba_k, cache_moba_v, i, page_table, S)
            s0 = state_s5[i].astype(F32).reshape(DB, S5_N, 2)
            s0r = jnp.repeat(s0[..., 0], SP, axis=0)
            s0i = jnp.repeat(s0[..., 1], SP, axis=0)
            o_d, xr, xi = _s5_sample(u, bb, pwr, pwi, cc, dsk, wglu, bglu, s0r, s0i)
            xs = _outproj(o_c, o_d, wo_c, wo_d, xs)
            mk_s.append(unpad(kr).reshape(DB, S, MOBA_HEADS, MOBA_HD))
            mv_s.append(unpad(v).reshape(DB, S, MOBA_HEADS, MOBA_HD))
            last = lambda a: a.reshape(DB, SP, S5_GROUPS, S5_STATE)[:, S - 1]
            s5_s.append(jnp.stack([last(xr), last(xi)], axis=-1))

        mkp, mvp = _proj(mem2d, ones_d, bf(jnp.concatenate([w_xk[l], w_xv[l]], axis=1)), (D, D), norm=False)
        mem_k_p.append(mkp.reshape(B, n_mem, X_HEADS, D // X_HEADS))
        mem_v_p.append(mvp.reshape(B, n_mem, X_HEADS, D // X_HEADS))
        wq, wo = bf(w_xq[l]), bf(w_xo[l])
        xp = _xattn_prompt(xp, g_xattn[l], wq, wo, mkp.reshape(B, n_mem, D), mvp.reshape(B, n_mem, D), T)
        xs = _xattn_sample(xs, g_xattn[l], wq, wo, cache_mem_k, cache_mem_v, l)

        wg2, wu2, wd2 = bf(w_ffn2_gate[l]), bf(w_ffn2_up[l]), bf(w_ffn2_down[l])
        xp = _ffn(xp, g_ffn2[l], wg2, wu2, wd2)
        xs = _ffn(xs, g_ffn2[l], wg2, wu2, wd2)

    y_prompt = _final_norm(xp, g_final).reshape(B, T, D)
    y_sample = unpad(_final_norm(xs, g_final))
    return (y_prompt, y_sample,
            jnp.stack(mem_k_p), jnp.stack(mem_v_p),
            jnp.stack(gla_p), jnp.stack(gla_s),
            jnp.stack(ckv_p), jnp.stack(kpe_p), jnp.stack(ckv_s), jnp.stack(kpe_s),
            jnp.stack(mk_p), jnp.stack(mv_p), jnp.stack(mk_s), jnp.stack(mv_s),
            jnp.stack(s5_p), jnp.stack(s5_s))
```

```python
import functools
import math

import jax
import jax.numpy as jnp
from jax import lax
from jax.experimental import pallas as pl
from jax.experimental.pallas import tpu as pltpu

F32 = jnp.float32
BF16 = jnp.bfloat16

RMS_EPS = 1e-6
ROPE_THETA = 10000.0
NEG_INF = -1e30
BELOW_NEG_INF = -3.0e38
PAGE_SIZE = 128
GLA_HEADS, GLA_DK, GLA_DV, GLA_GATE_RANK, GLA_TAU, GLA_CHUNK = 4, 64, 128, 16, 16.0, 64
MLA_HEADS, MLA_Q_LORA, MLA_KV_LORA, MLA_NOPE, MLA_ROPE, MLA_V = 4, 384, 256, 128, 64, 128
MOBA_HEADS, MOBA_HD, MOBA_BLOCK, MOBA_TOPK = 4, 128, 256, 3
S5_WIDTH, S5_GROUP, S5_STATE = 512, 16, 64
S5_GROUPS = S5_WIDTH // S5_GROUP
S5_N = S5_GROUPS * S5_STATE
X_HEADS = 4
LANES = 128
SUBLANES = 8
SAMPLE_PAD = SUBLANES
VMEM_LIMIT = 56 * 1024 * 1024
MLA_PAGES_PER_STEP = 32
MOBA_GROUP_PAGES = 8
MOBA_RING_GROUPS = 4
LOG2E = math.log2(math.e)


def _cparams(*sem):
    return pltpu.CompilerParams(dimension_semantics=sem, vmem_limit_bytes=VMEM_LIMIT)


def _row_tile(m, target):
    t = min(m, target)
    while m % t or t % SUBLANES:
        t -= 1
    return t


def _rms(x, g):
    return x * lax.rsqrt(jnp.mean(x * x, axis=-1, keepdims=True) + RMS_EPS) * g


def _sigmoid(x):
    return 1.0 / (1.0 + jnp.exp(-x))


def _dot(a, b):
    return jnp.dot(a, b, preferred_element_type=F32)


def _dot_nt(a, b):
    return lax.dot_general(a, b, (((1,), (1,)), ((), ())), preferred_element_type=F32)


def _dot_nt_f32(a, b):
    return lax.dot_general(a, b, (((1,), (1,)), ((), ())), precision=lax.Precision.HIGHEST,
                           preferred_element_type=F32)


def _dot_tn(a, b):
    return lax.dot_general(a, b, (((0,), (0,)), ((), ())), preferred_element_type=F32)


def _full(a):
    return pl.BlockSpec(a.shape, lambda *_: (0,) * a.ndim)


def _ffn_kernel(x_ref, g_ref, wg_ref, wu_ref, wd_ref, o_ref, h_ref):
    j = pl.program_id(1)

    @pl.when(j == 0)
    def _():
        h_ref[...] = _rms(x_ref[...], g_ref[...]).astype(BF16)
        o_ref[...] = jnp.zeros_like(o_ref)

    h = h_ref[...]
    a = _dot(h, wg_ref[...])
    z = a * _sigmoid(a) * _dot(h, wu_ref[...])
    o_ref[...] += _dot(z.astype(BF16), wd_ref[...])

    @pl.when(j == pl.num_programs(1) - 1)
    def _():
        o_ref[...] = x_ref[...] + 0.5 * o_ref[...]


def _ffn(x, g, wg, wu, wd):
    m, d = x.shape
    f = wg.shape[1]
    tm = _row_tile(m, 1024)
    tf = 256 if f % 256 == 0 else f
    return pl.pallas_call(
        _ffn_kernel,
        grid=(m // tm, f // tf),
        in_specs=[
            pl.BlockSpec((tm, d), lambda i, j: (i, 0)),
            pl.BlockSpec((1, d), lambda i, j: (0, 0)),
            pl.BlockSpec((d, tf), lambda i, j: (0, j)),
            pl.BlockSpec((d, tf), lambda i, j: (0, j)),
            pl.BlockSpec((tf, d), lambda i, j: (j, 0)),
        ],
        out_specs=pl.BlockSpec((tm, d), lambda i, j: (i, 0)),
        out_shape=jax.ShapeDtypeStruct((m, d), F32),
        scratch_shapes=[pltpu.VMEM((tm, d), BF16)],
        compiler_params=_cparams("parallel", "arbitrary"),
        name="ffn",
    )(x, g.reshape(1, d), wg, wu, wd)


def _proj_kernel(x_ref, g_ref, w_ref, *o_refs, widths, norm):
    x = x_ref[...]
    h = _rms(x, g_ref[...]) if norm else x
    y = _dot(h.astype(BF16), w_ref[...])
    off = 0
    for o_ref, wd in zip(o_refs, widths):
        o_ref[...] = y[:, off:off + wd].astype(o_ref.dtype)
        off += wd


def _proj(x, g, w, widths, norm=True):
    m, d = x.shape
    n = w.shape[1]
    assert sum(widths) == n
    tm = _row_tile(m, 512)
    return pl.pallas_call(
        functools.partial(_proj_kernel, widths=widths, norm=norm),
        grid=(m // tm,),
        in_specs=[pl.BlockSpec((tm, d), lambda i: (i, 0)), pl.BlockSpec((1, d), lambda i: (0, 0)), _full(w)],
        out_specs=[pl.BlockSpec((tm, wd), lambda i: (i, 0)) for wd in widths],
        out_shape=[jax.ShapeDtypeStruct((m, wd), F32) for wd in widths],
        compiler_params=_cparams("parallel"),
        name="proj",
    )(x, g.reshape(1, d), w)


def _outproj_kernel(a_ref, b_ref, wa_ref, wb_ref, x_ref, o_ref):
    o_ref[...] = x_ref[...] + _dot(a_ref[...], wa_ref[...]) + _dot(b_ref[...], wb_ref[...])


def _outproj(a, b, wa, wb, x):
    m, d = x.shape
    tm = _row_tile(m, 512)
    return pl.pallas_call(
        _outproj_kernel,
        grid=(m // tm,),
        in_specs=[
            pl.BlockSpec((tm, a.shape[1]), lambda i: (i, 0)),
            pl.BlockSpec((tm, b.shape[1]), lambda i: (i, 0)),
            _full(wa), _full(wb),
            pl.BlockSpec((tm, d), lambda i: (i, 0)),
        ],
        out_specs=pl.BlockSpec((tm, d), lambda i: (i, 0)),
        out_shape=jax.ShapeDtypeStruct((m, d), F32),
        compiler_params=_cparams("parallel"),
        name="outproj",
    )(a, b, wa, wb, x)


def _norm_kernel(x_ref, g_ref, o_ref):
    o_ref[...] = _rms(x_ref[...], g_ref[...])


def _final_norm(x, g):
    m, d = x.shape
    tm = _row_tile(m, 1024)
    return pl.pallas_call(
        _norm_kernel,
        grid=(m // tm,),
        in_specs=[pl.BlockSpec((tm, d), lambda i: (i, 0)), pl.BlockSpec((1, d), lambda i: (0, 0))],
        out_specs=pl.BlockSpec((tm, d), lambda i: (i, 0)),
        out_shape=jax.ShapeDtypeStruct((m, d), F32),
        compiler_params=_cparams("parallel"),
        name="final_norm",
    )(x, g.reshape(1, d))


def _xattn_core(x, g, wq, wo, mem_k, mem_v, seq_rows):
    d = x.shape[1]
    hd = d // X_HEADS
    q = (_dot(_rms(x, g).astype(BF16), wq) * (hd ** -0.5)).astype(BF16)
    nseq = x.shape[0] // seq_rows
    pairs = [(s, h) for s in range(nseq) for h in range(X_HEADS)]
    sc = [_dot_nt(q[s * seq_rows:(s + 1) * seq_rows, h * hd:(h + 1) * hd], mem_k(s, h)) for s, h in pairs]
    p = [jnp.exp(t - jnp.max(t, axis=-1, keepdims=True)) for t in sc]
    p = [t / jnp.sum(t, axis=-1, keepdims=True) for t in p]
    pv = [_dot(t.astype(BF16), mem_v(s, h)) for t, (s, h) in zip(p, pairs)]
    outs = [jnp.concatenate(pv[s * X_HEADS:(s + 1) * X_HEADS], axis=-1) for s in range(nseq)]
    o = outs[0] if nseq == 1 else jnp.concatenate(outs, axis=0)
    return x + _dot(o.astype(BF16), wo)


def _xattn_prompt_kernel(x_ref, g_ref, wq_ref, wo_ref, mk_ref, mv_ref, o_ref):
    x = x_ref[...]
    hd = x.shape[1] // X_HEADS
    mk = mk_ref[...].astype(BF16)
    mv = mv_ref[...].astype(BF16)
    o_ref[...] = _xattn_core(x, g_ref[...], wq_ref[...], wo_ref[...],
                             lambda s, h: mk[:, h * hd:(h + 1) * hd], lambda s, h: mv[:, h * hd:(h + 1) * hd],
                             x.shape[0])


def _xattn_prompt(x, g, wq, wo, mk, mv, seq):
    m, d = x.shape
    nm = mk.shape[1]
    tm = _row_tile(seq, 512)
    nt = seq // tm
    return pl.pallas_call(
        _xattn_prompt_kernel,
        grid=(m // tm,),
        in_specs=[
            pl.BlockSpec((tm, d), lambda i: (i, 0)),
            pl.BlockSpec((1, d), lambda i: (0, 0)),
            _full(wq), _full(wo),
            pl.BlockSpec((None, nm, d), lambda i: (i // nt, 0, 0)),
            pl.BlockSpec((None, nm, d), lambda i: (i // nt, 0, 0)),
        ],
        out_specs=pl.BlockSpec((tm, d), lambda i: (i, 0)),
        out_shape=jax.ShapeDtypeStruct((m, d), F32),
        compiler_params=_cparams("parallel"),
        name="xattn_prompt",
    )(x, g.reshape(1, d), wq, wo, mk, mv)


def _xattn_sample_kernel(x_ref, g_ref, wq_ref, wo_ref, mk_ref, mv_ref, o_ref, *, nseq):
    nc = x_ref.shape[1] // X_HEADS // LANES
    nm = mk_ref.shape[1] // (X_HEADS * nc)

    def head(ref, s, h):
        tiles = [ref[s, pl.ds(h + X_HEADS * c, nm, stride=X_HEADS * nc), :] for c in range(nc)]
        return jnp.concatenate(tiles, axis=1).astype(BF16)

    o_ref[...] = _xattn_core(x_ref[...], g_ref[...], wq_ref[...], wo_ref[...],
                             lambda s, h: head(mk_ref, s, h), lambda s, h: head(mv_ref, s, h),
                             x_ref.shape[0] // nseq)


def _xattn_sample(x, g, wq, wo, cache_k, cache_v, layer):
    m, d = x.shape
    depth, db, nm, nh, hd = cache_k.shape
    assert nh == X_HEADS and hd % LANES == 0
    nc = hd // LANES

    def rows_view(c):
        c = c.reshape(depth, db, nm, nh, nc, LANES)
        return jnp.transpose(c, (0, 1, 2, 4, 3, 5)).reshape(depth, db, nm * nc * nh, LANES)

    cache_k, cache_v = rows_view(cache_k), rows_view(cache_v)
    nseq = 4 if db % 4 == 0 else 1
    tm = nseq * (m // db)
    cache_spec = pl.BlockSpec((None, nseq, nm * nc * nh, LANES), lambda i: (layer, i, 0, 0))
    return pl.pallas_call(
        functools.partial(_xattn_sample_kernel, nseq=nseq),
        grid=(db // nseq,),
        in_specs=[
            pl.BlockSpec((tm, d), lambda i: (i, 0)),
            pl.BlockSpec((1, d), lambda i: (0, 0)),
            _full(wq), _full(wo), cache_spec, cache_spec,
        ],
        out_specs=pl.BlockSpec((tm, d), lambda i: (i, 0)),
        out_shape=jax.ShapeDtypeStruct((m, d), F32),
        compiler_params=_cparams("parallel"),
        name="xattn_sample",
    )(x, g.reshape(1, d), wq, wo, cache_k, cache_v)


def _split3(x):
    hi = x.astype(BF16)
    r = x - hi.astype(F32)
    mid = r.astype(BF16)
    lo = (r - mid.astype(F32)).astype(BF16)
    return hi, mid, lo


def _gla_kernel(q_ref, k_ref, v_ref, gs_ref, ra_ref, s0_ref, wg_ref, bg_ref, gn_ref, o_ref, sout_ref, st_ref,
                *, rows, chunk, nchunks, valid):
    t = pl.program_id(1)

    @pl.when(t == 0)
    def _():
        st_ref[...] = s0_ref[...]

    L = chunk
    r_io = lax.broadcasted_iota(jnp.int32, (L, L), 0)
    c_io = lax.broadcasted_iota(jnp.int32, (L, L), 1)
    tril = r_io >= c_io
    tril_bf = jnp.where(tril, 1.0, 0.0).astype(BF16)
    ones_bf = jnp.ones((L, GLA_DV), BF16)
    live = lax.broadcasted_iota(jnp.int32, (L, 1), 0) < valid

    def load(ref, r0):
        x = ref[pl.ds(r0, rows), :]
        if rows < L:
            x = jnp.concatenate([x, jnp.zeros((L - rows, x.shape[1]), x.dtype)], axis=0)
        return x

    def body(c, carry):
        r0 = pl.multiple_of(c * rows, rows)
        q = load(q_ref, r0) * (GLA_DK ** -0.5)
        k = load(k_ref, r0)
        v = load(v_ref, r0)
        ra = load(ra_ref, r0)
        z = _dot(load(gs_ref, r0).astype(BF16), wg_ref[...]) + bg_ref[...]
        lg = (jnp.minimum(z, 0.0) - jnp.log1p(jnp.exp(-jnp.abs(z)))) * (1.0 / GLA_TAU)
        if valid < L:
            lg = jnp.where(live, lg, 0.0)
            k = jnp.where(live, k, 0.0)
            v = jnp.where(live, v, 0.0)
        hi, mid, lo = _split3(lg)
        b = _dot(tril_bf, hi) + _dot(tril_bf, mid) + _dot(tril_bf, lo)
        bl = b[L - 1:L, :]
        qe = (q * jnp.exp(b)).astype(BF16)
        ke = (k * jnp.exp(-b)).astype(BF16)
        kd = (k * jnp.exp(bl - b)).astype(BF16)
        vb = v.astype(BF16)
        heads = range(GLA_HEADS)
        ks = [slice(h * GLA_DK, (h + 1) * GLA_DK) for h in heads]
        vs = [slice(h * GLA_DV, (h + 1) * GLA_DV) for h in heads]
        att = [jnp.where(tril, _dot_nt(qe[:, ks[h]], ke[:, ks[h]]), 0.0).astype(BF16) for h in heads]
        s_old = [st_ref[h] for h in heads]
        o_h = [_dot(qe[:, ks[h]], s_old[h].astype(BF16)) + _dot(att[h], vb[:, vs[h]]) for h in heads]
        dec = [_dot_tn(hi[:, ks[h]], ones_bf) + _dot_tn(mid[:, ks[h]], ones_bf) + _dot_tn(lo[:, ks[h]], ones_bf)
               for h in heads]
        upd = [_dot_tn(kd[:, ks[h]], vb[:, vs[h]]) for h in heads]
        for h in heads:
            st_ref[h] = s_old[h] * jnp.exp(dec[h]) + upd[h]
        gate = ra * _sigmoid(ra)
        o = jnp.concatenate([_rms(o_h[h], gn_ref[...]) for h in heads], axis=-1) * gate
        o_ref[pl.ds(r0, rows), :] = o[:rows].astype(o_ref.dtype)
        return carry

    if nchunks == 1:
        body(0, 0)
    else:
        lax.fori_loop(0, nchunks, body, 0)

    @pl.when(t == pl.num_programs(1) - 1)
    def _():
        sout_ref[...] = st_ref[...]


def _gla(q, k, v, gs, ra, s0, wg, bg, gn, *, seq, rows, chunk, valid):
    m = q.shape[0]
    nb = m // seq
    tt = _row_tile(seq, 512) if rows == chunk else rows
    nchunks = tt // rows
    nt = seq // tt
    row_spec = lambda w: pl.BlockSpec((tt, w), lambda b, t: (b * nt + t, 0))
    st_spec = pl.BlockSpec((None, GLA_HEADS, GLA_DK, GLA_DV), lambda b, t: (b, 0, 0, 0))
    return pl.pallas_call(
        functools.partial(_gla_kernel, rows=rows, chunk=chunk, nchunks=nchunks, valid=valid),
        grid=(nb, nt),
        in_specs=[row_spec(q.shape[1]), row_spec(k.shape[1]), row_spec(v.shape[1]), row_spec(gs.shape[1]),
                  row_spec(ra.shape[1]), st_spec, _full(wg), _full(bg), _full(gn)],
        out_specs=[row_spec(v.shape[1]), st_spec],
        out_shape=[jax.ShapeDtypeStruct((m, v.shape[1]), BF16),
                   jax.ShapeDtypeStruct((nb, GLA_HEADS, GLA_DK, GLA_DV), F32)],
        scratch_shapes=[pltpu.VMEM((GLA_HEADS, GLA_DK, GLA_DV), F32)],
        compiler_params=_cparams("parallel", "arbitrary"),
        name="gla",
    )(q, k, v, gs, ra, s0, wg, bg, gn)


def _rope(x, cos, sin_signed, group):
    w = x.shape[-1]
    ax = x.ndim - 1
    lane = lax.broadcasted_iota(jnp.int32, x.shape, ax)
    first = (lane % group) < (group // 2)
    partner = jnp.where(first, pltpu.roll(x, w - group // 2, ax), pltpu.roll(x, group // 2, ax))
    return x * cos + partner * sin_signed


def _rope_tables(pos, dim, width, reps):
    half = dim // 2
    inv = ROPE_THETA ** (-jnp.arange(half, dtype=F32) / half)
    ang = pos.astype(F32)[:, None] * inv[None, :]
    cos, sin = jnp.cos(ang), jnp.sin(ang)
    pad = jnp.zeros((pos.shape[0], width - dim), F32)
    c = jnp.concatenate([cos, cos, pad], axis=-1)
    s = jnp.concatenate([-sin, sin, pad], axis=-1)
    return jnp.tile(c, (1, reps)), jnp.tile(s, (1, reps))


MLA_QK = MLA_KV_LORA + LANES


def _mla_prep_kernel(ql_ref, kvl_ref, ks_ref, cos_ref, sin_ref, gq_ref, gkv_ref, wqn_ref, wqp_ref, wuk_ref,
                     q_ref, k_ref, ckv_ref, kpe_ref, *ct_ref, transposed):
    scale = (MLA_NOPE + MLA_ROPE) ** -0.5 * LOG2E
    hq = _rms(ql_ref[...], gq_ref[...]).astype(BF16)
    qn = _dot(hq, wqn_ref[...]).astype(BF16)
    cos, sin = cos_ref[...], sin_ref[...]
    qp = _rope(_dot(hq, wqp_ref[...]), cos, sin, MLA_ROPE)
    parts = []
    for h in range(MLA_HEADS):
        parts.append(_dot(qn[:, h * MLA_NOPE:(h + 1) * MLA_NOPE], wuk_ref[h]))
        parts.append(qp[:, h * LANES:(h + 1) * LANES])
    q = jnp.concatenate(parts, axis=-1) * scale
    ckv = _rms(kvl_ref[...], gkv_ref[...])
    ckv_ref[...] = ckv
    kpe = _rope(ks_ref[...], cos[:, :LANES], sin[:, :LANES], MLA_ROPE)
    kpe_ref[...] = kpe[:, :MLA_ROPE]
    k_ref[...] = jnp.concatenate([ckv, kpe], axis=-1).astype(BF16)
    if transposed:
        q_ref[...] = q.T.astype(BF16)
        ct_ref[0][...] = ckv.T.astype(BF16)
    else:
        q_ref[...] = q.astype(BF16)


def _mla_prep(q_lat, kv_lat, ks, cos, sin, gq, gkv, wqn, wqp, wuk, transposed):
    m = q_lat.shape[0]
    p = cos.shape[0]
    tm = _row_tile(math.gcd(m, p), 512)
    npos = p // tm
    row = lambda w: pl.BlockSpec((tm, w), lambda i: (i, 0))
    col = lambda w: pl.BlockSpec((w, tm), lambda i: (0, i))
    pos = lambda w: pl.BlockSpec((tm, w), lambda i: (i % npos, 0))
    hq, hp = MLA_HEADS * MLA_QK, MLA_HEADS * LANES
    out_specs = [col(hq) if transposed else row(hq), row(MLA_QK), row(MLA_KV_LORA), row(MLA_ROPE)]
    out_shape = [jax.ShapeDtypeStruct((hq, m) if transposed else (m, hq), BF16),
                 jax.ShapeDtypeStruct((m, MLA_QK), BF16),
                 jax.ShapeDtypeStruct((m, MLA_KV_LORA), F32), jax.ShapeDtypeStruct((m, MLA_ROPE), F32)]
    if transposed:
        out_specs.append(col(MLA_KV_LORA))
        out_shape.append(jax.ShapeDtypeStruct((MLA_KV_LORA, m), BF16))
    return pl.pallas_call(
        functools.partial(_mla_prep_kernel, transposed=transposed),
        grid=(m // tm,),
        in_specs=[row(MLA_Q_LORA), row(MLA_KV_LORA), row(LANES), pos(hp), pos(hp),
                  _full(gq), _full(gkv), _full(wqn), _full(wqp), _full(wuk)],
        out_specs=out_specs,
        out_shape=out_shape,
        compiler_params=_cparams("parallel"),
        name="mla_prep",
    )(q_lat, kv_lat, ks, cos, sin, gq, gkv, wqn, wqp, wuk)


def _stack_heads(x, width):
    return jnp.concatenate([x[:, h * width:(h + 1) * width] for h in range(MLA_HEADS)], axis=0)


def _softmax_update(s, v, m_ref, l_ref, acc_ref):
    m_prev = m_ref[...]
    m_new = jnp.maximum(m_prev, jnp.max(s, axis=-1, keepdims=True))
    alpha = jnp.exp2(m_prev - m_new)
    p = jnp.exp2(s - m_new)
    l_ref[...] = alpha * l_ref[...] + jnp.sum(p, axis=-1, keepdims=True)
    acc_ref[...] = alpha * acc_ref[...] + _dot(p.astype(BF16), v)
    m_ref[...] = m_new


def _softmax_init(m_ref, l_ref, acc_ref):
    m_ref[...] = jnp.full_like(m_ref, NEG_INF)
    l_ref[...] = jnp.zeros_like(l_ref)
    acc_ref[...] = jnp.zeros_like(acc_ref)


def _mla_out(acc_ref, l_ref, wuv_ref, rows):
    o_lat = (acc_ref[...] / l_ref[...]).astype(BF16)
    return jnp.concatenate([_dot(o_lat[h * rows:(h + 1) * rows], wuv_ref[h]) for h in range(MLA_HEADS)], axis=-1)


def _mla_prompt_kernel(qi_ref, ki_ref, qt_ref, k_ref, ct_ref, wuvt_ref, o_ref, qs_ref, m_ref, l_ref, acc_ref,
                       *, tq, tk):
    j = pl.program_id(1)
    qi, ki = qi_ref[j], ki_ref[j]
    last = ((qi + 1) * tq - 1) // tk

    @pl.when(ki == 0)
    def _():
        qt = qt_ref[...]
        qs_ref[...] = jnp.concatenate([qt[h * MLA_QK:(h + 1) * MLA_QK] for h in range(MLA_HEADS)], axis=1)
        _softmax_init(m_ref, l_ref, acc_ref)

    def step(masked):
        heads = range(MLA_HEADS)
        cs = [slice(h * tq, (h + 1) * tq) for h in heads]
        k, ct = k_ref[...], ct_ref[...]
        st = [_dot(k, qs_ref[:, cs[h]]) for h in heads]
        if masked:
            keep = (ki * tk + lax.broadcasted_iota(jnp.int32, (tk, tq), 0)
                    <= qi * tq + lax.broadcasted_iota(jnp.int32, (tk, tq), 1))
            st = [jnp.where(keep, st[h], NEG_INF) for h in heads]
        m_prev = [m_ref[:, cs[h]] for h in heads]
        m_new = [jnp.maximum(m_prev[h], jnp.max(st[h], axis=0, keepdims=True)) for h in heads]
        alpha = [jnp.exp2(m_prev[h] - m_new[h]) for h in heads]
        p = [jnp.exp2(st[h] - m_new[h]) for h in heads]
        pv = [_dot(ct, p[h].astype(BF16)) for h in heads]
        for h in heads:
            m_ref[:, cs[h]] = m_new[h]
            l_ref[:, cs[h]] = alpha[h] * l_ref[:, cs[h]] + jnp.sum(p[h], axis=0, keepdims=True)
            acc_ref[:, cs[h]] = alpha[h] * acc_ref[:, cs[h]] + pv[h]

    crosses = (ki + 1) * tk - 1 > qi * tq

    @pl.when(crosses)
    def _():
        step(True)

    @pl.when(jnp.logical_not(crosses))
    def _():
        step(False)

    @pl.when(ki == last)
    def _():
        o_lat = (acc_ref[...] / l_ref[...]).astype(BF16)
        ot = jnp.concatenate([_dot(wuvt_ref[h], o_lat[:, h * tq:(h + 1) * tq]) for h in range(MLA_HEADS)], axis=0)
        o_ref[...] = ot.T.astype(o_ref.dtype)


def _mla_prompt(qt, k, ct, wuvt, seq):
    m = k.shape[0]
    nb = m // seq
    tq = tk = _row_tile(seq, 512)
    nq, nk = seq // tq, seq // tk
    pairs = [(qi, ki) for qi in range(nq) for ki in range(((qi + 1) * tq - 1) // tk + 1)]
    qi_tab = jnp.asarray([p[0] for p in pairs], jnp.int32)
    ki_tab = jnp.asarray([p[1] for p in pairs], jnp.int32)
    cols = MLA_HEADS * tq
    grid_spec = pltpu.PrefetchScalarGridSpec(
        num_scalar_prefetch=2,
        grid=(nb, len(pairs)),
        in_specs=[pl.BlockSpec((qt.shape[0], tq), lambda b, j, qi, ki: (0, b * nq + qi[j])),
                  pl.BlockSpec((tk, k.shape[1]), lambda b, j, qi, ki: (b * nk + ki[j], 0)),
                  pl.BlockSpec((ct.shape[0], tk), lambda b, j, qi, ki: (0, b * nk + ki[j])),
                  pl.BlockSpec(wuvt.shape, lambda b, j, qi, ki: (0, 0, 0))],
        out_specs=pl.BlockSpec((tq, MLA_HEADS * MLA_V), lambda b, j, qi, ki: (b * nq + qi[j], 0)),
        scratch_shapes=[pltpu.VMEM((MLA_QK, cols), BF16), pltpu.VMEM((1, cols), F32), pltpu.VMEM((1, cols), F32),
                        pltpu.VMEM((MLA_KV_LORA, cols), F32)],
    )
    return pl.pallas_call(
        functools.partial(_mla_prompt_kernel, tq=tq, tk=tk),
        grid_spec=grid_spec,
        out_shape=jax.ShapeDtypeStruct((m, MLA_HEADS * MLA_V), BF16),
        compiler_params=_cparams("parallel", "arbitrary"),
        name="mla_prompt",
    )(qi_tab, ki_tab, qt, k, ct, wuvt)


def _mla_sample_kernel(pt_ref, q_ref, kn_ref, wuv_ref, *rest, npg, valid):
    ckv_pages, kpet_pages = rest[:npg], rest[npg:2 * npg]
    o_ref, qs_ref, m_ref, l_ref, acc_ref = rest[2 * npg:]
    g = pl.program_id(1)
    sp = q_ref.shape[0]

    @pl.when(g == 0)
    def _():
        qs_ref[...] = _stack_heads(q_ref[...], MLA_QK)
        _softmax_init(m_ref, l_ref, acc_ref)

    qs = qs_ref[...]
    qc, qp = qs[:, :MLA_KV_LORA], qs[:, MLA_KV_LORA:MLA_KV_LORA + MLA_ROPE]
    ngrp = math.gcd(npg, 4)
    per = npg // ngrp
    grp = range(ngrp)
    ckv = [jnp.concatenate([r[...].astype(BF16) for r in ckv_pages[g * per:(g + 1) * per]], axis=0) for g in grp]
    kpet = [jnp.concatenate([r[...].astype(BF16) for r in kpet_pages[g * per:(g + 1) * per]], axis=1) for g in grp]
    s = [_dot_nt(qc, ckv[g]) + _dot(qp, kpet[g]) for g in grp]
    mg = [jnp.max(s[g], axis=-1, keepdims=True) for g in grp]
    p = [jnp.exp2(s[g] - mg[g]) for g in grp]
    lg = [jnp.sum(p[g], axis=-1, keepdims=True) for g in grp]
    pv = [_dot(p[g].astype(BF16), ckv[g]) for g in grp]
    m_prev = m_ref[...]
    m_new = functools.reduce(jnp.maximum, mg, m_prev)
    alpha = jnp.exp2(m_prev - m_new)
    w = [jnp.exp2(mg[g] - m_new) for g in grp]
    l_ref[...] = alpha * l_ref[...] + sum(w[g] * lg[g] for g in grp)
    acc_ref[...] = alpha * acc_ref[...] + sum(w[g] * pv[g] for g in grp)
    m_ref[...] = m_new

    @pl.when(g == pl.num_programs(1) - 1)
    def _():
        kn = kn_ref[...]
        ckvn = kn[:, :MLA_KV_LORA]
        s = _dot_nt(qs, kn)
        rows = MLA_HEADS * sp
        qpos = lax.broadcasted_iota(jnp.int32, (rows, sp), 0) % sp
        kpos = lax.broadcasted_iota(jnp.int32, (rows, sp), 1)
        s = jnp.where(jnp.logical_and(kpos <= qpos, kpos < valid), s, NEG_INF)
        _softmax_update(s, ckvn, m_ref, l_ref, acc_ref)
        o_ref[...] = _mla_out(acc_ref, l_ref, wuv_ref, sp).astype(o_ref.dtype)


def _mla_sample(q, kn, wuv, cache_ckv, cache_kpet, layer, page_table, valid):
    m = q.shape[0]
    db, n_pages = page_table.shape
    sp = m // db
    npg = math.gcd(n_pages, MLA_PAGES_PER_STEP)
    rows = MLA_HEADS * sp
    qmap = lambda b, g, pt: (b, 0)

    def page_spec(shape, j):
        return pl.BlockSpec((None, None) + shape, lambda b, g, pt: (layer, pt[b * n_pages + g * npg + j], 0, 0))

    grid_spec = pltpu.PrefetchScalarGridSpec(
        num_scalar_prefetch=1,
        grid=(db, n_pages // npg),
        in_specs=[pl.BlockSpec((sp, q.shape[1]), qmap), pl.BlockSpec((sp, kn.shape[1]), qmap),
                  pl.BlockSpec(wuv.shape, lambda b, g, pt: (0, 0, 0))]
                 + [page_spec((PAGE_SIZE, MLA_KV_LORA), j) for j in range(npg)]
                 + [page_spec((MLA_ROPE, PAGE_SIZE), j) for j in range(npg)],
        out_specs=pl.BlockSpec((sp, MLA_HEADS * MLA_V), qmap),
        scratch_shapes=[pltpu.VMEM((rows, MLA_QK), BF16), pltpu.VMEM((rows, 1), F32), pltpu.VMEM((rows, 1), F32),
                        pltpu.VMEM((rows, MLA_KV_LORA), F32)],
    )
    return pl.pallas_call(
        functools.partial(_mla_sample_kernel, npg=npg, valid=valid),
        grid_spec=grid_spec,
        out_shape=jax.ShapeDtypeStruct((m, MLA_HEADS * MLA_V), BF16),
        compiler_params=_cparams("parallel", "arbitrary"),
        name="mla_sample",
    )(page_table.reshape(-1), q, kn, wuv, *([cache_ckv] * npg), *([cache_kpet] * npg))


def _moba_prep_kernel(q_ref, k_ref, v_ref, cos_ref, sin_ref, qr_ref, kr_ref, *rest, nblk):
    cos, sin = cos_ref[...], sin_ref[...]
    qr_ref[...] = _rope(q_ref[...], cos, sin, MOBA_HD)
    kr = _rope(k_ref[...], cos, sin, MOBA_HD)
    kr_ref[...] = kr
    if nblk:
        kb_ref, vt_ref, km_ref = rest
        kb_ref[...] = kr.astype(BF16)
        vt_ref[...] = v_ref[...].T.astype(BF16)
        for j in range(nblk):
            km_ref[j] = jnp.mean(kr[j * MOBA_BLOCK:(j + 1) * MOBA_BLOCK], axis=0, keepdims=True)


def _moba_prep(q, k, v, cos, sin, prompt):
    m, w = q.shape
    p = cos.shape[0]
    tm = _row_tile(math.gcd(m, p), 512)
    npos = p // tm
    if prompt:
        assert tm % MOBA_BLOCK == 0
    nblk = tm // MOBA_BLOCK if prompt else 0
    row = pl.BlockSpec((tm, w), lambda i: (i, 0))
    pos = pl.BlockSpec((tm, w), lambda i: (i % npos, 0))
    out_specs = [row, row]
    out_shape = [jax.ShapeDtypeStruct((m, w), F32), jax.ShapeDtypeStruct((m, w), F32)]
    if prompt:
        out_specs += [row, pl.BlockSpec((w, tm), lambda i: (0, i)), pl.BlockSpec((nblk, 1, w), lambda i: (i, 0, 0))]
        out_shape += [jax.ShapeDtypeStruct((m, w), BF16), jax.ShapeDtypeStruct((w, m), BF16),
                      jax.ShapeDtypeStruct((m // MOBA_BLOCK, 1, w), F32)]
    return pl.pallas_call(
        functools.partial(_moba_prep_kernel, nblk=nblk),
        grid=(m // tm,),
        in_specs=[row, row, row, pos, pos],
        out_specs=out_specs,
        out_shape=out_shape,
        compiler_params=_cparams("parallel"),
        name="moba_prep",
    )(q, k, v, cos, sin)


def _top_rows(gate, idx, count, limit, axis):
    sel = jnp.zeros(gate.shape, F32)
    for _ in range(count):
        mx = jnp.max(gate, axis=axis, keepdims=True)
        first = jnp.min(jnp.where(gate == mx, idx, limit), axis=axis, keepdims=True)
        pick = idx == first
        sel = jnp.where(pick, 1.0, sel)
        gate = jnp.where(pick, BELOW_NEG_INF, gate)
    return sel


def _moba_prompt_kernel(q_ref, k_ref, vt_ref, km_ref, o_ref, qt_ref, sel_ref, m_ref, l_ref, acc_ref, *, nb, topn):
    qi = pl.program_id(1)
    blk, hd = MOBA_BLOCK, MOBA_HD
    b_idx = lax.broadcasted_iota(jnp.int32, (nb, blk), 0)
    past = b_idx < qi
    causal = lax.broadcasted_iota(jnp.int32, (blk, blk), 0) <= lax.broadcasted_iota(jnp.int32, (blk, blk), 1)
    r0 = pl.multiple_of(qi * blk, blk)

    heads = range(MOBA_HEADS)
    hsl = [slice(h * hd, (h + 1) * hd) for h in heads]
    q = [q_ref[:, hsl[h]] for h in heads]
    qt = [(q[h] * (hd ** -0.5 * LOG2E)).T.astype(BF16) for h in heads]
    st = [jnp.where(causal, _dot(k_ref[pl.ds(r0, blk), hsl[h]], qt[h]), NEG_INF) for h in heads]
    m0 = [jnp.max(st[h], axis=0, keepdims=True) for h in heads]
    p0 = [jnp.exp2(st[h] - m0[h]) for h in heads]
    pv0 = [_dot(vt_ref[hsl[h], pl.ds(r0, blk)], p0[h].astype(BF16)) for h in heads]
    gate = [jnp.where(past, _dot_nt_f32(km_ref[:, hsl[h]], q[h]), NEG_INF) for h in heads]
    for h in heads:
        qt_ref[h] = qt[h]
        m_ref[h] = m0[h]
        l_ref[h] = jnp.sum(p0[h], axis=0, keepdims=True)
        acc_ref[h] = pv0[h]
        sel_ref[h] = jnp.where(past, _top_rows(gate[h], b_idx, topn, nb, 0), 0.0)

    def body(n, carry):
        rn = pl.multiple_of(n * blk, blk)
        s = [_dot(k_ref[pl.ds(rn, blk), hsl[h]], qt_ref[h]) for h in heads]
        s = [jnp.where(sel_ref[h, pl.ds(n, 1), :] > 0.0, s[h], NEG_INF) for h in heads]
        m_old = [m_ref[h] for h in heads]
        m_new = [jnp.maximum(m_old[h], jnp.max(s[h], axis=0, keepdims=True)) for h in heads]
        alpha = [jnp.exp2(m_old[h] - m_new[h]) for h in heads]
        p = [jnp.exp2(s[h] - m_new[h]) for h in heads]
        pv = [_dot(vt_ref[hsl[h], pl.ds(rn, blk)], p[h].astype(BF16)) for h in heads]
        for h in heads:
            m_ref[h] = m_new[h]
            l_ref[h] = alpha[h] * l_ref[h] + jnp.sum(p[h], axis=0, keepdims=True)
            acc_ref[h] = alpha[h] * acc_ref[h] + pv[h]
        return carry

    lax.fori_loop(0, qi, body, 0)
    o_ref[...] = jnp.concatenate([(acc_ref[h] / l_ref[h]).T for h in range(MOBA_HEADS)], axis=1).astype(o_ref.dtype)


def _moba_prompt(qr, kb, vt, km, seq):
    m, hw = qr.shape
    nbatch = m // seq
    assert seq % MOBA_BLOCK == 0
    nb = seq // MOBA_BLOCK
    topn = min(MOBA_TOPK, nb)
    qmap = lambda b, qi: (b * nb + qi, 0)
    once = pl.Buffered(1)
    return pl.pallas_call(
        functools.partial(_moba_prompt_kernel, nb=nb, topn=topn),
        grid=(nbatch, nb),
        in_specs=[pl.BlockSpec((MOBA_BLOCK, hw), qmap),
                  pl.BlockSpec((seq, hw), lambda b, qi: (b, 0), pipeline_mode=once),
                  pl.BlockSpec((hw, seq), lambda b, qi: (0, b), pipeline_mode=once),
                  pl.BlockSpec((None, nb, hw), lambda b, qi: (b, 0, 0))],
        out_specs=pl.BlockSpec((MOBA_BLOCK, hw), qmap),
        out_shape=jax.ShapeDtypeStruct((m, hw), BF16),
        scratch_shapes=[pltpu.VMEM((MOBA_HEADS, MOBA_HD, MOBA_BLOCK), BF16),
                        pltpu.VMEM((MOBA_HEADS, nb, MOBA_BLOCK), F32),
                        pltpu.VMEM((MOBA_HEADS, 1, MOBA_BLOCK), F32),
                        pltpu.VMEM((MOBA_HEADS, 1, MOBA_BLOCK), F32),
                        pltpu.VMEM((MOBA_HEADS, MOBA_HD, MOBA_BLOCK), F32)],
        compiler_params=_cparams("parallel", "arbitrary"),
        name="moba_prompt",
    )(qr, kb, vt, km)


def _moba_sample_kernel(pt_ref, q_ref, kn_ref, vn_ref, ck_hbm, cv_hbm, o_ref,
                        buf, sem, qrow_ref, s_ref, p_ref, km_ref, acc_ref, pown_ref, l_ref,
                        *, layer, n_pages, nb, topn, valid):
    b = pl.program_id(0)
    nseq = pl.num_programs(0)
    gp, ring = MOBA_GROUP_PAGES, MOBA_RING_GROUPS
    ngk = n_pages // gp
    total = 2 * ngk
    look = ring - 1
    sp = q_ref.shape[0]
    hw = MOBA_HEADS * MOBA_HD
    ppb = MOBA_BLOCK // PAGE_SIZE
    nbs = gp // ppb
    scale = MOBA_HD ** -0.5

    def page_copy(src, page, slot, j):
        return pltpu.make_async_copy(src.at[layer, page], buf.at[slot * gp + j], sem.at[slot])

    def start_group(seq, gi):
        src = ck_hbm if gi < ngk else cv_hbm
        for j in range(gp):
            page_copy(src, pt_ref[seq * n_pages + (gi % ngk) * gp + j], gi % ring, j).start()

    def wait_group(gi):
        for j in range(gp):
            page_copy(ck_hbm, 0, gi % ring, j).wait()

    def query_rows():
        qt = jnp.concatenate([q_ref[...]] * (LANES // sp), axis=0)
        rh = lax.broadcasted_iota(jnp.int32, (LANES, hw), 0) // sp
        lh = lax.broadcasted_iota(jnp.int32, (LANES, hw), 1) // MOBA_HD
        return jnp.where(rh == lh, qt, 0.0)

    def page_heads(slot, j):
        return [buf[slot * gp + j, pl.ds(h, PAGE_SIZE, stride=MOBA_HEADS), :] for h in range(MOBA_HEADS)]

    def key_group(gk):
        slot = gk % ring
        sums, kall = [], []
        for j in range(gp):
            xh = page_heads(slot, j)
            kall.append(jnp.concatenate([t.astype(BF16) for t in xh], axis=1))
            sums.append(jnp.concatenate([jnp.sum(t, axis=0, keepdims=True) for t in xh], axis=1))
        keys = gp * PAGE_SIZE
        s_ref[gk * keys:(gk + 1) * keys, :] = _dot_nt(jnp.concatenate(kall, axis=0), qrow_ref[...])
        for jb in range(nbs):
            blk = gk * nbs + jb
            km_ref[blk:blk + 1, :] = sum(sums[jb * ppb:(jb + 1) * ppb]) * (1.0 / MOBA_BLOCK)

    def value_group(gv):
        slot = (ngk + gv) % ring
        keys = gp * PAGE_SIZE
        vall = jnp.concatenate(
            [jnp.concatenate([t.astype(BF16) for t in page_heads(slot, j)], axis=1) for j in range(gp)], axis=0)
        acc_ref[...] += _dot_tn(p_ref[gv * keys:(gv + 1) * keys, :], vall)

    def select_and_weigh():
        b_idx = lax.broadcasted_iota(jnp.int32, (nb, LANES), 0)
        sel = _top_rows(_dot_nt_f32(km_ref[...], query_rows()), b_idx, topn, nb, 0)
        s_own = _dot_nt(kn_ref[...].astype(BF16), qrow_ref[...])
        kt = lax.broadcasted_iota(jnp.int32, (sp, LANES), 0)
        qt = lax.broadcasted_iota(jnp.int32, (sp, LANES), 1) % sp
        s_own = jnp.where(jnp.logical_and(kt <= qt, kt < valid), s_own, NEG_INF)

        def masked(n):
            return jnp.where(sel[n:n + 1, :] > 0.0, s_ref[n * MOBA_BLOCK:(n + 1) * MOBA_BLOCK, :], NEG_INF)

        mx = jnp.max(s_own, axis=0, keepdims=True)
        for n in range(nb):
            mx = jnp.maximum(mx, jnp.max(masked(n), axis=0, keepdims=True))
        p_own = jnp.exp(s_own - mx)
        l = jnp.sum(p_own, axis=0, keepdims=True)
        for n in range(nb):
            pn = jnp.exp(masked(n) - mx)
            l = l + jnp.sum(pn, axis=0, keepdims=True)
            p_ref[n * MOBA_BLOCK:(n + 1) * MOBA_BLOCK, :] = pn.astype(BF16)
        pown_ref[...] = jnp.concatenate([p_own, jnp.zeros((LANES - sp, LANES), F32)], axis=0).astype(BF16)
        l_ref[...] = jnp.broadcast_to(l, l_ref.shape)

    def finish():
        vn = jnp.concatenate([vn_ref[...], jnp.zeros((LANES - sp, hw), F32)], axis=0).astype(BF16)
        acc = acc_ref[...] + _dot_tn(pown_ref[...], vn)
        o = acc / l_ref[...].T[:, :1]
        o_ref[...] = jnp.concatenate(
            [o[h * sp:(h + 1) * sp, h * MOBA_HD:(h + 1) * MOBA_HD] for h in range(MOBA_HEADS)], axis=1
        ).astype(o_ref.dtype)

    @pl.when(b == 0)
    def _():
        for gi in range(look):
            start_group(b, gi)

    qrow_ref[...] = (query_rows() * scale).astype(BF16)
    acc_ref[...] = jnp.zeros_like(acc_ref)

    for gi in range(total):
        ahead = gi + look
        if ahead < total:
            start_group(b, ahead)
        else:
            @pl.when(b + 1 < nseq)
            def _():
                start_group(b + 1, ahead - total)
        wait_group(gi)
        if gi < ngk:
            key_group(gi)
            if gi == ngk - 1:
                select_and_weigh()
        else:
            value_group(gi - ngk)
    finish()


def _moba_sample(qr, kr, v, cache_k, cache_v, layer, page_table, valid):
    m, hw = qr.shape
    cache_k = cache_k.reshape(cache_k.shape[:2] + (PAGE_SIZE * MOBA_HEADS, MOBA_HD))
    cache_v = cache_v.reshape(cache_v.shape[:2] + (PAGE_SIZE * MOBA_HEADS, MOBA_HD))
    db, n_pages = page_table.shape
    sp = m // db
    past = n_pages * PAGE_SIZE
    assert past % MOBA_BLOCK == 0
    nb = past // MOBA_BLOCK
    topn = min(MOBA_TOPK, nb)
    ppb = MOBA_BLOCK // PAGE_SIZE
    gp, ring = MOBA_GROUP_PAGES, MOBA_RING_GROUPS
    assert n_pages % gp == 0 and gp % ppb == 0 and (2 * n_pages // gp) % ring == 0 and nb >= 1
    qmap = lambda b, pt: (b, 0)
    grid_spec = pltpu.PrefetchScalarGridSpec(
        num_scalar_prefetch=1,
        grid=(db,),
        in_specs=[pl.BlockSpec((sp, hw), qmap)] * 3 + [pl.BlockSpec(memory_space=pl.ANY)] * 2,
        out_specs=pl.BlockSpec((sp, hw), qmap),
        scratch_shapes=[pltpu.VMEM((ring * gp, PAGE_SIZE * MOBA_HEADS, MOBA_HD), F32), pltpu.SemaphoreType.DMA((ring,)),
                        pltpu.VMEM((LANES, hw), BF16), pltpu.VMEM((past, LANES), F32), pltpu.VMEM((past, LANES), BF16),
                        pltpu.VMEM((nb, hw), F32), pltpu.VMEM((LANES, hw), F32), pltpu.VMEM((LANES, LANES), BF16),
                        pltpu.VMEM((SUBLANES, LANES), F32)],
    )
    return pl.pallas_call(
        functools.partial(_moba_sample_kernel, layer=layer, n_pages=n_pages, nb=nb, topn=topn, valid=valid),
        grid_spec=grid_spec,
        out_shape=jax.ShapeDtypeStruct((m, hw), BF16),
        compiler_params=_cparams("arbitrary"),
        name="moba_sample",
    )(page_table.reshape(-1), qr, kr, v, cache_k, cache_v)


def _s5_disc_kernel(lr_ref, li_ref, dt_ref, bre_ref, bim_ref, pwr_ref, pwi_ref, bbr_ref, bbi_ref):
    lr, li, dt = lr_ref[...], li_ref[...], dt_ref[...]
    k = (lax.broadcasted_iota(jnp.int32, pwr_ref.shape, 0) + 1).astype(F32)
    mag = jnp.exp(k * (lr * dt))
    ang = k * (li * dt)
    pwr = mag * jnp.cos(ang)
    pwi = mag * jnp.sin(ang)
    pwr_ref[...] = pwr
    pwi_ref[...] = pwi
    nr, ni = pwr[0:1] - 1.0, pwi[0:1]
    den = lr * lr + li * li
    f_re = (nr * lr + ni * li) / den
    f_im = (ni * lr - nr * li) / den
    bre, bim = bre_ref[...], bim_ref[...]
    bbr_ref[...] = f_re * bre - f_im * bim
    bbi_ref[...] = f_re * bim + f_im * bre


def _s5_discretize(lam_re, lam_im, log_dt, b_re, b_im):
    n = S5_N
    flat = lambda a: a.astype(F32).reshape(1, n)
    dt = jnp.exp(jnp.repeat(log_dt.astype(F32), S5_STATE)).reshape(1, n)
    bt = lambda b: jnp.transpose(b.astype(F32), (2, 0, 1)).reshape(S5_GROUP, n)
    args = (flat(lam_re), flat(lam_im), dt, bt(b_re), bt(b_im))
    return pl.pallas_call(
        _s5_disc_kernel,
        in_specs=[_full(a) for a in args],
        out_shape=[jax.ShapeDtypeStruct((SUBLANES, n), F32)] * 2 + [jax.ShapeDtypeStruct((S5_GROUP, n), F32)] * 2,
        name="s5_discretize",
    )(*args)


def _s5_block_diag(pieces):
    r, g, c = pieces.shape
    eye = jnp.eye(g, dtype=pieces.dtype)
    return jnp.einsum('rgc,gh->grhc', pieces, eye).reshape(g * r, g * c)


def _s5_kernel(u_ref, bb_ref, pwr_ref, pwi_ref, cc_ref, d_ref, wg_ref, bg_ref, *rest, chain):
    n = S5_N
    u = u_ref[...]
    tt = u.shape[0]
    ngrp = tt // SUBLANES
    bu = _dot(u.astype(BF16), bb_ref[...])
    xr = bu[:, :n].reshape(ngrp, SUBLANES, n)
    xi = bu[:, n:].reshape(ngrp, SUBLANES, n)
    row = lax.broadcasted_iota(jnp.int32, (ngrp, SUBLANES, n), 1)
    for d in (1, 2, 4):
        pr, pi = pwr_ref[d - 1:d, :], pwi_ref[d - 1:d, :]
        sr, si = pltpu.roll(xr, d, 1), pltpu.roll(xi, d, 1)
        keep = row >= d
        xr, xi = (xr + jnp.where(keep, pr * sr - pi * si, 0.0),
                  xi + jnp.where(keep, pr * si + pi * sr, 0.0))
    pw_r, pw_i = pwr_ref[...], pwi_ref[...]

    if chain:
        o_ref, sr_ref, si_ref, xr_s, xi_s, cr_s, ci_s = rest
        t = pl.program_id(1)

        @pl.when(t == 0)
        def _():
            cr_s[...] = jnp.zeros_like(cr_s)
            ci_s[...] = jnp.zeros_like(ci_s)

        xr_s[...] = xr
        xi_s[...] = xi

        def body(i, carry):
            cr, ci = carry
            a = xr_s[i] + pw_r * cr - pw_i * ci
            b = xi_s[i] + pw_r * ci + pw_i * cr
            xr_s[i] = a
            xi_s[i] = b
            return a[SUBLANES - 1:SUBLANES], b[SUBLANES - 1:SUBLANES]

        cr, ci = lax.fori_loop(0, ngrp, body, (cr_s[...], ci_s[...]))
        cr_s[...] = cr
        ci_s[...] = ci
        xr, xi = xr_s[...], xi_s[...]

        @pl.when(t == pl.num_programs(1) - 1)
        def _():
            sr_ref[...] = cr
            si_ref[...] = ci
    else:
        s0r_ref, s0i_ref, o_ref, xr_ref, xi_ref = rest
        s0r = s0r_ref[...].reshape(ngrp, SUBLANES, n)
        s0i = s0i_ref[...].reshape(ngrp, SUBLANES, n)
        xr, xi = xr + pw_r * s0r - pw_i * s0i, xi + pw_r * s0i + pw_i * s0r
        xr_ref[...] = xr.reshape(tt, n)
        xi_ref[...] = xi.reshape(tt, n)

    xcat = jnp.concatenate([xr.reshape(tt, n), xi.reshape(tt, n)], axis=1).astype(BF16)
    y = _dot(xcat, cc_ref[...]) + d_ref[...] * u
    y = y * (0.5 * (1.0 + jnp.tanh(math.sqrt(2.0 / math.pi) * (y + 0.044715 * (y * y * y)))))
    z = _dot(y.astype(BF16), wg_ref[...]) + bg_ref[...]
    o_ref[...] = (y * _sigmoid(z)).astype(o_ref.dtype)


def _s5_prompt(u, bb, pwr, pwi, cc, d, wg, bg, seq):
    m, w = u.shape
    nbatch = m // seq
    tt = _row_tile(seq, 256)
    nt = seq // tt
    n = S5_N
    row = pl.BlockSpec((tt, w), lambda b, t: (b * nt + t, 0))
    st = pl.BlockSpec((None, 1, n), lambda b, t: (b, 0, 0))
    return pl.pallas_call(
        functools.partial(_s5_kernel, chain=True),
        grid=(nbatch, nt),
        in_specs=[row, _full(bb), _full(pwr), _full(pwi), _full(cc), _full(d), _full(wg), _full(bg)],
        out_specs=[row, st, st],
        out_shape=[jax.ShapeDtypeStruct((m, w), BF16), jax.ShapeDtypeStruct((nbatch, 1, n), F32),
                   jax.ShapeDtypeStruct((nbatch, 1, n), F32)],
        scratch_shapes=[pltpu.VMEM((tt // SUBLANES, SUBLANES, n), F32), pltpu.VMEM((tt // SUBLANES, SUBLANES, n), F32),
                        pltpu.VMEM((1, n), F32), pltpu.VMEM((1, n), F32)],
        compiler_params=_cparams("parallel", "arbitrary"),
        name="s5_prompt",
    )(u, bb, pwr, pwi, cc, d, wg, bg)


def _s5_sample(u, bb, pwr, pwi, cc, d, wg, bg, s0r, s0i):
    m, w = u.shape
    n = S5_N
    tt = _row_tile(m, 256)
    row = pl.BlockSpec((tt, w), lambda i: (i, 0))
    st = pl.BlockSpec((tt, n), lambda i: (i, 0))
    return pl.pallas_call(
        functools.partial(_s5_kernel, chain=False),
        grid=(m // tt,),
        in_specs=[row, _full(bb), _full(pwr), _full(pwi), _full(cc), _full(d), _full(wg), _full(bg), st, st],
        out_specs=[row, st, st],
        out_shape=[jax.ShapeDtypeStruct((m, w), BF16), jax.ShapeDtypeStruct((m, n), F32),
                   jax.ShapeDtypeStruct((m, n), F32)],
        compiler_params=_cparams("parallel"),
        name="s5_sample",
    )(u, bb, pwr, pwi, cc, d, wg, bg, s0r, s0i)


def kernel(x_prompt, x_sample, cache_mem_k, cache_mem_v, state_gla, cache_mla_ckv, cache_mla_kpe, cache_moba_k, cache_moba_v, state_s5, page_table, mem_prompt, g_ffn1, w_ffn1_gate, w_ffn1_up, w_ffn1_down, g_mix, w_in_a, w_gla_gate, b_gla_gate, g_gla_norm, g_mla_q, w_mla_qb, g_mla_kv, w_mla_kvb, w_out_a, w_in_b, s5_lam_re, s5_lam_im, s5_log_dt, s5_b_re, s5_b_im, s5_c_re, s5_c_im, s5_d, w_s5_glu, b_s5_glu, w_out_b, g_xattn, w_xq, w_xk, w_xv, w_xo, g_ffn2, w_ffn2_gate, w_ffn2_up, w_ffn2_down, g_final):
    B, T, D = x_prompt.shape
    DB, S, _ = x_sample.shape
    depth = g_ffn1.shape[0]
    n_pages = page_table.shape[1]
    past = n_pages * PAGE_SIZE
    SP = SAMPLE_PAD
    assert S <= SP
    n_mem = mem_prompt.shape[1]
    bf = lambda a: a.astype(BF16)

    xp = x_prompt.reshape(B * T, D)
    xs = jnp.pad(x_sample, ((0, 0), (0, SP - S), (0, 0))).reshape(DB * SP, D)
    pos_p = jnp.arange(T, dtype=jnp.int32)
    pos_s = jnp.tile(past + jnp.arange(SP, dtype=jnp.int32), DB)
    mla_tab_p = _rope_tables(pos_p, MLA_ROPE, LANES, MLA_HEADS)
    mla_tab_s = _rope_tables(pos_s, MLA_ROPE, LANES, MLA_HEADS)
    moba_tab_p = _rope_tables(pos_p, MOBA_HD, MOBA_HD, MOBA_HEADS)
    moba_tab_s = _rope_tables(pos_s, MOBA_HD, MOBA_HD, MOBA_HEADS)
    mem2d = mem_prompt.reshape(B * n_mem, D)
    cache_kpet = jnp.swapaxes(cache_mla_kpe, 2, 3)
    ones_d = jnp.ones((D,), F32)
    gla_chunk = math.gcd(T, GLA_CHUNK)

    def unpad(a):
        return a.reshape((DB, SP) + a.shape[1:])[:, :S]

    mem_k_p, mem_v_p = [], []
    gla_p, gla_s = [], []
    ckv_p, kpe_p, ckv_s, kpe_s = [], [], [], []
    mk_p, mv_p, mk_s, mv_s = [], [], [], []
    s5_p, s5_s = [], []

    for l in range(depth):
        i = l // 2
        wg1, wu1, wd1 = bf(w_ffn1_gate[l]), bf(w_ffn1_up[l]), bf(w_ffn1_down[l])
        xp = _ffn(xp, g_ffn1[l], wg1, wu1, wd1)
        xs = _ffn(xs, g_ffn1[l], wg1, wu1, wd1)

        if l % 2 == 0:
            qa_w, ka_w, va_w, ga_w, ra_w, ql_w, kvl_w, kr_w = jnp.split(
                w_in_a[i], [256, 512, 1024, 1040, 1552, 1936, 2192], axis=1)
            padc = lambda w: jnp.pad(w, ((0, 0), (0, LANES - w.shape[1])))
            w_in = bf(jnp.concatenate([va_w, ra_w, qa_w, ka_w, kvl_w, ql_w, padc(kr_w), padc(ga_w)], axis=1))
            widths = (512, 512, 256, 256, 256, 384, LANES, LANES)
            wgate = bf(jnp.pad(w_gla_gate[i], ((0, LANES - GLA_GATE_RANK), (0, 0))))
            bgate = b_gla_gate[i].reshape(1, -1)
            gnorm = g_gla_norm[i].reshape(1, -1)
            wqb = w_mla_qb[i].reshape(MLA_Q_LORA, MLA_HEADS, MLA_NOPE + MLA_ROPE)
            wqn = bf(wqb[:, :, :MLA_NOPE].reshape(MLA_Q_LORA, MLA_HEADS * MLA_NOPE))
            wqp = bf(jnp.pad(wqb[:, :, MLA_NOPE:], ((0, 0), (0, 0), (0, LANES - MLA_ROPE))).reshape(MLA_Q_LORA, MLA_HEADS * LANES))
            wkvb = w_mla_kvb[i].reshape(MLA_KV_LORA, MLA_HEADS, MLA_NOPE + MLA_V)
            wuk = bf(jnp.transpose(wkvb[:, :, :MLA_NOPE], (1, 2, 0)))
            wuv = bf(jnp.transpose(wkvb[:, :, MLA_NOPE:], (1, 0, 2)))
            wuvt = bf(jnp.transpose(wkvb[:, :, MLA_NOPE:], (1, 2, 0)))
            gq = g_mla_q[i].reshape(1, -1)
            gkv = g_mla_kv[i].reshape(1, -1)
            wo = bf(w_out_a[i])
            wo_a, wo_b = wo[:GLA_HEADS * GLA_DV], wo[GLA_HEADS * GLA_DV:]

            def even(x, tabs, s0, prompt):
                va, ra, qa, ka, kvl, ql, ksl, gsl = _proj(x, g_mix[l], w_in, widths)
                if prompt:
                    o_a, st = _gla(qa, ka, va, gsl, ra, s0, wgate, bgate, gnorm,
                                   seq=T, rows=gla_chunk, chunk=gla_chunk, valid=gla_chunk)
                else:
                    o_a, st = _gla(qa, ka, va, gsl, ra, s0, wgate, bgate, gnorm,
                                   seq=SP, rows=SP, chunk=2 * SP, valid=S)
                if prompt:
                    q_t, k_mla, ckv, kpe, c_t = _mla_prep(ql, kvl, ksl, tabs[0], tabs[1], gq, gkv, wqn, wqp, wuk, True)
                    o_b = _mla_prompt(q_t, k_mla, c_t, wuvt, T)
                else:
                    q_mla, k_mla, ckv, kpe = _mla_prep(ql, kvl, ksl, tabs[0], tabs[1], gq, gkv, wqn, wqp, wuk, False)
                    o_b = _mla_sample(q_mla, k_mla, wuv, cache_mla_ckv, cache_kpet, i, page_table, S)
                return _outproj(o_a, o_b, wo_a, wo_b, x), st, ckv, kpe

            xp, st, ckv, kpe = even(xp, mla_tab_p, jnp.zeros((B, GLA_HEADS, GLA_DK, GLA_DV), F32), True)
            gla_p.append(st)
            ckv_p.append(ckv.reshape(B, T, MLA_KV_LORA))
            kpe_p.append(kpe.reshape(B, T, MLA_ROPE))
            xs, st, ckv, kpe = even(xs, mla_tab_s, state_gla[i], False)
            gla_s.append(st)
            ckv_s.append(unpad(ckv))
            kpe_s.append(unpad(kpe))
        else:
            hw = MOBA_HEADS * MOBA_HD
            w_in = bf(w_in_b[i])
            widths = (hw, hw, hw, S5_WIDTH)
            pwr, pwi, bbr, bbi = _s5_discretize(s5_lam_re[i], s5_lam_im[i], s5_log_dt[i], s5_b_re[i], s5_b_im[i])
            to_blocks = lambda a: a.reshape(a.shape[0], S5_GROUPS, S5_STATE)
            bb = bf(jnp.concatenate([_s5_block_diag(to_blocks(bbr)), _s5_block_diag(to_blocks(bbi))], axis=1))
            c_t = lambda c: jnp.transpose(c.astype(F32), (2, 0, 1))
            cc = bf(jnp.concatenate([_s5_block_diag(c_t(s5_c_re[i])), -_s5_block_diag(c_t(s5_c_im[i]))], axis=0))
            dsk = s5_d[i].reshape(1, -1)
            wglu = bf(w_s5_glu[i])
            bglu = b_s5_glu[i].reshape(1, -1)
            wo = bf(w_out_b[i])
            wo_c, wo_d = wo[:hw], wo[hw:]

            q, k, v, u = _proj(xp, g_mix[l], w_in, widths)
            qr, kr, kb, vt, km = _moba_prep(q, k, v, moba_tab_p[0], moba_tab_p[1], True)
            o_c = _moba_prompt(qr, kb, vt, km.reshape(B, T // MOBA_BLOCK, hw), T)
            o_d, sr, si = _s5_prompt(u, bb, pwr, pwi, cc, dsk, wglu, bglu, T)
            xp = _outproj(o_c, o_d, wo_c, wo_d, xp)
            mk_p.append(kr.reshape(B, T, MOBA_HEADS, MOBA_HD))
            mv_p.append(v.reshape(B, T, MOBA_HEADS, MOBA_HD))
            s5_p.append(jnp.stack([sr.reshape(B, S5_GROUPS, S5_STATE), si.reshape(B, S5_GROUPS, S5_STATE)], axis=-1))

            q, k, v, u = _proj(xs, g_mix[l], w_in, widths)
            qr, kr = _moba_prep(q, k, v, moba_tab_s[0], moba_tab_s[1], False)
            o_c = _moba_sample(qr, kr, v, cache_moba_k, cache_moba_v, i, page_table, S)
            s0 = state_s5[i].astype(F32).reshape(DB, S5_N, 2)
            s0r = jnp.repeat(s0[..., 0], SP, axis=0)
            s0i = jnp.repeat(s0[..., 1], SP, axis=0)
            o_d, xr, xi = _s5_sample(u, bb, pwr, pwi, cc, dsk, wglu, bglu, s0r, s0i)
            xs = _outproj(o_c, o_d, wo_c, wo_d, xs)
            mk_s.append(unpad(kr).reshape(DB, S, MOBA_HEADS, MOBA_HD))
            mv_s.append(unpad(v).reshape(DB, S, MOBA_HEADS, MOBA_HD))
            last = lambda a: a.reshape(DB, SP, S5_GROUPS, S5_STATE)[:, S - 1]
            s5_s.append(jnp.stack([last(xr), last(xi)], axis=-1))

        mkp, mvp = _proj(mem2d, ones_d, bf(jnp.concatenate([w_xk[l], w_xv[l]], axis=1)), (D, D), norm=False)
        mem_k_p.append(mkp.reshape(B, n_mem, X_HEADS, D // X_HEADS))
        mem_v_p.append(mvp.reshape(B, n_mem, X_HEADS, D // X_HEADS))
        wq, wo = bf(w_xq[l]), bf(w_xo[l])
        xp = _xattn_prompt(xp, g_xattn[l], wq, wo, mkp.reshape(B, n_mem, D), mvp.reshape(B, n_mem, D), T)
        xs = _xattn_sample(xs, g_xattn[l], wq, wo, cache_mem_k, cache_mem_v, l)

        wg2, wu2, wd2 = bf(w_ffn2_gate[l]), bf(w_ffn2_up[l]), bf(w_ffn2_down[l])
        xp = _ffn(xp, g_ffn2[l], wg2, wu2, wd2)
        xs = _ffn(xs, g_ffn2[l], wg2, wu2, wd2)

    y_prompt = _final_norm(xp, g_final).reshape(B, T, D)
    y_sample = unpad(_final_norm(xs, g_final))
    return (y_prompt, y_sample,
            jnp.stack(mem_k_p), jnp.stack(mem_v_p),
            jnp.stack(gla_p), jnp.stack(gla_s),
            jnp.stack(ckv_p), jnp.stack(kpe_p), jnp.stack(ckv_s), jnp.stack(kpe_s),
            jnp.stack(mk_p), jnp.stack(mv_p), jnp.stack(mk_s), jnp.stack(mv_s),
            jnp.stack(s5_p), jnp.stack(s5_s))
```

```python
import functools
import math

import jax
import jax.numpy as jnp
from jax import lax
from jax.experimental import pallas as pl
from jax.experimental.pallas import tpu as pltpu

F32 = jnp.float32
BF16 = jnp.bfloat16

RMS_EPS = 1e-6
ROPE_THETA = 10000.0
NEG_INF = -1e30
BELOW_NEG_INF = -3.0e38
PAGE_SIZE = 128
GLA_HEADS, GLA_DK, GLA_DV, GLA_GATE_RANK, GLA_TAU, GLA_CHUNK = 4, 64, 128, 16, 16.0, 64
MLA_HEADS, MLA_Q_LORA, MLA_KV_LORA, MLA_NOPE, MLA_ROPE, MLA_V = 4, 384, 256, 128, 64, 128
MOBA_HEADS, MOBA_HD, MOBA_BLOCK, MOBA_TOPK = 4, 128, 256, 3
S5_WIDTH, S5_GROUP, S5_STATE = 512, 16, 64
S5_GROUPS = S5_WIDTH // S5_GROUP
S5_N = S5_GROUPS * S5_STATE
X_HEADS = 4
LANES = 128
SUBLANES = 8
SAMPLE_PAD = SUBLANES
VMEM_LIMIT = 56 * 1024 * 1024
MLA_GROUP_PAGES = 16
MLA_RING_GROUPS = 4
MOBA_GROUP_PAGES = 8
MOBA_RING_GROUPS = 4
LOG2E = math.log2(math.e)


def _cparams(*sem):
    return pltpu.CompilerParams(dimension_semantics=sem, vmem_limit_bytes=VMEM_LIMIT)


def _row_tile(m, target):
    t = min(m, target)
    while m % t or t % SUBLANES:
        t -= 1
    return t


def _rms(x, g):
    return x * lax.rsqrt(jnp.mean(x * x, axis=-1, keepdims=True) + RMS_EPS) * g


def _sigmoid(x):
    return 1.0 / (1.0 + jnp.exp(-x))


def _dot(a, b):
    return jnp.dot(a, b, preferred_element_type=F32)


def _dot_nt(a, b):
    return lax.dot_general(a, b, (((1,), (1,)), ((), ())), preferred_element_type=F32)


def _dot_nt_f32(a, b):
    return lax.dot_general(a, b, (((1,), (1,)), ((), ())), precision=lax.Precision.HIGHEST,
                           preferred_element_type=F32)


def _dot_tn(a, b):
    return lax.dot_general(a, b, (((0,), (0,)), ((), ())), preferred_element_type=F32)


def _full(a):
    return pl.BlockSpec(a.shape, lambda *_: (0,) * a.ndim)


def _ffn_kernel(x_ref, g_ref, wg_ref, wu_ref, wd_ref, o_ref, z_ref, *, tf):
    x = x_ref[...]
    h = _rms(x, g_ref[...]).astype(BF16)
    for c in range(wg_ref.shape[1] // tf):
        cs = slice(c * tf, (c + 1) * tf)
        a = _dot(h, wg_ref[:, cs])
        z_ref[:, cs] = (a * _sigmoid(a) * _dot(h, wu_ref[:, cs])).astype(BF16)
    o_ref[...] = x + 0.5 * _dot(z_ref[...], wd_ref[...])


def _ffn(x, g, wg, wu, wd):
    m, d = x.shape
    f = wg.shape[1]
    tm = _row_tile(m, 512)
    tf = 256 if f % 256 == 0 else f
    once = pl.Buffered(1)
    return pl.pallas_call(
        functools.partial(_ffn_kernel, tf=tf),
        grid=(m // tm,),
        in_specs=[
            pl.BlockSpec((tm, d), lambda i: (i, 0)),
            pl.BlockSpec((1, d), lambda i: (0, 0)),
            pl.BlockSpec((d, f), lambda i: (0, 0), pipeline_mode=once),
            pl.BlockSpec((d, f), lambda i: (0, 0), pipeline_mode=once),
            pl.BlockSpec((f, d), lambda i: (0, 0), pipeline_mode=once),
        ],
        out_specs=pl.BlockSpec((tm, d), lambda i: (i, 0)),
        out_shape=jax.ShapeDtypeStruct((m, d), F32),
        scratch_shapes=[pltpu.VMEM((tm, f), BF16)],
        compiler_params=_cparams("parallel"),
        name="ffn",
    )(x, g.reshape(1, d), wg, wu, wd)


def _proj_kernel(x_ref, g_ref, w_ref, *o_refs, widths, norm):
    x = x_ref[...]
    h = _rms(x, g_ref[...]) if norm else x
    y = _dot(h.astype(BF16), w_ref[...])
    off = 0
    for o_ref, wd in zip(o_refs, widths):
        o_ref[...] = y[:, off:off + wd].astype(o_ref.dtype)
        off += wd


def _proj(x, g, w, widths, norm=True):
    m, d = x.shape
    n = w.shape[1]
    assert sum(widths) == n
    tm = _row_tile(m, 512)
    return pl.pallas_call(
        functools.partial(_proj_kernel, widths=widths, norm=norm),
        grid=(m // tm,),
        in_specs=[pl.BlockSpec((tm, d), lambda i: (i, 0)), pl.BlockSpec((1, d), lambda i: (0, 0)), _full(w)],
        out_specs=[pl.BlockSpec((tm, wd), lambda i: (i, 0)) for wd in widths],
        out_shape=[jax.ShapeDtypeStruct((m, wd), F32) for wd in widths],
        compiler_params=_cparams("parallel"),
        name="proj",
    )(x, g.reshape(1, d), w)


def _outproj_kernel(a_ref, b_ref, wa_ref, wb_ref, x_ref, o_ref):
    o_ref[...] = x_ref[...] + _dot(a_ref[...], wa_ref[...]) + _dot(b_ref[...], wb_ref[...])


def _outproj(a, b, wa, wb, x):
    m, d = x.shape
    tm = _row_tile(m, 512)
    return pl.pallas_call(
        _outproj_kernel,
        grid=(m // tm,),
        in_specs=[
            pl.BlockSpec((tm, a.shape[1]), lambda i: (i, 0)),
            pl.BlockSpec((tm, b.shape[1]), lambda i: (i, 0)),
            _full(wa), _full(wb),
            pl.BlockSpec((tm, d), lambda i: (i, 0)),
        ],
        out_specs=pl.BlockSpec((tm, d), lambda i: (i, 0)),
        out_shape=jax.ShapeDtypeStruct((m, d), F32),
        compiler_params=_cparams("parallel"),
        name="outproj",
    )(a, b, wa, wb, x)


def _norm_kernel(x_ref, g_ref, o_ref):
    o_ref[...] = _rms(x_ref[...], g_ref[...])


def _final_norm(x, g):
    m, d = x.shape
    tm = _row_tile(m, 1024)
    return pl.pallas_call(
        _norm_kernel,
        grid=(m // tm,),
        in_specs=[pl.BlockSpec((tm, d), lambda i: (i, 0)), pl.BlockSpec((1, d), lambda i: (0, 0))],
        out_specs=pl.BlockSpec((tm, d), lambda i: (i, 0)),
        out_shape=jax.ShapeDtypeStruct((m, d), F32),
        compiler_params=_cparams("parallel"),
        name="final_norm",
    )(x, g.reshape(1, d))


def _xattn_core(x, g, wq, wo, mem_k, mem_v, seq_rows):
    d = x.shape[1]
    hd = d // X_HEADS
    q = (_dot(_rms(x, g).astype(BF16), wq) * (hd ** -0.5)).astype(BF16)
    nseq = x.shape[0] // seq_rows
    pairs = [(s, h) for s in range(nseq) for h in range(X_HEADS)]
    sc = [_dot_nt(q[s * seq_rows:(s + 1) * seq_rows, h * hd:(h + 1) * hd], mem_k(s, h)) for s, h in pairs]
    p = [jnp.exp(t - jnp.max(t, axis=-1, keepdims=True)) for t in sc]
    p = [t / jnp.sum(t, axis=-1, keepdims=True) for t in p]
    pv = [_dot(t.astype(BF16), mem_v(s, h)) for t, (s, h) in zip(p, pairs)]
    outs = [jnp.concatenate(pv[s * X_HEADS:(s + 1) * X_HEADS], axis=-1) for s in range(nseq)]
    o = outs[0] if nseq == 1 else jnp.concatenate(outs, axis=0)
    return x + _dot(o.astype(BF16), wo)


def _xattn_prompt_kernel(x_ref, g_ref, wq_ref, wo_ref, mk_ref, mv_ref, o_ref):
    x = x_ref[...]
    hd = x.shape[1] // X_HEADS
    mk = mk_ref[...].astype(BF16)
    mv = mv_ref[...].astype(BF16)
    o_ref[...] = _xattn_core(x, g_ref[...], wq_ref[...], wo_ref[...],
                             lambda s, h: mk[:, h * hd:(h + 1) * hd], lambda s, h: mv[:, h * hd:(h + 1) * hd],
                             x.shape[0])


def _xattn_prompt(x, g, wq, wo, mk, mv, seq):
    m, d = x.shape
    nm = mk.shape[1]
    tm = _row_tile(seq, 512)
    nt = seq // tm
    return pl.pallas_call(
        _xattn_prompt_kernel,
        grid=(m // tm,),
        in_specs=[
            pl.BlockSpec((tm, d), lambda i: (i, 0)),
            pl.BlockSpec((1, d), lambda i: (0, 0)),
            _full(wq), _full(wo),
            pl.BlockSpec((None, nm, d), lambda i: (i // nt, 0, 0)),
            pl.BlockSpec((None, nm, d), lambda i: (i // nt, 0, 0)),
        ],
        out_specs=pl.BlockSpec((tm, d), lambda i: (i, 0)),
        out_shape=jax.ShapeDtypeStruct((m, d), F32),
        compiler_params=_cparams("parallel"),
        name="xattn_prompt",
    )(x, g.reshape(1, d), wq, wo, mk, mv)


def _xattn_sample_kernel(x_ref, g_ref, wq_ref, wo_ref, mk_ref, mv_ref, o_ref, *, nseq):
    nc = x_ref.shape[1] // X_HEADS // LANES
    nm = mk_ref.shape[1] // (X_HEADS * nc)

    def head(ref, s, h):
        tiles = [ref[s, pl.ds(h + X_HEADS * c, nm, stride=X_HEADS * nc), :] for c in range(nc)]
        return jnp.concatenate(tiles, axis=1).astype(BF16)

    o_ref[...] = _xattn_core(x_ref[...], g_ref[...], wq_ref[...], wo_ref[...],
                             lambda s, h: head(mk_ref, s, h), lambda s, h: head(mv_ref, s, h),
                             x_ref.shape[0] // nseq)


def _xattn_sample(x, g, wq, wo, cache_k, cache_v, layer):
    m, d = x.shape
    depth, db, nm, nh, hd = cache_k.shape
    assert nh == X_HEADS and hd % LANES == 0
    nc = hd // LANES

    def rows_view(c):
        c = c.reshape(depth, db, nm, nh, nc, LANES)
        return jnp.transpose(c, (0, 1, 2, 4, 3, 5)).reshape(depth, db, nm * nc * nh, LANES)

    cache_k, cache_v = rows_view(cache_k), rows_view(cache_v)
    nseq = 4 if db % 4 == 0 else 1
    tm = nseq * (m // db)
    cache_spec = pl.BlockSpec((None, nseq, nm * nc * nh, LANES), lambda i: (layer, i, 0, 0))
    return pl.pallas_call(
        functools.partial(_xattn_sample_kernel, nseq=nseq),
        grid=(db // nseq,),
        in_specs=[
            pl.BlockSpec((tm, d), lambda i: (i, 0)),
            pl.BlockSpec((1, d), lambda i: (0, 0)),
            _full(wq), _full(wo), cache_spec, cache_spec,
        ],
        out_specs=pl.BlockSpec((tm, d), lambda i: (i, 0)),
        out_shape=jax.ShapeDtypeStruct((m, d), F32),
        compiler_params=_cparams("parallel"),
        name="xattn_sample",
    )(x, g.reshape(1, d), wq, wo, cache_k, cache_v)


def _split3(x):
    hi = x.astype(BF16)
    r = x - hi.astype(F32)
    mid = r.astype(BF16)
    lo = (r - mid.astype(F32)).astype(BF16)
    return hi, mid, lo


def _gla_kernel(q_ref, k_ref, v_ref, gs_ref, ra_ref, s0_ref, wg_ref, bg_ref, gn_ref, o_ref, sout_ref, st_ref,
                *, rows, chunk, nchunks, valid):
    t = pl.program_id(1)

    @pl.when(t == 0)
    def _():
        st_ref[...] = s0_ref[...]

    L = chunk
    r_io = lax.broadcasted_iota(jnp.int32, (L, L), 0)
    c_io = lax.broadcasted_iota(jnp.int32, (L, L), 1)
    tril = r_io >= c_io
    tril_bf = jnp.where(tril, 1.0, 0.0).astype(BF16)
    ones_bf = jnp.ones((L, GLA_DV), BF16)
    live = lax.broadcasted_iota(jnp.int32, (L, 1), 0) < valid

    def load(ref, r0):
        x = ref[pl.ds(r0, rows), :]
        if rows < L:
            x = jnp.concatenate([x, jnp.zeros((L - rows, x.shape[1]), x.dtype)], axis=0)
        return x

    def body(c, carry):
        r0 = pl.multiple_of(c * rows, rows)
        q = load(q_ref, r0) * (GLA_DK ** -0.5)
        k = load(k_ref, r0)
        v = load(v_ref, r0)
        ra = load(ra_ref, r0)
        z = _dot(load(gs_ref, r0).astype(BF16), wg_ref[...]) + bg_ref[...]
        lg = (jnp.minimum(z, 0.0) - jnp.log1p(jnp.exp(-jnp.abs(z)))) * (1.0 / GLA_TAU)
        if valid < L:
            lg = jnp.where(live, lg, 0.0)
            k = jnp.where(live, k, 0.0)
            v = jnp.where(live, v, 0.0)
        hi, mid, lo = _split3(lg)
        b = _dot(tril_bf, hi) + _dot(tril_bf, mid) + _dot(tril_bf, lo)
        bl = b[L - 1:L, :]
        qe = (q * jnp.exp(b)).astype(BF16)
        ke = (k * jnp.exp(-b)).astype(BF16)
        kd = (k * jnp.exp(bl - b)).astype(BF16)
        vb = v.astype(BF16)
        heads = range(GLA_HEADS)
        ks = [slice(h * GLA_DK, (h + 1) * GLA_DK) for h in heads]
        vs = [slice(h * GLA_DV, (h + 1) * GLA_DV) for h in heads]
        att = [jnp.where(tril, _dot_nt(qe[:, ks[h]], ke[:, ks[h]]), 0.0).astype(BF16) for h in heads]
        s_old = [st_ref[h] for h in heads]
        o_h = [_dot(qe[:, ks[h]], s_old[h].astype(BF16)) + _dot(att[h], vb[:, vs[h]]) for h in heads]
        dec = [_dot_tn(hi[:, ks[h]], ones_bf) + _dot_tn(mid[:, ks[h]], ones_bf) + _dot_tn(lo[:, ks[h]], ones_bf)
               for h in heads]
        upd = [_dot_tn(kd[:, ks[h]], vb[:, vs[h]]) for h in heads]
        for h in heads:
            st_ref[h] = s_old[h] * jnp.exp(dec[h]) + upd[h]
        gate = ra * _sigmoid(ra)
        o = jnp.concatenate([_rms(o_h[h], gn_ref[...]) for h in heads], axis=-1) * gate
        o_ref[pl.ds(r0, rows), :] = o[:rows].astype(o_ref.dtype)
        return carry

    if nchunks == 1:
        body(0, 0)
    else:
        lax.fori_loop(0, nchunks, body, 0)

    @pl.when(t == pl.num_programs(1) - 1)
    def _():
        sout_ref[...] = st_ref[...]


def _gla(q, k, v, gs, ra, s0, wg, bg, gn, *, seq, rows, chunk, valid):
    m = q.shape[0]
    nb = m // seq
    tt = _row_tile(seq, 512) if rows == chunk else rows
    nchunks = tt // rows
    nt = seq // tt
    row_spec = lambda w: pl.BlockSpec((tt, w), lambda b, t: (b * nt + t, 0))
    st_spec = pl.BlockSpec((None, GLA_HEADS, GLA_DK, GLA_DV), lambda b, t: (b, 0, 0, 0))
    return pl.pallas_call(
        functools.partial(_gla_kernel, rows=rows, chunk=chunk, nchunks=nchunks, valid=valid),
        grid=(nb, nt),
        in_specs=[row_spec(q.shape[1]), row_spec(k.shape[1]), row_spec(v.shape[1]), row_spec(gs.shape[1]),
                  row_spec(ra.shape[1]), st_spec, _full(wg), _full(bg), _full(gn)],
        out_specs=[row_spec(v.shape[1]), st_spec],
        out_shape=[jax.ShapeDtypeStruct((m, v.shape[1]), BF16),
                   jax.ShapeDtypeStruct((nb, GLA_HEADS, GLA_DK, GLA_DV), F32)],
        scratch_shapes=[pltpu.VMEM((GLA_HEADS, GLA_DK, GLA_DV), F32)],
        compiler_params=_cparams("parallel", "arbitrary"),
        name="gla",
    )(q, k, v, gs, ra, s0, wg, bg, gn)


def _rope(x, cos, sin_signed, group):
    w = x.shape[-1]
    ax = x.ndim - 1
    lane = lax.broadcasted_iota(jnp.int32, x.shape, ax)
    first = (lane % group) < (group // 2)
    partner = jnp.where(first, pltpu.roll(x, w - group // 2, ax), pltpu.roll(x, group // 2, ax))
    return x * cos + partner * sin_signed


def _rope_tables(pos, dim, width, reps):
    half = dim // 2
    inv = ROPE_THETA ** (-jnp.arange(half, dtype=F32) / half)
    ang = pos.astype(F32)[:, None] * inv[None, :]
    cos, sin = jnp.cos(ang), jnp.sin(ang)
    pad = jnp.zeros((pos.shape[0], width - dim), F32)
    c = jnp.concatenate([cos, cos, pad], axis=-1)
    s = jnp.concatenate([-sin, sin, pad], axis=-1)
    return jnp.tile(c, (1, reps)), jnp.tile(s, (1, reps))


MLA_QK = MLA_KV_LORA + LANES


def _mla_prep_kernel(ql_ref, kvl_ref, ks_ref, cos_ref, sin_ref, gq_ref, gkv_ref, wqn_ref, wqp_ref, wuk_ref,
                     q_ref, k_ref, ckv_ref, kpe_ref, *ct_ref, transposed):
    scale = (MLA_NOPE + MLA_ROPE) ** -0.5 * LOG2E
    hq = _rms(ql_ref[...], gq_ref[...]).astype(BF16)
    qn = _dot(hq, wqn_ref[...]).astype(BF16)
    cos, sin = cos_ref[...], sin_ref[...]
    qp = _rope(_dot(hq, wqp_ref[...]), cos, sin, MLA_ROPE)
    parts = []
    for h in range(MLA_HEADS):
        parts.append(_dot(qn[:, h * MLA_NOPE:(h + 1) * MLA_NOPE], wuk_ref[h]))
        parts.append(qp[:, h * LANES:(h + 1) * LANES])
    q = jnp.concatenate(parts, axis=-1) * scale
    ckv = _rms(kvl_ref[...], gkv_ref[...])
    ckv_ref[...] = ckv
    kpe = _rope(ks_ref[...], cos[:, :LANES], sin[:, :LANES], MLA_ROPE)
    kpe_ref[...] = kpe[:, :MLA_ROPE]
    k_ref[...] = jnp.concatenate([ckv, kpe], axis=-1).astype(BF16)
    if transposed:
        q_ref[...] = q.T.astype(BF16)
        ct_ref[0][...] = ckv.T.astype(BF16)
    else:
        q_ref[...] = q.astype(BF16)


def _mla_prep(q_lat, kv_lat, ks, cos, sin, gq, gkv, wqn, wqp, wuk, transposed):
    m = q_lat.shape[0]
    p = cos.shape[0]
    tm = _row_tile(math.gcd(m, p), 512)
    npos = p // tm
    row = lambda w: pl.BlockSpec((tm, w), lambda i: (i, 0))
    col = lambda w: pl.BlockSpec((w, tm), lambda i: (0, i))
    pos = lambda w: pl.BlockSpec((tm, w), lambda i: (i % npos, 0))
    hq, hp = MLA_HEADS * MLA_QK, MLA_HEADS * LANES
    out_specs = [col(hq) if transposed else row(hq), row(MLA_QK), row(MLA_KV_LORA), row(MLA_ROPE)]
    out_shape = [jax.ShapeDtypeStruct((hq, m) if transposed else (m, hq), BF16),
                 jax.ShapeDtypeStruct((m, MLA_QK), BF16),
                 jax.ShapeDtypeStruct((m, MLA_KV_LORA), F32), jax.ShapeDtypeStruct((m, MLA_ROPE), F32)]
    if transposed:
        out_specs.append(col(MLA_KV_LORA))
        out_shape.append(jax.ShapeDtypeStruct((MLA_KV_LORA, m), BF16))
    return pl.pallas_call(
        functools.partial(_mla_prep_kernel, transposed=transposed),
        grid=(m // tm,),
        in_specs=[row(MLA_Q_LORA), row(MLA_KV_LORA), row(LANES), pos(hp), pos(hp),
                  _full(gq), _full(gkv), _full(wqn), _full(wqp), _full(wuk)],
        out_specs=out_specs,
        out_shape=out_shape,
        compiler_params=_cparams("parallel"),
        name="mla_prep",
    )(q_lat, kv_lat, ks, cos, sin, gq, gkv, wqn, wqp, wuk)


def _stack_heads(x, width):
    return jnp.concatenate([x[:, h * width:(h + 1) * width] for h in range(MLA_HEADS)], axis=0)


def _softmax_update(s, v, m_ref, l_ref, acc_ref):
    m_prev = m_ref[...]
    m_new = jnp.maximum(m_prev, jnp.max(s, axis=-1, keepdims=True))
    alpha = jnp.exp2(m_prev - m_new)
    p = jnp.exp2(s - m_new)
    l_ref[...] = alpha * l_ref[...] + jnp.sum(p, axis=-1, keepdims=True)
    acc_ref[...] = alpha * acc_ref[...] + _dot(p.astype(BF16), v)
    m_ref[...] = m_new


def _softmax_init(m_ref, l_ref, acc_ref):
    m_ref[...] = jnp.full_like(m_ref, NEG_INF)
    l_ref[...] = jnp.zeros_like(l_ref)
    acc_ref[...] = jnp.zeros_like(acc_ref)


def _mla_out(acc_ref, l_ref, wuv_ref, rows):
    o_lat = (acc_ref[...] / l_ref[...]).astype(BF16)
    return jnp.concatenate([_dot(o_lat[h * rows:(h + 1) * rows], wuv_ref[h]) for h in range(MLA_HEADS)], axis=-1)


def _mla_prompt_kernel(qi_ref, ki_ref, qt_ref, k_ref, ct_ref, wuvt_ref, o_ref, qs_ref, m_ref, l_ref, acc_ref,
                       *, tq, tk):
    j = pl.program_id(1)
    qi, ki = qi_ref[j], ki_ref[j]
    last = ((qi + 1) * tq - 1) // tk

    @pl.when(ki == 0)
    def _():
        qt = qt_ref[...]
        qs_ref[...] = jnp.concatenate([qt[h * MLA_QK:(h + 1) * MLA_QK] for h in range(MLA_HEADS)], axis=1)
        _softmax_init(m_ref, l_ref, acc_ref)

    def step(masked):
        heads = range(MLA_HEADS)
        cs = [slice(h * tq, (h + 1) * tq) for h in heads]
        k, ct = k_ref[...], ct_ref[...]
        st = [_dot(k, qs_ref[:, cs[h]]) for h in heads]
        if masked:
            keep = (ki * tk + lax.broadcasted_iota(jnp.int32, (tk, tq), 0)
                    <= qi * tq + lax.broadcasted_iota(jnp.int32, (tk, tq), 1))
            st = [jnp.where(keep, st[h], NEG_INF) for h in heads]
        m_prev = [m_ref[:, cs[h]] for h in heads]
        m_new = [jnp.maximum(m_prev[h], jnp.max(st[h], axis=0, keepdims=True)) for h in heads]
        alpha = [jnp.exp2(m_prev[h] - m_new[h]) for h in heads]
        p = [jnp.exp2(st[h] - m_new[h]) for h in heads]
        pv = [_dot(ct, p[h].astype(BF16)) for h in heads]
        for h in heads:
            m_ref[:, cs[h]] = m_new[h]
            l_ref[:, cs[h]] = alpha[h] * l_ref[:, cs[h]] + jnp.sum(p[h], axis=0, keepdims=True)
            acc_ref[:, cs[h]] = alpha[h] * acc_ref[:, cs[h]] + pv[h]

    crosses = (ki + 1) * tk - 1 > qi * tq

    @pl.when(crosses)
    def _():
        step(True)

    @pl.when(jnp.logical_not(crosses))
    def _():
        step(False)

    @pl.when(ki == last)
    def _():
        o_lat = (acc_ref[...] / l_ref[...]).astype(BF16)
        ot = jnp.concatenate([_dot(wuvt_ref[h], o_lat[:, h * tq:(h + 1) * tq]) for h in range(MLA_HEADS)], axis=0)
        o_ref[...] = ot.T.astype(o_ref.dtype)


def _mla_prompt(qt, k, ct, wuvt, seq):
    m = k.shape[0]
    nb = m // seq
    tq = tk = _row_tile(seq, 512)
    nq, nk = seq // tq, seq // tk
    pairs = [(qi, ki) for qi in range(nq) for ki in range(((qi + 1) * tq - 1) // tk + 1)]
    qi_tab = jnp.asarray([p[0] for p in pairs], jnp.int32)
    ki_tab = jnp.asarray([p[1] for p in pairs], jnp.int32)
    cols = MLA_HEADS * tq
    grid_spec = pltpu.PrefetchScalarGridSpec(
        num_scalar_prefetch=2,
        grid=(nb, len(pairs)),
        in_specs=[pl.BlockSpec((qt.shape[0], tq), lambda b, j, qi, ki: (0, b * nq + qi[j])),
                  pl.BlockSpec((tk, k.shape[1]), lambda b, j, qi, ki: (b * nk + ki[j], 0)),
                  pl.BlockSpec((ct.shape[0], tk), lambda b, j, qi, ki: (0, b * nk + ki[j])),
                  pl.BlockSpec(wuvt.shape, lambda b, j, qi, ki: (0, 0, 0))],
        out_specs=pl.BlockSpec((tq, MLA_HEADS * MLA_V), lambda b, j, qi, ki: (b * nq + qi[j], 0)),
        scratch_shapes=[pltpu.VMEM((MLA_QK, cols), BF16), pltpu.VMEM((1, cols), F32), pltpu.VMEM((1, cols), F32),
                        pltpu.VMEM((MLA_KV_LORA, cols), F32)],
    )
    return pl.pallas_call(
        functools.partial(_mla_prompt_kernel, tq=tq, tk=tk),
        grid_spec=grid_spec,
        out_shape=jax.ShapeDtypeStruct((m, MLA_HEADS * MLA_V), BF16),
        compiler_params=_cparams("parallel", "arbitrary"),
        name="mla_prompt",
    )(qi_tab, ki_tab, qt, k, ct, wuvt)


def _mla_sample_kernel(pt_ref, q_ref, kn_ref, wuv_ref, ckv_hbm, kpet_hbm, o_ref,
                       cbuf, pbuf, sem, qs_ref, m_ref, l_ref, acc_ref, *, layer, n_pages, gp, ring, valid):
    b = pl.program_id(0)
    nseq = pl.num_programs(0)
    total = n_pages // gp
    look = ring - 1
    sp = q_ref.shape[0]

    def page_copies(page, slot, j):
        return (pltpu.make_async_copy(ckv_hbm.at[layer, page], cbuf.at[slot * gp + j], sem.at[0, slot]),
                pltpu.make_async_copy(kpet_hbm.at[layer, page], pbuf.at[slot * gp + j], sem.at[1, slot]))

    def start_group(seq, gi):
        for j in range(gp):
            for cp in page_copies(pt_ref[seq * n_pages + gi * gp + j], gi % ring, j):
                cp.start()

    def wait_group(gi):
        for j in range(gp):
            for cp in page_copies(0, gi % ring, j):
                cp.wait()

    @pl.when(b == 0)
    def _():
        for gi in range(look):
            start_group(b, gi)

    qs_ref[...] = _stack_heads(q_ref[...], MLA_QK)
    _softmax_init(m_ref, l_ref, acc_ref)
    qs = qs_ref[...]
    qc, qp = qs[:, :MLA_KV_LORA], qs[:, MLA_KV_LORA:MLA_KV_LORA + MLA_ROPE]
    nsub = math.gcd(gp, 4)
    per = gp // nsub
    sub = range(nsub)

    for gi in range(total):
        ahead = gi + look
        if ahead < total:
            start_group(b, ahead)
        else:
            @pl.when(b + 1 < nseq)
            def _():
                start_group(b + 1, ahead - total)
        wait_group(gi)
        base = (gi % ring) * gp
        ckv = [jnp.concatenate([cbuf[base + u * per + j].astype(BF16) for j in range(per)], axis=0) for u in sub]
        kpet = [jnp.concatenate([pbuf[base + u * per + j].astype(BF16) for j in range(per)], axis=1) for u in sub]
        s = [_dot_nt(qc, ckv[u]) + _dot(qp, kpet[u]) for u in sub]
        mg = [jnp.max(s[u], axis=-1, keepdims=True) for u in sub]
        p = [jnp.exp2(s[u] - mg[u]) for u in sub]
        lg = [jnp.sum(p[u], axis=-1, keepdims=True) for u in sub]
        pv = [_dot(p[u].astype(BF16), ckv[u]) for u in sub]
        m_prev = m_ref[...]
        m_new = functools.reduce(jnp.maximum, mg, m_prev)
        alpha = jnp.exp2(m_prev - m_new)
        w = [jnp.exp2(mg[u] - m_new) for u in sub]
        l_ref[...] = alpha * l_ref[...] + sum(w[u] * lg[u] for u in sub)
        acc_ref[...] = alpha * acc_ref[...] + sum(w[u] * pv[u] for u in sub)
        m_ref[...] = m_new

    kn = kn_ref[...]
    s = _dot_nt(qs, kn)
    rows = MLA_HEADS * sp
    qpos = lax.broadcasted_iota(jnp.int32, (rows, sp), 0) % sp
    kpos = lax.broadcasted_iota(jnp.int32, (rows, sp), 1)
    s = jnp.where(jnp.logical_and(kpos <= qpos, kpos < valid), s, NEG_INF)
    _softmax_update(s, kn[:, :MLA_KV_LORA], m_ref, l_ref, acc_ref)
    o_ref[...] = _mla_out(acc_ref, l_ref, wuv_ref, sp).astype(o_ref.dtype)


def _mla_sample(q, kn, wuv, cache_ckv, cache_kpet, layer, page_table, valid):
    m = q.shape[0]
    db, n_pages = page_table.shape
    sp = m // db
    gp = math.gcd(n_pages, MLA_GROUP_PAGES)
    ring = math.gcd(n_pages // gp, MLA_RING_GROUPS)
    rows = MLA_HEADS * sp
    qmap = lambda b, pt: (b, 0)
    grid_spec = pltpu.PrefetchScalarGridSpec(
        num_scalar_prefetch=1,
        grid=(db,),
        in_specs=[pl.BlockSpec((sp, q.shape[1]), qmap), pl.BlockSpec((sp, kn.shape[1]), qmap),
                  pl.BlockSpec(wuv.shape, lambda b, pt: (0, 0, 0)),
                  pl.BlockSpec(memory_space=pl.ANY), pl.BlockSpec(memory_space=pl.ANY)],
        out_specs=pl.BlockSpec((sp, MLA_HEADS * MLA_V), qmap),
        scratch_shapes=[pltpu.VMEM((ring * gp, PAGE_SIZE, MLA_KV_LORA), F32),
                        pltpu.VMEM((ring * gp, MLA_ROPE, PAGE_SIZE), F32),
                        pltpu.SemaphoreType.DMA((2, ring)),
                        pltpu.VMEM((rows, MLA_QK), BF16), pltpu.VMEM((rows, 1), F32), pltpu.VMEM((rows, 1), F32),
                        pltpu.VMEM((rows, MLA_KV_LORA), F32)],
    )
    return pl.pallas_call(
        functools.partial(_mla_sample_kernel, layer=layer, n_pages=n_pages, gp=gp, ring=ring, valid=valid),
        grid_spec=grid_spec,
        out_shape=jax.ShapeDtypeStruct((m, MLA_HEADS * MLA_V), BF16),
        compiler_params=_cparams("arbitrary"),
        name="mla_sample",
    )(page_table.reshape(-1), q, kn, wuv, cache_ckv, cache_kpet)


def _moba_prep_kernel(q_ref, k_ref, v_ref, cos_ref, sin_ref, qr_ref, kr_ref, *rest, nblk):
    cos, sin = cos_ref[...], sin_ref[...]
    qr_ref[...] = _rope(q_ref[...], cos, sin, MOBA_HD)
    kr = _rope(k_ref[...], cos, sin, MOBA_HD)
    kr_ref[...] = kr
    if nblk:
        kb_ref, vt_ref, km_ref = rest
        kb_ref[...] = kr.astype(BF16)
        vt_ref[...] = v_ref[...].T.astype(BF16)
        for j in range(nblk):
            km_ref[j] = jnp.mean(kr[j * MOBA_BLOCK:(j + 1) * MOBA_BLOCK], axis=0, keepdims=True)


def _moba_prep(q, k, v, cos, sin, prompt):
    m, w = q.shape
    p = cos.shape[0]
    tm = _row_tile(math.gcd(m, p), 512)
    npos = p // tm
    if prompt:
        assert tm % MOBA_BLOCK == 0
    nblk = tm // MOBA_BLOCK if prompt else 0
    row = pl.BlockSpec((tm, w), lambda i: (i, 0))
    pos = pl.BlockSpec((tm, w), lambda i: (i % npos, 0))
    out_specs = [row, row]
    out_shape = [jax.ShapeDtypeStruct((m, w), F32), jax.ShapeDtypeStruct((m, w), F32)]
    if prompt:
        out_specs += [row, pl.BlockSpec((w, tm), lambda i: (0, i)), pl.BlockSpec((nblk, 1, w), lambda i: (i, 0, 0))]
        out_shape += [jax.ShapeDtypeStruct((m, w), BF16), jax.ShapeDtypeStruct((w, m), BF16),
                      jax.ShapeDtypeStruct((m // MOBA_BLOCK, 1, w), F32)]
    return pl.pallas_call(
        functools.partial(_moba_prep_kernel, nblk=nblk),
        grid=(m // tm,),
        in_specs=[row, row, row, pos, pos],
        out_specs=out_specs,
        out_shape=out_shape,
        compiler_params=_cparams("parallel"),
        name="moba_prep",
    )(q, k, v, cos, sin)


def _top_rows(gate, idx, count, limit, axis):
    sel = jnp.zeros(gate.shape, F32)
    for _ in range(count):
        mx = jnp.max(gate, axis=axis, keepdims=True)
        first = jnp.min(jnp.where(gate == mx, idx, limit), axis=axis, keepdims=True)
        pick = idx == first
        sel = jnp.where(pick, 1.0, sel)
        gate = jnp.where(pick, BELOW_NEG_INF, gate)
    return sel


def _moba_prompt_kernel(q_ref, k_ref, vt_ref, km_ref, o_ref, qt_ref, sel_ref, m_ref, l_ref, acc_ref, *, nb, topn):
    qi = pl.program_id(1)
    blk, hd = MOBA_BLOCK, MOBA_HD
    b_idx = lax.broadcasted_iota(jnp.int32, (nb, blk), 0)
    past = b_idx < qi
    causal = lax.broadcasted_iota(jnp.int32, (blk, blk), 0) <= lax.broadcasted_iota(jnp.int32, (blk, blk), 1)
    r0 = pl.multiple_of(qi * blk, blk)

    heads = range(MOBA_HEADS)
    hsl = [slice(h * hd, (h + 1) * hd) for h in heads]
    q = [q_ref[:, hsl[h]] for h in heads]
    qt = [(q[h] * (hd ** -0.5 * LOG2E)).T.astype(BF16) for h in heads]
    st = [jnp.where(causal, _dot(k_ref[pl.ds(r0, blk), hsl[h]], qt[h]), NEG_INF) for h in heads]
    m0 = [jnp.max(st[h], axis=0, keepdims=True) for h in heads]
    p0 = [jnp.exp2(st[h] - m0[h]) for h in heads]
    pv0 = [_dot(vt_ref[hsl[h], pl.ds(r0, blk)], p0[h].astype(BF16)) for h in heads]
    gate = [jnp.where(past, _dot_nt_f32(km_ref[:, hsl[h]], q[h]), NEG_INF) for h in heads]
    for h in heads:
        qt_ref[h] = qt[h]
        m_ref[h] = m0[h]
        l_ref[h] = jnp.sum(p0[h], axis=0, keepdims=True)
        acc_ref[h] = pv0[h]
        sel_ref[h] = jnp.where(past, _top_rows(gate[h], b_idx, topn, nb, 0), 0.0)

    def body(n, carry):
        rn = pl.multiple_of(n * blk, blk)
        s = [_dot(k_ref[pl.ds(rn, blk), hsl[h]], qt_ref[h]) for h in heads]
        s = [jnp.where(sel_ref[h, pl.ds(n, 1), :] > 0.0, s[h], NEG_INF) for h in heads]
        m_old = [m_ref[h] for h in heads]
        m_new = [jnp.maximum(m_old[h], jnp.max(s[h], axis=0, keepdims=True)) for h in heads]
        alpha = [jnp.exp2(m_old[h] - m_new[h]) for h in heads]
        p = [jnp.exp2(s[h] - m_new[h]) for h in heads]
        pv = [_dot(vt_ref[hsl[h], pl.ds(rn, blk)], p[h].astype(BF16)) for h in heads]
        for h in heads:
            m_ref[h] = m_new[h]
            l_ref[h] = alpha[h] * l_ref[h] + jnp.sum(p[h], axis=0, keepdims=True)
            acc_ref[h] = alpha[h] * acc_ref[h] + pv[h]
        return carry

    lax.fori_loop(0, qi, body, 0)
    o_ref[...] = jnp.concatenate([(acc_ref[h] / l_ref[h]).T for h in range(MOBA_HEADS)], axis=1).astype(o_ref.dtype)


def _moba_prompt(qr, kb, vt, km, seq):
    m, hw = qr.shape
    nbatch = m // seq
    assert seq % MOBA_BLOCK == 0
    nb = seq // MOBA_BLOCK
    topn = min(MOBA_TOPK, nb)
    qmap = lambda b, qi: (b * nb + qi, 0)
    once = pl.Buffered(1)
    return pl.pallas_call(
        functools.partial(_moba_prompt_kernel, nb=nb, topn=topn),
        grid=(nbatch, nb),
        in_specs=[pl.BlockSpec((MOBA_BLOCK, hw), qmap),
                  pl.BlockSpec((seq, hw), lambda b, qi: (b, 0), pipeline_mode=once),
                  pl.BlockSpec((hw, seq), lambda b, qi: (0, b), pipeline_mode=once),
                  pl.BlockSpec((None, nb, hw), lambda b, qi: (b, 0, 0))],
        out_specs=pl.BlockSpec((MOBA_BLOCK, hw), qmap),
        out_shape=jax.ShapeDtypeStruct((m, hw), BF16),
        scratch_shapes=[pltpu.VMEM((MOBA_HEADS, MOBA_HD, MOBA_BLOCK), BF16),
                        pltpu.VMEM((MOBA_HEADS, nb, MOBA_BLOCK), F32),
                        pltpu.VMEM((MOBA_HEADS, 1, MOBA_BLOCK), F32),
                        pltpu.VMEM((MOBA_HEADS, 1, MOBA_BLOCK), F32),
                        pltpu.VMEM((MOBA_HEADS, MOBA_HD, MOBA_BLOCK), F32)],
        compiler_params=_cparams("parallel", "arbitrary"),
        name="moba_prompt",
    )(qr, kb, vt, km)


def _moba_sample_kernel(pt_ref, q_ref, kn_ref, vn_ref, ck_hbm, cv_hbm, o_ref,
                        buf, sem, qrow_ref, s_ref, p_ref, km_ref, acc_ref, pown_ref, l_ref,
                        *, layer, n_pages, nb, topn, valid):
    b = pl.program_id(0)
    nseq = pl.num_programs(0)
    gp, ring = MOBA_GROUP_PAGES, MOBA_RING_GROUPS
    ngk = n_pages // gp
    total = 2 * ngk
    look = ring - 1
    sp = q_ref.shape[0]
    hw = MOBA_HEADS * MOBA_HD
    ppb = MOBA_BLOCK // PAGE_SIZE
    nbs = gp // ppb
    scale = MOBA_HD ** -0.5

    def page_copy(src, page, slot, j):
        return pltpu.make_async_copy(src.at[layer, page], buf.at[slot * gp + j], sem.at[slot])

    def start_group(seq, gi):
        src = ck_hbm if gi < ngk else cv_hbm
        for j in range(gp):
            page_copy(src, pt_ref[seq * n_pages + (gi % ngk) * gp + j], gi % ring, j).start()

    def wait_group(gi):
        for j in range(gp):
            page_copy(ck_hbm, 0, gi % ring, j).wait()

    def query_rows():
        qt = jnp.concatenate([q_ref[...]] * (LANES // sp), axis=0)
        rh = lax.broadcasted_iota(jnp.int32, (LANES, hw), 0) // sp
        lh = lax.broadcasted_iota(jnp.int32, (LANES, hw), 1) // MOBA_HD
        return jnp.where(rh == lh, qt, 0.0)

    def page_heads(slot, j):
        return [buf[slot * gp + j, pl.ds(h, PAGE_SIZE, stride=MOBA_HEADS), :] for h in range(MOBA_HEADS)]

    def key_group(gk):
        slot = gk % ring
        sums, kall = [], []
        for j in range(gp):
            xh = page_heads(slot, j)
            kall.append(jnp.concatenate([t.astype(BF16) for t in xh], axis=1))
            sums.append(jnp.concatenate([jnp.sum(t, axis=0, keepdims=True) for t in xh], axis=1))
        keys = gp * PAGE_SIZE
        s_ref[gk * keys:(gk + 1) * keys, :] = _dot_nt(jnp.concatenate(kall, axis=0), qrow_ref[...])
        for jb in range(nbs):
            blk = gk * nbs + jb
            km_ref[blk:blk + 1, :] = sum(sums[jb * ppb:(jb + 1) * ppb]) * (1.0 / MOBA_BLOCK)

    def value_group(gv):
        slot = (ngk + gv) % ring
        keys = gp * PAGE_SIZE
        vall = jnp.concatenate(
            [jnp.concatenate([t.astype(BF16) for t in page_heads(slot, j)], axis=1) for j in range(gp)], axis=0)
        acc_ref[...] += _dot_tn(p_ref[gv * keys:(gv + 1) * keys, :], vall)

    def select_and_weigh():
        b_idx = lax.broadcasted_iota(jnp.int32, (nb, LANES), 0)
        sel = _top_rows(_dot_nt_f32(km_ref[...], query_rows()), b_idx, topn, nb, 0)
        s_own = _dot_nt(kn_ref[...].astype(BF16), qrow_ref[...])
        kt = lax.broadcasted_iota(jnp.int32, (sp, LANES), 0)
        qt = lax.broadcasted_iota(jnp.int32, (sp, LANES), 1) % sp
        s_own = jnp.where(jnp.logical_and(kt <= qt, kt < valid), s_own, NEG_INF)

        def masked(n):
            return jnp.where(sel[n:n + 1, :] > 0.0, s_ref[n * MOBA_BLOCK:(n + 1) * MOBA_BLOCK, :], NEG_INF)

        mx = jnp.max(s_own, axis=0, keepdims=True)
        for n in range(nb):
            mx = jnp.maximum(mx, jnp.max(masked(n), axis=0, keepdims=True))
        p_own = jnp.exp(s_own - mx)
        l = jnp.sum(p_own, axis=0, keepdims=True)
        for n in range(nb):
            pn = jnp.exp(masked(n) - mx)
            l = l + jnp.sum(pn, axis=0, keepdims=True)
            p_ref[n * MOBA_BLOCK:(n + 1) * MOBA_BLOCK, :] = pn.astype(BF16)
        pown_ref[...] = jnp.concatenate([p_own, jnp.zeros((LANES - sp, LANES), F32)], axis=0).astype(BF16)
        l_ref[...] = jnp.broadcast_to(l, l_ref.shape)

    def finish():
        vn = jnp.concatenate([vn_ref[...], jnp.zeros((LANES - sp, hw), F32)], axis=0).astype(BF16)
        acc = acc_ref[...] + _dot_tn(pown_ref[...], vn)
        o = acc / l_ref[...].T[:, :1]
        o_ref[...] = jnp.concatenate(
            [o[h * sp:(h + 1) * sp, h * MOBA_HD:(h + 1) * MOBA_HD] for h in range(MOBA_HEADS)], axis=1
        ).astype(o_ref.dtype)

    @pl.when(b == 0)
    def _():
        for gi in range(look):
            start_group(b, gi)

    qrow_ref[...] = (query_rows() * scale).astype(BF16)
    acc_ref[...] = jnp.zeros_like(acc_ref)

    for gi in range(total):
        ahead = gi + look
        if ahead < total:
            start_group(b, ahead)
        else:
            @pl.when(b + 1 < nseq)
            def _():
                start_group(b + 1, ahead - total)
        wait_group(gi)
        if gi < ngk:
            key_group(gi)
            if gi == ngk - 1:
                select_and_weigh()
        else:
            value_group(gi - ngk)
    finish()


def _moba_sample(qr, kr, v, cache_k, cache_v, layer, page_table, valid):
    m, hw = qr.shape
    cache_k = cache_k.reshape(cache_k.shape[:2] + (PAGE_SIZE * MOBA_HEADS, MOBA_HD))
    cache_v = cache_v.reshape(cache_v.shape[:2] + (PAGE_SIZE * MOBA_HEADS, MOBA_HD))
    db, n_pages = page_table.shape
    sp = m // db
    past = n_pages * PAGE_SIZE
    assert past % MOBA_BLOCK == 0
    nb = past // MOBA_BLOCK
    topn = min(MOBA_TOPK, nb)
    ppb = MOBA_BLOCK // PAGE_SIZE
    gp, ring = MOBA_GROUP_PAGES, MOBA_RING_GROUPS
    assert n_pages % gp == 0 and gp % ppb == 0 and (2 * n_pages // gp) % ring == 0 and nb >= 1
    qmap = lambda b, pt: (b, 0)
    grid_spec = pltpu.PrefetchScalarGridSpec(
        num_scalar_prefetch=1,
        grid=(db,),
        in_specs=[pl.BlockSpec((sp, hw), qmap)] * 3 + [pl.BlockSpec(memory_space=pl.ANY)] * 2,
        out_specs=pl.BlockSpec((sp, hw), qmap),
        scratch_shapes=[pltpu.VMEM((ring * gp, PAGE_SIZE * MOBA_HEADS, MOBA_HD), F32), pltpu.SemaphoreType.DMA((ring,)),
                        pltpu.VMEM((LANES, hw), BF16), pltpu.VMEM((past, LANES), F32), pltpu.VMEM((past, LANES), BF16),
                        pltpu.VMEM((nb, hw), F32), pltpu.VMEM((LANES, hw), F32), pltpu.VMEM((LANES, LANES), BF16),
                        pltpu.VMEM((SUBLANES, LANES), F32)],
    )
    return pl.pallas_call(
        functools.partial(_moba_sample_kernel, layer=layer, n_pages=n_pages, nb=nb, topn=topn, valid=valid),
        grid_spec=grid_spec,
        out_shape=jax.ShapeDtypeStruct((m, hw), BF16),
        compiler_params=_cparams("arbitrary"),
        name="moba_sample",
    )(page_table.reshape(-1), qr, kr, v, cache_k, cache_v)


def _s5_disc_kernel(lr_ref, li_ref, dt_ref, bre_ref, bim_ref, pwr_ref, pwi_ref, bbr_ref, bbi_ref):
    lr, li, dt = lr_ref[...], li_ref[...], dt_ref[...]
    k = (lax.broadcasted_iota(jnp.int32, pwr_ref.shape, 0) + 1).astype(F32)
    mag = jnp.exp(k * (lr * dt))
    ang = k * (li * dt)
    pwr = mag * jnp.cos(ang)
    pwi = mag * jnp.sin(ang)
    pwr_ref[...] = pwr
    pwi_ref[...] = pwi
    nr, ni = pwr[0:1] - 1.0, pwi[0:1]
    den = lr * lr + li * li
    f_re = (nr * lr + ni * li) / den
    f_im = (ni * lr - nr * li) / den
    bre, bim = bre_ref[...], bim_ref[...]
    bbr_ref[...] = f_re * bre - f_im * bim
    bbi_ref[...] = f_re * bim + f_im * bre


def _s5_discretize(lam_re, lam_im, log_dt, b_re, b_im):
    n = S5_N
    flat = lambda a: a.astype(F32).reshape(1, n)
    dt = jnp.exp(jnp.repeat(log_dt.astype(F32), S5_STATE)).reshape(1, n)
    bt = lambda b: jnp.transpose(b.astype(F32), (2, 0, 1)).reshape(S5_GROUP, n)
    args = (flat(lam_re), flat(lam_im), dt, bt(b_re), bt(b_im))
    return pl.pallas_call(
        _s5_disc_kernel,
        in_specs=[_full(a) for a in args],
        out_shape=[jax.ShapeDtypeStruct((SUBLANES, n), F32)] * 2 + [jax.ShapeDtypeStruct((S5_GROUP, n), F32)] * 2,
        name="s5_discretize",
    )(*args)


def _s5_block_diag(pieces):
    r, g, c = pieces.shape
    eye = jnp.eye(g, dtype=pieces.dtype)
    return jnp.einsum('rgc,gh->grhc', pieces, eye).reshape(g * r, g * c)


def _s5_kernel(u_ref, bb_ref, pwr_ref, pwi_ref, cc_ref, d_ref, wg_ref, bg_ref, *rest, chain):
    n = S5_N
    u = u_ref[...]
    tt = u.shape[0]
    ngrp = tt // SUBLANES
    bu = _dot(u.astype(BF16), bb_ref[...])
    xr = bu[:, :n].reshape(ngrp, SUBLANES, n)
    xi = bu[:, n:].reshape(ngrp, SUBLANES, n)
    row = lax.broadcasted_iota(jnp.int32, (ngrp, SUBLANES, n), 1)
    for d in (1, 2, 4):
        pr, pi = pwr_ref[d - 1:d, :], pwi_ref[d - 1:d, :]
        sr, si = pltpu.roll(xr, d, 1), pltpu.roll(xi, d, 1)
        keep = row >= d
        xr, xi = (xr + jnp.where(keep, pr * sr - pi * si, 0.0),
                  xi + jnp.where(keep, pr * si + pi * sr, 0.0))
    pw_r, pw_i = pwr_ref[...], pwi_ref[...]

    if chain:
        o_ref, sr_ref, si_ref, xr_s, xi_s, cr_s, ci_s = rest
        t = pl.program_id(1)

        @pl.when(t == 0)
        def _():
            cr_s[...] = jnp.zeros_like(cr_s)
            ci_s[...] = jnp.zeros_like(ci_s)

        xr_s[...] = xr
        xi_s[...] = xi

        def body(i, carry):
            cr, ci = carry
            a = xr_s[i] + pw_r * cr - pw_i * ci
            b = xi_s[i] + pw_r * ci + pw_i * cr
            xr_s[i] = a
            xi_s[i] = b
            return a[SUBLANES - 1:SUBLANES], b[SUBLANES - 1:SUBLANES]

        cr, ci = lax.fori_loop(0, ngrp, body, (cr_s[...], ci_s[...]))
        cr_s[...] = cr
        ci_s[...] = ci
        xr, xi = xr_s[...], xi_s[...]

        @pl.when(t == pl.num_programs(1) - 1)
        def _():
            sr_ref[...] = cr
            si_ref[...] = ci
    else:
        s0r_ref, s0i_ref, o_ref, xr_ref, xi_ref = rest
        s0r = s0r_ref[...].reshape(ngrp, SUBLANES, n)
        s0i = s0i_ref[...].reshape(ngrp, SUBLANES, n)
        xr, xi = xr + pw_r * s0r - pw_i * s0i, xi + pw_r * s0i + pw_i * s0r
        xr_ref[...] = xr.reshape(tt, n)
        xi_ref[...] = xi.reshape(tt, n)

    xcat = jnp.concatenate([xr.reshape(tt, n), xi.reshape(tt, n)], axis=1).astype(BF16)
    y = _dot(xcat, cc_ref[...]) + d_ref[...] * u
    y = y * (0.5 * (1.0 + jnp.tanh(math.sqrt(2.0 / math.pi) * (y + 0.044715 * (y * y * y)))))
    z = _dot(y.astype(BF16), wg_ref[...]) + bg_ref[...]
    o_ref[...] = (y * _sigmoid(z)).astype(o_ref.dtype)


def _s5_prompt(u, bb, pwr, pwi, cc, d, wg, bg, seq):
    m, w = u.shape
    nbatch = m // seq
    tt = _row_tile(seq, 256)
    nt = seq // tt
    n = S5_N
    row = pl.BlockSpec((tt, w), lambda b, t: (b * nt + t, 0))
    st = pl.BlockSpec((None, 1, n), lambda b, t: (b, 0, 0))
    return pl.pallas_call(
        functools.partial(_s5_kernel, chain=True),
        grid=(nbatch, nt),
        in_specs=[row, _full(bb), _full(pwr), _full(pwi), _full(cc), _full(d), _full(wg), _full(bg)],
        out_specs=[row, st, st],
        out_shape=[jax.ShapeDtypeStruct((m, w), BF16), jax.ShapeDtypeStruct((nbatch, 1, n), F32),
                   jax.ShapeDtypeStruct((nbatch, 1, n), F32)],
        scratch_shapes=[pltpu.VMEM((tt // SUBLANES, SUBLANES, n), F32), pltpu.VMEM((tt // SUBLANES, SUBLANES, n), F32),
                        pltpu.VMEM((1, n), F32), pltpu.VMEM((1, n), F32)],
        compiler_params=_cparams("parallel", "arbitrary"),
        name="s5_prompt",
    )(u, bb, pwr, pwi, cc, d, wg, bg)


def _s5_sample(u, bb, pwr, pwi, cc, d, wg, bg, s0r, s0i):
    m, w = u.shape
    n = S5_N
    tt = _row_tile(m, 256)
    row = pl.BlockSpec((tt, w), lambda i: (i, 0))
    st = pl.BlockSpec((tt, n), lambda i: (i, 0))
    return pl.pallas_call(
        functools.partial(_s5_kernel, chain=False),
        grid=(m // tt,),
        in_specs=[row, _full(bb), _full(pwr), _full(pwi), _full(cc), _full(d), _full(wg), _full(bg), st, st],
        out_specs=[row, st, st],
        out_shape=[jax.ShapeDtypeStruct((m, w), BF16), jax.ShapeDtypeStruct((m, n), F32),
                   jax.ShapeDtypeStruct((m, n), F32)],
        compiler_params=_cparams("parallel"),
        name="s5_sample",
    )(u, bb, pwr, pwi, cc, d, wg, bg, s0r, s0i)


def kernel(x_prompt, x_sample, cache_mem_k, cache_mem_v, state_gla, cache_mla_ckv, cache_mla_kpe, cache_moba_k, cache_moba_v, state_s5, page_table, mem_prompt, g_ffn1, w_ffn1_gate, w_ffn1_up, w_ffn1_down, g_mix, w_in_a, w_gla_gate, b_gla_gate, g_gla_norm, g_mla_q, w_mla_qb, g_mla_kv, w_mla_kvb, w_out_a, w_in_b, s5_lam_re, s5_lam_im, s5_log_dt, s5_b_re, s5_b_im, s5_c_re, s5_c_im, s5_d, w_s5_glu, b_s5_glu, w_out_b, g_xattn, w_xq, w_xk, w_xv, w_xo, g_ffn2, w_ffn2_gate, w_ffn2_up, w_ffn2_down, g_final):
    B, T, D = x_prompt.shape
    DB, S, _ = x_sample.shape
    depth = g_ffn1.shape[0]
    n_pages = page_table.shape[1]
    past = n_pages * PAGE_SIZE
    SP = SAMPLE_PAD
    assert S <= SP
    n_mem = mem_prompt.shape[1]
    bf = lambda a: a.astype(BF16)

    xp = x_prompt.reshape(B * T, D)
    xs = jnp.pad(x_sample, ((0, 0), (0, SP - S), (0, 0))).reshape(DB * SP, D)
    pos_p = jnp.arange(T, dtype=jnp.int32)
    pos_s = jnp.tile(past + jnp.arange(SP, dtype=jnp.int32), DB)
    mla_tab_p = _rope_tables(pos_p, MLA_ROPE, LANES, MLA_HEADS)
    mla_tab_s = _rope_tables(pos_s, MLA_ROPE, LANES, MLA_HEADS)
    moba_tab_p = _rope_tables(pos_p, MOBA_HD, MOBA_HD, MOBA_HEADS)
    moba_tab_s = _rope_tables(pos_s, MOBA_HD, MOBA_HD, MOBA_HEADS)
    mem2d = mem_prompt.reshape(B * n_mem, D)
    cache_kpet = jnp.swapaxes(cache_mla_kpe, 2, 3)
    ones_d = jnp.ones((D,), F32)
    gla_chunk = math.gcd(T, GLA_CHUNK)

    def unpad(a):
        return a.reshape((DB, SP) + a.shape[1:])[:, :S]

    mem_k_p, mem_v_p = [], []
    gla_p, gla_s = [], []
    ckv_p, kpe_p, ckv_s, kpe_s = [], [], [], []
    mk_p, mv_p, mk_s, mv_s = [], [], [], []
    s5_p, s5_s = [], []

    for l in range(depth):
        i = l // 2
        wg1, wu1, wd1 = bf(w_ffn1_gate[l]), bf(w_ffn1_up[l]), bf(w_ffn1_down[l])
        xp = _ffn(xp, g_ffn1[l], wg1, wu1, wd1)
        xs = _ffn(xs, g_ffn1[l], wg1, wu1, wd1)

        if l % 2 == 0:
            qa_w, ka_w, va_w, ga_w, ra_w, ql_w, kvl_w, kr_w = jnp.split(
                w_in_a[i], [256, 512, 1024, 1040, 1552, 1936, 2192], axis=1)
            padc = lambda w: jnp.pad(w, ((0, 0), (0, LANES - w.shape[1])))
            w_in = bf(jnp.concatenate([va_w, ra_w, qa_w, ka_w, kvl_w, ql_w, padc(kr_w), padc(ga_w)], axis=1))
            widths = (512, 512, 256, 256, 256, 384, LANES, LANES)
            wgate = bf(jnp.pad(w_gla_gate[i], ((0, LANES - GLA_GATE_RANK), (0, 0))))
            bgate = b_gla_gate[i].reshape(1, -1)
            gnorm = g_gla_norm[i].reshape(1, -1)
            wqb = w_mla_qb[i].reshape(MLA_Q_LORA, MLA_HEADS, MLA_NOPE + MLA_ROPE)
            wqn = bf(wqb[:, :, :MLA_NOPE].reshape(MLA_Q_LORA, MLA_HEADS * MLA_NOPE))
            wqp = bf(jnp.pad(wqb[:, :, MLA_NOPE:], ((0, 0), (0, 0), (0, LANES - MLA_ROPE))).reshape(MLA_Q_LORA, MLA_HEADS * LANES))
            wkvb = w_mla_kvb[i].reshape(MLA_KV_LORA, MLA_HEADS, MLA_NOPE + MLA_V)
            wuk = bf(jnp.transpose(wkvb[:, :, :MLA_NOPE], (1, 2, 0)))
            wuv = bf(jnp.transpose(wkvb[:, :, MLA_NOPE:], (1, 0, 2)))
            wuvt = bf(jnp.transpose(wkvb[:, :, MLA_NOPE:], (1, 2, 0)))
            gq = g_mla_q[i].reshape(1, -1)
            gkv = g_mla_kv[i].reshape(1, -1)
            wo = bf(w_out_a[i])
            wo_a, wo_b = wo[:GLA_HEADS * GLA_DV], wo[GLA_HEADS * GLA_DV:]

            def even(x, tabs, s0, prompt):
                va, ra, qa, ka, kvl, ql, ksl, gsl = _proj(x, g_mix[l], w_in, widths)
                if prompt:
                    o_a, st = _gla(qa, ka, va, gsl, ra, s0, wgate, bgate, gnorm,
                                   seq=T, rows=gla_chunk, chunk=gla_chunk, valid=gla_chunk)
                else:
                    o_a, st = _gla(qa, ka, va, gsl, ra, s0, wgate, bgate, gnorm,
                                   seq=SP, rows=SP, chunk=2 * SP, valid=S)
                if prompt:
                    q_t, k_mla, ckv, kpe, c_t = _mla_prep(ql, kvl, ksl, tabs[0], tabs[1], gq, gkv, wqn, wqp, wuk, True)
                    o_b = _mla_prompt(q_t, k_mla, c_t, wuvt, T)
                else:
                    q_mla, k_mla, ckv, kpe = _mla_prep(ql, kvl, ksl, tabs[0], tabs[1], gq, gkv, wqn, wqp, wuk, False)
                    o_b = _mla_sample(q_mla, k_mla, wuv, cache_mla_ckv, cache_kpet, i, page_table, S)
                return _outproj(o_a, o_b, wo_a, wo_b, x), st, ckv, kpe

            xp, st, ckv, kpe = even(xp, mla_tab_p, jnp.zeros((B, GLA_HEADS, GLA_DK, GLA_DV), F32), True)
            gla_p.append(st)
            ckv_p.append(ckv.reshape(B, T, MLA_KV_LORA))
            kpe_p.append(kpe.reshape(B, T, MLA_ROPE))
            xs, st, ckv, kpe = even(xs, mla_tab_s, state_gla[i], False)
            gla_s.append(st)
            ckv_s.append(unpad(ckv))
            kpe_s.append(unpad(kpe))
        else:
            hw = MOBA_HEADS * MOBA_HD
            w_in = bf(w_in_b[i])
            widths = (hw, hw, hw, S5_WIDTH)
            pwr, pwi, bbr, bbi = _s5_discretize(s5_lam_re[i], s5_lam_im[i], s5_log_dt[i], s5_b_re[i], s5_b_im[i])
            to_blocks = lambda a: a.reshape(a.shape[0], S5_GROUPS, S5_STATE)
            bb = bf(jnp.concatenate([_s5_block_diag(to_blocks(bbr)), _s5_block_diag(to_blocks(bbi))], axis=1))
            c_t = lambda c: jnp.transpose(c.astype(F32), (2, 0, 1))
            cc = bf(jnp.concatenate([_s5_block_diag(c_t(s5_c_re[i])), -_s5_block_diag(c_t(s5_c_im[i]))], axis=0))
            dsk = s5_d[i].reshape(1, -1)
            wglu = bf(w_s5_glu[i])
            bglu = b_s5_glu[i].reshape(1, -1)
            wo = bf(w_out_b[i])
            wo_c, wo_d = wo[:hw], wo[hw:]

            q, k, v, u = _proj(xp, g_mix[l], w_in, widths)
            qr, kr, kb, vt, km = _moba_prep(q, k, v, moba_tab_p[0], moba_tab_p[1], True)
            o_c = _moba_prompt(qr, kb, vt, km.reshape(B, T // MOBA_BLOCK, hw), T)
            o_d, sr, si = _s5_prompt(u, bb, pwr, pwi, cc, dsk, wglu, bglu, T)
            xp = _outproj(o_c, o_d, wo_c, wo_d, xp)
            mk_p.append(kr.reshape(B, T, MOBA_HEADS, MOBA_HD))
            mv_p.append(v.reshape(B, T, MOBA_HEADS, MOBA_HD))
            s5_p.append(jnp.stack([sr.reshape(B, S5_GROUPS, S5_STATE), si.reshape(B, S5_GROUPS, S5_STATE)], axis=-1))

            q, k, v, u = _proj(xs, g_mix[l], w_in, widths)
            qr, kr = _moba_prep(q, k, v, moba_tab_s[0], moba_tab_s[1], False)
            o_c = _moba_sample(qr, kr, v, cache_moba_k, cache_moba_v, i, page_table, S)
            s0 = state_s5[i].astype(F32).reshape(DB, S5_N, 2)
            s0r = jnp.repeat(s0[..., 0], SP, axis=0)
            s0i = jnp.repeat(s0[..., 1], SP, axis=0)
            o_d, xr, xi = _s5_sample(u, bb, pwr, pwi, cc, dsk, wglu, bglu, s0r, s0i)
            xs = _outproj(o_c, o_d, wo_c, wo_d, xs)
            mk_s.append(unpad(kr).reshape(DB, S, MOBA_HEADS, MOBA_HD))
            mv_s.append(unpad(v).reshape(DB, S, MOBA_HEADS, MOBA_HD))
            last = lambda a: a.reshape(DB, SP, S5_GROUPS, S5_STATE)[:, S - 1]
            s5_s.append(jnp.stack([last(xr), last(xi)], axis=-1))

        mkp, mvp = _proj(mem2d, ones_d, bf(jnp.concatenate([w_xk[l], w_xv[l]], axis=1)), (D, D), norm=False)
        mem_k_p.append(mkp.reshape(B, n_mem, X_HEADS, D // X_HEADS))
        mem_v_p.append(mvp.reshape(B, n_mem, X_HEADS, D // X_HEADS))
        wq, wo = bf(w_xq[l]), bf(w_xo[l])
        xp = _xattn_prompt(xp, g_xattn[l], wq, wo, mkp.reshape(B, n_mem, D), mvp.reshape(B, n_mem, D), T)
        xs = _xattn_sample(xs, g_xattn[l], wq, wo, cache_mem_k, cache_mem_v, l)

        wg2, wu2, wd2 = bf(w_ffn2_gate[l]), bf(w_ffn2_up[l]), bf(w_ffn2_down[l])
        xp = _ffn(xp, g_ffn2[l], wg2, wu2, wd2)
        xs = _ffn(xs, g_ffn2[l], wg2, wu2, wd2)

    y_prompt = _final_norm(xp, g_final).reshape(B, T, D)
    y_sample = unpad(_final_norm(xs, g_final))
    return (y_prompt, y_sample,
            jnp.stack(mem_k_p), jnp.stack(mem_v_p),
            jnp.stack(gla_p), jnp.stack(gla_s),
            jnp.stack(ckv_p), jnp.stack(kpe_p), jnp.stack(ckv_s), jnp.stack(kpe_s),
            jnp.stack(mk_p), jnp.stack(mv_p), jnp.stack(mk_s), jnp.stack(mv_s),
            jnp.stack(s5_p), jnp.stack(s5_s))
```

```python
import functools
import math

import jax
import jax.numpy as jnp
from jax import lax
from jax.experimental import pallas as pl
from jax.experimental.pallas import tpu as pltpu

F32 = jnp.float32
BF16 = jnp.bfloat16

RMS_EPS = 1e-6
ROPE_THETA = 10000.0
NEG_INF = -1e30
BELOW_NEG_INF = -3.0e38
PAGE_SIZE = 128
GLA_HEADS, GLA_DK, GLA_DV, GLA_GATE_RANK, GLA_TAU, GLA_CHUNK = 4, 64, 128, 16, 16.0, 64
MLA_HEADS, MLA_Q_LORA, MLA_KV_LORA, MLA_NOPE, MLA_ROPE, MLA_V = 4, 384, 256, 128, 64, 128
MOBA_HEADS, MOBA_HD, MOBA_BLOCK, MOBA_TOPK = 4, 128, 256, 3
S5_WIDTH, S5_GROUP, S5_STATE = 512, 16, 64
S5_GROUPS = S5_WIDTH // S5_GROUP
S5_N = S5_GROUPS * S5_STATE
S5_TILE = 256
X_HEADS = 4
LANES = 128
SUBLANES = 8
SAMPLE_PAD = SUBLANES
VMEM_LIMIT = 56 * 1024 * 1024
MLA_GROUP_PAGES = 16
MLA_RING_GROUPS = 4
MOBA_GROUP_PAGES = 8
MOBA_RING_GROUPS = 4
LOG2E = math.log2(math.e)


def _cparams(*sem):
    return pltpu.CompilerParams(dimension_semantics=sem, vmem_limit_bytes=VMEM_LIMIT)


def _row_tile(m, target):
    t = min(m, target)
    while m % t or t % SUBLANES:
        t -= 1
    return t


def _rms(x, g):
    return x * lax.rsqrt(jnp.mean(x * x, axis=-1, keepdims=True) + RMS_EPS) * g


def _sigmoid(x):
    return 1.0 / (1.0 + jnp.exp(-x))


def _dot(a, b):
    return jnp.dot(a, b, preferred_element_type=F32)


def _dot_nt(a, b):
    return lax.dot_general(a, b, (((1,), (1,)), ((), ())), preferred_element_type=F32)


def _dot_nt_f32(a, b):
    return lax.dot_general(a, b, (((1,), (1,)), ((), ())), precision=lax.Precision.HIGHEST,
                           preferred_element_type=F32)


def _dot_tn(a, b):
    return lax.dot_general(a, b, (((0,), (0,)), ((), ())), preferred_element_type=F32)


def _full(a):
    return pl.BlockSpec(a.shape, lambda *_: (0,) * a.ndim)


def _ffn_kernel(x_ref, g_ref, wg_ref, wu_ref, wd_ref, o_ref, z_ref, *, tf):
    x = x_ref[...]
    h = _rms(x, g_ref[...]).astype(BF16)
    for c in range(wg_ref.shape[1] // tf):
        cs = slice(c * tf, (c + 1) * tf)
        a = _dot(h, wg_ref[:, cs])
        z_ref[:, cs] = (a * _sigmoid(a) * _dot(h, wu_ref[:, cs])).astype(BF16)
    o_ref[...] = x + 0.5 * _dot(z_ref[...], wd_ref[...])


def _ffn(x, g, wg, wu, wd):
    m, d = x.shape
    f = wg.shape[1]
    tm = _row_tile(m, 512)
    tf = 256 if f % 256 == 0 else f
    once = pl.Buffered(1)
    return pl.pallas_call(
        functools.partial(_ffn_kernel, tf=tf),
        grid=(m // tm,),
        in_specs=[
            pl.BlockSpec((tm, d), lambda i: (i, 0)),
            pl.BlockSpec((1, d), lambda i: (0, 0)),
            pl.BlockSpec((d, f), lambda i: (0, 0), pipeline_mode=once),
            pl.BlockSpec((d, f), lambda i: (0, 0), pipeline_mode=once),
            pl.BlockSpec((f, d), lambda i: (0, 0), pipeline_mode=once),
        ],
        out_specs=pl.BlockSpec((tm, d), lambda i: (i, 0)),
        out_shape=jax.ShapeDtypeStruct((m, d), F32),
        scratch_shapes=[pltpu.VMEM((tm, f), BF16)],
        compiler_params=_cparams("parallel"),
        name="ffn",
    )(x, g.reshape(1, d), wg, wu, wd)


def _proj_kernel(x_ref, g_ref, w_ref, *o_refs, widths, norm):
    x = x_ref[...]
    h = _rms(x, g_ref[...]) if norm else x
    y = _dot(h.astype(BF16), w_ref[...])
    off = 0
    for o_ref, wd in zip(o_refs, widths):
        o_ref[...] = y[:, off:off + wd].astype(o_ref.dtype)
        off += wd


def _proj(x, g, w, widths, norm=True):
    m, d = x.shape
    n = w.shape[1]
    assert sum(widths) == n
    tm = _row_tile(m, 512)
    return pl.pallas_call(
        functools.partial(_proj_kernel, widths=widths, norm=norm),
        grid=(m // tm,),
        in_specs=[pl.BlockSpec((tm, d), lambda i: (i, 0)), pl.BlockSpec((1, d), lambda i: (0, 0)), _full(w)],
        out_specs=[pl.BlockSpec((tm, wd), lambda i: (i, 0)) for wd in widths],
        out_shape=[jax.ShapeDtypeStruct((m, wd), F32) for wd in widths],
        compiler_params=_cparams("parallel"),
        name="proj",
    )(x, g.reshape(1, d), w)


def _outproj_kernel(a_ref, b_ref, wa_ref, wb_ref, x_ref, o_ref):
    o_ref[...] = x_ref[...] + _dot(a_ref[...], wa_ref[...]) + _dot(b_ref[...], wb_ref[...])


def _outproj(a, b, wa, wb, x):
    m, d = x.shape
    tm = _row_tile(m, 512)
    return pl.pallas_call(
        _outproj_kernel,
        grid=(m // tm,),
        in_specs=[
            pl.BlockSpec((tm, a.shape[1]), lambda i: (i, 0)),
            pl.BlockSpec((tm, b.shape[1]), lambda i: (i, 0)),
            _full(wa), _full(wb),
            pl.BlockSpec((tm, d), lambda i: (i, 0)),
        ],
        out_specs=pl.BlockSpec((tm, d), lambda i: (i, 0)),
        out_shape=jax.ShapeDtypeStruct((m, d), F32),
        compiler_params=_cparams("parallel"),
        name="outproj",
    )(a, b, wa, wb, x)


def _norm_kernel(x_ref, g_ref, o_ref):
    o_ref[...] = _rms(x_ref[...], g_ref[...])


def _final_norm(x, g):
    m, d = x.shape
    tm = _row_tile(m, 1024)
    return pl.pallas_call(
        _norm_kernel,
        grid=(m // tm,),
        in_specs=[pl.BlockSpec((tm, d), lambda i: (i, 0)), pl.BlockSpec((1, d), lambda i: (0, 0))],
        out_specs=pl.BlockSpec((tm, d), lambda i: (i, 0)),
        out_shape=jax.ShapeDtypeStruct((m, d), F32),
        compiler_params=_cparams("parallel"),
        name="final_norm",
    )(x, g.reshape(1, d))


def _xattn_core(x, g, wq, wo, mem_k, mem_v, seq_rows):
    d = x.shape[1]
    hd = d // X_HEADS
    q = (_dot(_rms(x, g).astype(BF16), wq) * (hd ** -0.5)).astype(BF16)
    nseq = x.shape[0] // seq_rows
    pairs = [(s, h) for s in range(nseq) for h in range(X_HEADS)]
    sc = [_dot_nt(q[s * seq_rows:(s + 1) * seq_rows, h * hd:(h + 1) * hd], mem_k(s, h)) for s, h in pairs]
    p = [jnp.exp(t - jnp.max(t, axis=-1, keepdims=True)) for t in sc]
    p = [t / jnp.sum(t, axis=-1, keepdims=True) for t in p]
    pv = [_dot(t.astype(BF16), mem_v(s, h)) for t, (s, h) in zip(p, pairs)]
    outs = [jnp.concatenate(pv[s * X_HEADS:(s + 1) * X_HEADS], axis=-1) for s in range(nseq)]
    o = outs[0] if nseq == 1 else jnp.concatenate(outs, axis=0)
    return x + _dot(o.astype(BF16), wo)


def _xattn_prompt_kernel(x_ref, g_ref, wq_ref, wo_ref, mk_ref, mv_ref, o_ref):
    x = x_ref[...]
    hd = x.shape[1] // X_HEADS
    mk = mk_ref[...].astype(BF16)
    mv = mv_ref[...].astype(BF16)
    o_ref[...] = _xattn_core(x, g_ref[...], wq_ref[...], wo_ref[...],
                             lambda s, h: mk[:, h * hd:(h + 1) * hd], lambda s, h: mv[:, h * hd:(h + 1) * hd],
                             x.shape[0])


def _xattn_prompt(x, g, wq, wo, mk, mv, seq):
    m, d = x.shape
    nm = mk.shape[1]
    tm = _row_tile(seq, 512)
    nt = seq // tm
    return pl.pallas_call(
        _xattn_prompt_kernel,
        grid=(m // tm,),
        in_specs=[
            pl.BlockSpec((tm, d), lambda i: (i, 0)),
            pl.BlockSpec((1, d), lambda i: (0, 0)),
            _full(wq), _full(wo),
            pl.BlockSpec((None, nm, d), lambda i: (i // nt, 0, 0)),
            pl.BlockSpec((None, nm, d), lambda i: (i // nt, 0, 0)),
        ],
        out_specs=pl.BlockSpec((tm, d), lambda i: (i, 0)),
        out_shape=jax.ShapeDtypeStruct((m, d), F32),
        compiler_params=_cparams("parallel"),
        name="xattn_prompt",
    )(x, g.reshape(1, d), wq, wo, mk, mv)


def _xattn_sample_kernel(x_ref, g_ref, wq_ref, wo_ref, mk_ref, mv_ref, o_ref, *, nseq):
    nc = x_ref.shape[1] // X_HEADS // LANES
    nm = mk_ref.shape[1] // (X_HEADS * nc)

    def head(ref, s, h):
        tiles = [ref[s, pl.ds(h + X_HEADS * c, nm, stride=X_HEADS * nc), :] for c in range(nc)]
        return jnp.concatenate(tiles, axis=1).astype(BF16)

    o_ref[...] = _xattn_core(x_ref[...], g_ref[...], wq_ref[...], wo_ref[...],
                             lambda s, h: head(mk_ref, s, h), lambda s, h: head(mv_ref, s, h),
                             x_ref.shape[0] // nseq)


def _xattn_sample(x, g, wq, wo, cache_k, cache_v, layer):
    m, d = x.shape
    depth, db, nm, nh, hd = cache_k.shape
    assert nh == X_HEADS and hd % LANES == 0
    nc = hd // LANES

    def rows_view(c):
        c = c.reshape(depth, db, nm, nh, nc, LANES)
        return jnp.transpose(c, (0, 1, 2, 4, 3, 5)).reshape(depth, db, nm * nc * nh, LANES)

    cache_k, cache_v = rows_view(cache_k), rows_view(cache_v)
    nseq = 4 if db % 4 == 0 else 1
    tm = nseq * (m // db)
    cache_spec = pl.BlockSpec((None, nseq, nm * nc * nh, LANES), lambda i: (layer, i, 0, 0))
    return pl.pallas_call(
        functools.partial(_xattn_sample_kernel, nseq=nseq),
        grid=(db // nseq,),
        in_specs=[
            pl.BlockSpec((tm, d), lambda i: (i, 0)),
            pl.BlockSpec((1, d), lambda i: (0, 0)),
            _full(wq), _full(wo), cache_spec, cache_spec,
        ],
        out_specs=pl.BlockSpec((tm, d), lambda i: (i, 0)),
        out_shape=jax.ShapeDtypeStruct((m, d), F32),
        compiler_params=_cparams("parallel"),
        name="xattn_sample",
    )(x, g.reshape(1, d), wq, wo, cache_k, cache_v)


def _split3(x):
    hi = x.astype(BF16)
    r = x - hi.astype(F32)
    mid = r.astype(BF16)
    lo = (r - mid.astype(F32)).astype(BF16)
    return hi, mid, lo


def _gla_kernel(q_ref, k_ref, v_ref, gs_ref, ra_ref, s0_ref, wg_ref, bg_ref, gn_ref, o_ref, sout_ref, st_ref,
                *, rows, chunk, nchunks, valid):
    t = pl.program_id(1)

    @pl.when(t == 0)
    def _():
        st_ref[...] = s0_ref[...]

    L = chunk
    r_io = lax.broadcasted_iota(jnp.int32, (L, L), 0)
    c_io = lax.broadcasted_iota(jnp.int32, (L, L), 1)
    tril = r_io >= c_io
    tril_bf = jnp.where(tril, 1.0, 0.0).astype(BF16)
    ones_bf = jnp.ones((L, GLA_DV), BF16)
    live = lax.broadcasted_iota(jnp.int32, (L, 1), 0) < valid

    def load(ref, r0):
        x = ref[pl.ds(r0, rows), :]
        if rows < L:
            x = jnp.concatenate([x, jnp.zeros((L - rows, x.shape[1]), x.dtype)], axis=0)
        return x

    def body(c, carry):
        r0 = pl.multiple_of(c * rows, rows)
        q = load(q_ref, r0) * (GLA_DK ** -0.5)
        k = load(k_ref, r0)
        v = load(v_ref, r0)
        ra = load(ra_ref, r0)
        z = _dot(load(gs_ref, r0).astype(BF16), wg_ref[...]) + bg_ref[...]
        lg = (jnp.minimum(z, 0.0) - jnp.log1p(jnp.exp(-jnp.abs(z)))) * (1.0 / GLA_TAU)
        if valid < L:
            lg = jnp.where(live, lg, 0.0)
            k = jnp.where(live, k, 0.0)
            v = jnp.where(live, v, 0.0)
        hi, mid, lo = _split3(lg)
        b = _dot(tril_bf, hi) + _dot(tril_bf, mid) + _dot(tril_bf, lo)
        bl = b[L - 1:L, :]
        qe = (q * jnp.exp(b)).astype(BF16)
        ke = (k * jnp.exp(-b)).astype(BF16)
        kd = (k * jnp.exp(bl - b)).astype(BF16)
        vb = v.astype(BF16)
        heads = range(GLA_HEADS)
        ks = [slice(h * GLA_DK, (h + 1) * GLA_DK) for h in heads]
        vs = [slice(h * GLA_DV, (h + 1) * GLA_DV) for h in heads]
        att = [jnp.where(tril, _dot_nt(qe[:, ks[h]], ke[:, ks[h]]), 0.0).astype(BF16) for h in heads]
        s_old = [st_ref[h] for h in heads]
        o_h = [_dot(qe[:, ks[h]], s_old[h].astype(BF16)) + _dot(att[h], vb[:, vs[h]]) for h in heads]
        dec = [_dot_tn(hi[:, ks[h]], ones_bf) + _dot_tn(mid[:, ks[h]], ones_bf) + _dot_tn(lo[:, ks[h]], ones_bf)
               for h in heads]
        upd = [_dot_tn(kd[:, ks[h]], vb[:, vs[h]]) for h in heads]
        for h in heads:
            st_ref[h] = s_old[h] * jnp.exp(dec[h]) + upd[h]
        gate = ra * _sigmoid(ra)
        o = jnp.concatenate([_rms(o_h[h], gn_ref[...]) for h in heads], axis=-1) * gate
        o_ref[pl.ds(r0, rows), :] = o[:rows].astype(o_ref.dtype)
        return carry

    if nchunks == 1:
        body(0, 0)
    else:
        lax.fori_loop(0, nchunks, body, 0)

    @pl.when(t == pl.num_programs(1) - 1)
    def _():
        sout_ref[...] = st_ref[...]


def _gla(q, k, v, gs, ra, s0, wg, bg, gn, *, seq, rows, chunk, valid):
    m = q.shape[0]
    nb = m // seq
    tt = _row_tile(seq, 512) if rows == chunk else rows
    nchunks = tt // rows
    nt = seq // tt
    row_spec = lambda w: pl.BlockSpec((tt, w), lambda b, t: (b * nt + t, 0))
    st_spec = pl.BlockSpec((None, GLA_HEADS, GLA_DK, GLA_DV), lambda b, t: (b, 0, 0, 0))
    return pl.pallas_call(
        functools.partial(_gla_kernel, rows=rows, chunk=chunk, nchunks=nchunks, valid=valid),
        grid=(nb, nt),
        in_specs=[row_spec(q.shape[1]), row_spec(k.shape[1]), row_spec(v.shape[1]), row_spec(gs.shape[1]),
                  row_spec(ra.shape[1]), st_spec, _full(wg), _full(bg), _full(gn)],
        out_specs=[row_spec(v.shape[1]), st_spec],
        out_shape=[jax.ShapeDtypeStruct((m, v.shape[1]), BF16),
                   jax.ShapeDtypeStruct((nb, GLA_HEADS, GLA_DK, GLA_DV), F32)],
        scratch_shapes=[pltpu.VMEM((GLA_HEADS, GLA_DK, GLA_DV), F32)],
        compiler_params=_cparams("parallel", "arbitrary"),
        name="gla",
    )(q, k, v, gs, ra, s0, wg, bg, gn)


def _rope(x, cos, sin_signed, group):
    w = x.shape[-1]
    ax = x.ndim - 1
    lane = lax.broadcasted_iota(jnp.int32, x.shape, ax)
    first = (lane % group) < (group // 2)
    partner = jnp.where(first, pltpu.roll(x, w - group // 2, ax), pltpu.roll(x, group // 2, ax))
    return x * cos + partner * sin_signed


def _rope_tables(pos, dim, width, reps):
    half = dim // 2
    inv = ROPE_THETA ** (-jnp.arange(half, dtype=F32) / half)
    ang = pos.astype(F32)[:, None] * inv[None, :]
    cos, sin = jnp.cos(ang), jnp.sin(ang)
    pad = jnp.zeros((pos.shape[0], width - dim), F32)
    c = jnp.concatenate([cos, cos, pad], axis=-1)
    s = jnp.concatenate([-sin, sin, pad], axis=-1)
    return jnp.tile(c, (1, reps)), jnp.tile(s, (1, reps))


MLA_QK = MLA_KV_LORA + LANES


def _mla_prep_kernel(ql_ref, kvl_ref, ks_ref, cos_ref, sin_ref, gq_ref, gkv_ref, wqn_ref, wqp_ref, wuk_ref,
                     q_ref, k_ref, ckv_ref, kpe_ref, *ct_ref, transposed):
    scale = (MLA_NOPE + MLA_ROPE) ** -0.5 * LOG2E
    hq = _rms(ql_ref[...], gq_ref[...]).astype(BF16)
    qn = _dot(hq, wqn_ref[...]).astype(BF16)
    cos, sin = cos_ref[...], sin_ref[...]
    qp = _rope(_dot(hq, wqp_ref[...]), cos, sin, MLA_ROPE)
    parts = []
    for h in range(MLA_HEADS):
        parts.append(_dot(qn[:, h * MLA_NOPE:(h + 1) * MLA_NOPE], wuk_ref[h]))
        parts.append(qp[:, h * LANES:(h + 1) * LANES])
    q = jnp.concatenate(parts, axis=-1) * scale
    ckv = _rms(kvl_ref[...], gkv_ref[...])
    ckv_ref[...] = ckv
    kpe = _rope(ks_ref[...], cos[:, :LANES], sin[:, :LANES], MLA_ROPE)
    kpe_ref[...] = kpe[:, :MLA_ROPE]
    k_ref[...] = jnp.concatenate([ckv, kpe], axis=-1).astype(BF16)
    if transposed:
        q_ref[...] = q.T.astype(BF16)
        ct_ref[0][...] = ckv.T.astype(BF16)
    else:
        q_ref[...] = q.astype(BF16)


def _mla_prep(q_lat, kv_lat, ks, cos, sin, gq, gkv, wqn, wqp, wuk, transposed):
    m = q_lat.shape[0]
    p = cos.shape[0]
    tm = _row_tile(math.gcd(m, p), 512)
    npos = p // tm
    row = lambda w: pl.BlockSpec((tm, w), lambda i: (i, 0))
    col = lambda w: pl.BlockSpec((w, tm), lambda i: (0, i))
    pos = lambda w: pl.BlockSpec((tm, w), lambda i: (i % npos, 0))
    hq, hp = MLA_HEADS * MLA_QK, MLA_HEADS * LANES
    out_specs = [col(hq) if transposed else row(hq), row(MLA_QK), row(MLA_KV_LORA), row(MLA_ROPE)]
    out_shape = [jax.ShapeDtypeStruct((hq, m) if transposed else (m, hq), BF16),
                 jax.ShapeDtypeStruct((m, MLA_QK), BF16),
                 jax.ShapeDtypeStruct((m, MLA_KV_LORA), F32), jax.ShapeDtypeStruct((m, MLA_ROPE), F32)]
    if transposed:
        out_specs.append(col(MLA_KV_LORA))
        out_shape.append(jax.ShapeDtypeStruct((MLA_KV_LORA, m), BF16))
    return pl.pallas_call(
        functools.partial(_mla_prep_kernel, transposed=transposed),
        grid=(m // tm,),
        in_specs=[row(MLA_Q_LORA), row(MLA_KV_LORA), row(LANES), pos(hp), pos(hp),
                  _full(gq), _full(gkv), _full(wqn), _full(wqp), _full(wuk)],
        out_specs=out_specs,
        out_shape=out_shape,
        compiler_params=_cparams("parallel"),
        name="mla_prep",
    )(q_lat, kv_lat, ks, cos, sin, gq, gkv, wqn, wqp, wuk)


def _stack_heads(x, width):
    return jnp.concatenate([x[:, h * width:(h + 1) * width] for h in range(MLA_HEADS)], axis=0)


def _softmax_update(s, v, m_ref, l_ref, acc_ref):
    m_prev = m_ref[...]
    m_new = jnp.maximum(m_prev, jnp.max(s, axis=-1, keepdims=True))
    alpha = jnp.exp2(m_prev - m_new)
    p = jnp.exp2(s - m_new)
    l_ref[...] = alpha * l_ref[...] + jnp.sum(p, axis=-1, keepdims=True)
    acc_ref[...] = alpha * acc_ref[...] + _dot(p.astype(BF16), v)
    m_ref[...] = m_new


def _softmax_init(m_ref, l_ref, acc_ref):
    m_ref[...] = jnp.full_like(m_ref, NEG_INF)
    l_ref[...] = jnp.zeros_like(l_ref)
    acc_ref[...] = jnp.zeros_like(acc_ref)


def _mla_out(acc_ref, l_ref, wuv_ref, rows):
    o_lat = (acc_ref[...] / l_ref[...]).astype(BF16)
    return jnp.concatenate([_dot(o_lat[h * rows:(h + 1) * rows], wuv_ref[h]) for h in range(MLA_HEADS)], axis=-1)


def _mla_prompt_kernel(qi_ref, ki_ref, qt_ref, k_ref, ct_ref, wuvt_ref, o_ref, qs_ref, m_ref, l_ref, acc_ref,
                       *, tq, tk):
    j = pl.program_id(1)
    qi, ki = qi_ref[j], ki_ref[j]
    last = ((qi + 1) * tq - 1) // tk

    @pl.when(ki == 0)
    def _():
        qt = qt_ref[...]
        qs_ref[...] = jnp.concatenate([qt[h * MLA_QK:(h + 1) * MLA_QK] for h in range(MLA_HEADS)], axis=1)
        _softmax_init(m_ref, l_ref, acc_ref)

    def step(masked):
        heads = range(MLA_HEADS)
        cs = [slice(h * tq, (h + 1) * tq) for h in heads]
        k, ct = k_ref[...], ct_ref[...]
        st = [_dot(k, qs_ref[:, cs[h]]) for h in heads]
        if masked:
            keep = (ki * tk + lax.broadcasted_iota(jnp.int32, (tk, tq), 0)
                    <= qi * tq + lax.broadcasted_iota(jnp.int32, (tk, tq), 1))
            st = [jnp.where(keep, st[h], NEG_INF) for h in heads]
        m_prev = [m_ref[:, cs[h]] for h in heads]
        m_new = [jnp.maximum(m_prev[h], jnp.max(st[h], axis=0, keepdims=True)) for h in heads]
        alpha = [jnp.exp2(m_prev[h] - m_new[h]) for h in heads]
        p = [jnp.exp2(st[h] - m_new[h]) for h in heads]
        pv = [_dot(ct, p[h].astype(BF16)) for h in heads]
        for h in heads:
            m_ref[:, cs[h]] = m_new[h]
            l_ref[:, cs[h]] = alpha[h] * l_ref[:, cs[h]] + jnp.sum(p[h], axis=0, keepdims=True)
            acc_ref[:, cs[h]] = alpha[h] * acc_ref[:, cs[h]] + pv[h]

    crosses = (ki + 1) * tk - 1 > qi * tq

    @pl.when(crosses)
    def _():
        step(True)

    @pl.when(jnp.logical_not(crosses))
    def _():
        step(False)

    @pl.when(ki == last)
    def _():
        o_lat = (acc_ref[...] / l_ref[...]).astype(BF16)
        ot = jnp.concatenate([_dot(wuvt_ref[h], o_lat[:, h * tq:(h + 1) * tq]) for h in range(MLA_HEADS)], axis=0)
        o_ref[...] = ot.T.astype(o_ref.dtype)


def _mla_prompt(qt, k, ct, wuvt, seq):
    m = k.shape[0]
    nb = m // seq
    tq = tk = _row_tile(seq, 512)
    nq, nk = seq // tq, seq // tk
    pairs = [(qi, ki) for qi in range(nq) for ki in range(((qi + 1) * tq - 1) // tk + 1)]
    qi_tab = jnp.asarray([p[0] for p in pairs], jnp.int32)
    ki_tab = jnp.asarray([p[1] for p in pairs], jnp.int32)
    cols = MLA_HEADS * tq
    grid_spec = pltpu.PrefetchScalarGridSpec(
        num_scalar_prefetch=2,
        grid=(nb, len(pairs)),
        in_specs=[pl.BlockSpec((qt.shape[0], tq), lambda b, j, qi, ki: (0, b * nq + qi[j])),
                  pl.BlockSpec((tk, k.shape[1]), lambda b, j, qi, ki: (b * nk + ki[j], 0)),
                  pl.BlockSpec((ct.shape[0], tk), lambda b, j, qi, ki: (0, b * nk + ki[j])),
                  pl.BlockSpec(wuvt.shape, lambda b, j, qi, ki: (0, 0, 0))],
        out_specs=pl.BlockSpec((tq, MLA_HEADS * MLA_V), lambda b, j, qi, ki: (b * nq + qi[j], 0)),
        scratch_shapes=[pltpu.VMEM((MLA_QK, cols), BF16), pltpu.VMEM((1, cols), F32), pltpu.VMEM((1, cols), F32),
                        pltpu.VMEM((MLA_KV_LORA, cols), F32)],
    )
    return pl.pallas_call(
        functools.partial(_mla_prompt_kernel, tq=tq, tk=tk),
        grid_spec=grid_spec,
        out_shape=jax.ShapeDtypeStruct((m, MLA_HEADS * MLA_V), BF16),
        compiler_params=_cparams("parallel", "arbitrary"),
        name="mla_prompt",
    )(qi_tab, ki_tab, qt, k, ct, wuvt)


def _mla_sample_kernel(pt_ref, q_ref, kn_ref, wuv_ref, ckv_hbm, kpet_hbm, o_ref,
                       cbuf, pbuf, sem, qs_ref, m_ref, l_ref, acc_ref, *, layer, n_pages, gp, ring, valid):
    b = pl.program_id(0)
    nseq = pl.num_programs(0)
    total = n_pages // gp
    look = ring - 1
    sp = q_ref.shape[0]

    def page_copies(page, slot, j):
        return (pltpu.make_async_copy(ckv_hbm.at[layer, page], cbuf.at[slot * gp + j], sem.at[0, slot]),
                pltpu.make_async_copy(kpet_hbm.at[layer, page], pbuf.at[slot * gp + j], sem.at[1, slot]))

    def start_group(seq, gi):
        for j in range(gp):
            for cp in page_copies(pt_ref[seq * n_pages + gi * gp + j], gi % ring, j):
                cp.start()

    def wait_group(gi):
        for j in range(gp):
            for cp in page_copies(0, gi % ring, j):
                cp.wait()

    @pl.when(b == 0)
    def _():
        for gi in range(look):
            start_group(b, gi)

    qs_ref[...] = _stack_heads(q_ref[...], MLA_QK)
    _softmax_init(m_ref, l_ref, acc_ref)
    qs = qs_ref[...]
    qc, qp = qs[:, :MLA_KV_LORA], qs[:, MLA_KV_LORA:MLA_KV_LORA + MLA_ROPE]
    nsub = math.gcd(gp, 4)
    per = gp // nsub
    sub = range(nsub)

    for gi in range(total):
        ahead = gi + look
        if ahead < total:
            start_group(b, ahead)
        else:
            @pl.when(b + 1 < nseq)
            def _():
                start_group(b + 1, ahead - total)
        wait_group(gi)
        base = (gi % ring) * gp
        ckv = [jnp.concatenate([cbuf[base + u * per + j].astype(BF16) for j in range(per)], axis=0) for u in sub]
        kpet = [jnp.concatenate([pbuf[base + u * per + j].astype(BF16) for j in range(per)], axis=1) for u in sub]
        s = [_dot_nt(qc, ckv[u]) + _dot(qp, kpet[u]) for u in sub]
        mg = [jnp.max(s[u], axis=-1, keepdims=True) for u in sub]
        p = [jnp.exp2(s[u] - mg[u]) for u in sub]
        lg = [jnp.sum(p[u], axis=-1, keepdims=True) for u in sub]
        pv = [_dot(p[u].astype(BF16), ckv[u]) for u in sub]
        m_prev = m_ref[...]
        m_new = functools.reduce(jnp.maximum, mg, m_prev)
        alpha = jnp.exp2(m_prev - m_new)
        w = [jnp.exp2(mg[u] - m_new) for u in sub]
        l_ref[...] = alpha * l_ref[...] + sum(w[u] * lg[u] for u in sub)
        acc_ref[...] = alpha * acc_ref[...] + sum(w[u] * pv[u] for u in sub)
        m_ref[...] = m_new

    kn = kn_ref[...]
    s = _dot_nt(qs, kn)
    rows = MLA_HEADS * sp
    qpos = lax.broadcasted_iota(jnp.int32, (rows, sp), 0) % sp
    kpos = lax.broadcasted_iota(jnp.int32, (rows, sp), 1)
    s = jnp.where(jnp.logical_and(kpos <= qpos, kpos < valid), s, NEG_INF)
    _softmax_update(s, kn[:, :MLA_KV_LORA], m_ref, l_ref, acc_ref)
    o_ref[...] = _mla_out(acc_ref, l_ref, wuv_ref, sp).astype(o_ref.dtype)


def _mla_sample(q, kn, wuv, cache_ckv, cache_kpet, layer, page_table, valid):
    m = q.shape[0]
    db, n_pages = page_table.shape
    sp = m // db
    gp = math.gcd(n_pages, MLA_GROUP_PAGES)
    ring = math.gcd(n_pages // gp, MLA_RING_GROUPS)
    rows = MLA_HEADS * sp
    qmap = lambda b, pt: (b, 0)
    grid_spec = pltpu.PrefetchScalarGridSpec(
        num_scalar_prefetch=1,
        grid=(db,),
        in_specs=[pl.BlockSpec((sp, q.shape[1]), qmap), pl.BlockSpec((sp, kn.shape[1]), qmap),
                  pl.BlockSpec(wuv.shape, lambda b, pt: (0, 0, 0)),
                  pl.BlockSpec(memory_space=pl.ANY), pl.BlockSpec(memory_space=pl.ANY)],
        out_specs=pl.BlockSpec((sp, MLA_HEADS * MLA_V), qmap),
        scratch_shapes=[pltpu.VMEM((ring * gp, PAGE_SIZE, MLA_KV_LORA), F32),
                        pltpu.VMEM((ring * gp, MLA_ROPE, PAGE_SIZE), F32),
                        pltpu.SemaphoreType.DMA((2, ring)),
                        pltpu.VMEM((rows, MLA_QK), BF16), pltpu.VMEM((rows, 1), F32), pltpu.VMEM((rows, 1), F32),
                        pltpu.VMEM((rows, MLA_KV_LORA), F32)],
    )
    return pl.pallas_call(
        functools.partial(_mla_sample_kernel, layer=layer, n_pages=n_pages, gp=gp, ring=ring, valid=valid),
        grid_spec=grid_spec,
        out_shape=jax.ShapeDtypeStruct((m, MLA_HEADS * MLA_V), BF16),
        compiler_params=_cparams("arbitrary"),
        name="mla_sample",
    )(page_table.reshape(-1), q, kn, wuv, cache_ckv, cache_kpet)


def _moba_prep_kernel(q_ref, k_ref, v_ref, cos_ref, sin_ref, qr_ref, kr_ref, *rest, nblk):
    cos, sin = cos_ref[...], sin_ref[...]
    qr_ref[...] = _rope(q_ref[...], cos, sin, MOBA_HD)
    kr = _rope(k_ref[...], cos, sin, MOBA_HD)
    if nblk:
        vo_ref, kb_ref, vt_ref, km_ref = rest
        v = v_ref[...]
        tm = v.shape[0]
        for h in range(MOBA_HEADS):
            hs = slice(h * MOBA_HD, (h + 1) * MOBA_HD)
            kr_ref[pl.ds(h, tm, stride=MOBA_HEADS), :] = kr[:, hs]
            vo_ref[pl.ds(h, tm, stride=MOBA_HEADS), :] = v[:, hs]
        kb_ref[...] = kr.astype(BF16)
        vt_ref[...] = v.T.astype(BF16)
        for j in range(nblk):
            km_ref[j] = jnp.mean(kr[j * MOBA_BLOCK:(j + 1) * MOBA_BLOCK], axis=0, keepdims=True)
    else:
        kr_ref[...] = kr


def _moba_prep(q, k, v, cos, sin, prompt):
    m, w = q.shape
    p = cos.shape[0]
    tm = _row_tile(math.gcd(m, p), 512)
    npos = p // tm
    if prompt:
        assert tm % MOBA_BLOCK == 0
    nblk = tm // MOBA_BLOCK if prompt else 0
    row = pl.BlockSpec((tm, w), lambda i: (i, 0))
    pos = pl.BlockSpec((tm, w), lambda i: (i % npos, 0))
    out_specs = [row, row]
    out_shape = [jax.ShapeDtypeStruct((m, w), F32), jax.ShapeDtypeStruct((m, w), F32)]
    if prompt:
        pairs = pl.BlockSpec((tm * MOBA_HEADS, MOBA_HD), lambda i: (i, 0))
        pairs_shape = jax.ShapeDtypeStruct((m * MOBA_HEADS, MOBA_HD), F32)
        out_specs = [row, pairs, pairs, row, pl.BlockSpec((w, tm), lambda i: (0, i)),
                     pl.BlockSpec((nblk, 1, w), lambda i: (i, 0, 0))]
        out_shape = [out_shape[0], pairs_shape, pairs_shape, jax.ShapeDtypeStruct((m, w), BF16),
                     jax.ShapeDtypeStruct((w, m), BF16), jax.ShapeDtypeStruct((m // MOBA_BLOCK, 1, w), F32)]
    return pl.pallas_call(
        functools.partial(_moba_prep_kernel, nblk=nblk),
        grid=(m // tm,),
        in_specs=[row, row, row, pos, pos],
        out_specs=out_specs,
        out_shape=out_shape,
        compiler_params=_cparams("parallel"),
        name="moba_prep",
    )(q, k, v, cos, sin)


def _top_rows(gate, idx, count, limit, axis):
    sel = jnp.zeros(gate.shape, F32)
    for _ in range(count):
        mx = jnp.max(gate, axis=axis, keepdims=True)
        first = jnp.min(jnp.where(gate == mx, idx, limit), axis=axis, keepdims=True)
        pick = idx == first
        sel = jnp.where(pick, 1.0, sel)
        gate = jnp.where(pick, BELOW_NEG_INF, gate)
    return sel


def _moba_prompt_kernel(q_ref, k_ref, vt_ref, km_ref, o_ref, qt_ref, sel_ref, m_ref, l_ref, acc_ref, *, nb, topn):
    qi = pl.program_id(1)
    blk, hd = MOBA_BLOCK, MOBA_HD
    b_idx = lax.broadcasted_iota(jnp.int32, (nb, blk), 0)
    past = b_idx < qi
    causal = lax.broadcasted_iota(jnp.int32, (blk, blk), 0) <= lax.broadcasted_iota(jnp.int32, (blk, blk), 1)
    r0 = pl.multiple_of(qi * blk, blk)

    heads = range(MOBA_HEADS)
    hsl = [slice(h * hd, (h + 1) * hd) for h in heads]
    q = [q_ref[:, hsl[h]] for h in heads]
    qt = [(q[h] * (hd ** -0.5 * LOG2E)).T.astype(BF16) for h in heads]
    st = [jnp.where(causal, _dot(k_ref[pl.ds(r0, blk), hsl[h]], qt[h]), NEG_INF) for h in heads]
    m0 = [jnp.max(st[h], axis=0, keepdims=True) for h in heads]
    p0 = [jnp.exp2(st[h] - m0[h]) for h in heads]
    pv0 = [_dot(vt_ref[hsl[h], pl.ds(r0, blk)], p0[h].astype(BF16)) for h in heads]
    gate = [jnp.where(past, _dot_nt_f32(km_ref[:, hsl[h]], q[h]), NEG_INF) for h in heads]
    for h in heads:
        qt_ref[h] = qt[h]
        m_ref[h] = m0[h]
        l_ref[h] = jnp.sum(p0[h], axis=0, keepdims=True)
        acc_ref[h] = pv0[h]
        sel_ref[h] = jnp.where(past, _top_rows(gate[h], b_idx, topn, nb, 0), 0.0)

    def past_blocks(n, count):
        rn = pl.multiple_of(n * blk, blk)
        keys = count * blk
        s = [_dot(k_ref[pl.ds(rn, keys), hsl[h]], qt_ref[h]) for h in heads]
        s = [jnp.concatenate([jnp.where(sel_ref[h, pl.ds(n + j, 1), :] > 0.0, s[h][j * blk:(j + 1) * blk], NEG_INF)
                              for j in range(count)], axis=0) for h in heads]
        m_old = [m_ref[h] for h in heads]
        m_new = [jnp.maximum(m_old[h], jnp.max(s[h], axis=0, keepdims=True)) for h in heads]
        alpha = [jnp.exp2(m_old[h] - m_new[h]) for h in heads]
        p = [jnp.exp2(s[h] - m_new[h]) for h in heads]
        pv = [_dot(vt_ref[hsl[h], pl.ds(rn, keys)], p[h].astype(BF16)) for h in heads]
        for h in heads:
            m_ref[h] = m_new[h]
            l_ref[h] = alpha[h] * l_ref[h] + jnp.sum(p[h], axis=0, keepdims=True)
            acc_ref[h] = alpha[h] * acc_ref[h] + pv[h]

    def pair_body(i, carry):
        past_blocks(2 * i, 2)
        return carry

    lax.fori_loop(0, qi // 2, pair_body, 0)

    @pl.when(qi % 2 == 1)
    def _():
        past_blocks(qi - 1, 1)
    o_ref[...] = jnp.concatenate([(acc_ref[h] / l_ref[h]).T for h in range(MOBA_HEADS)], axis=1).astype(o_ref.dtype)


def _moba_prompt(qr, kb, vt, km, seq):
    m, hw = qr.shape
    nbatch = m // seq
    assert seq % MOBA_BLOCK == 0
    nb = seq // MOBA_BLOCK
    topn = min(MOBA_TOPK, nb)
    qmap = lambda b, qi: (b * nb + qi, 0)
    once = pl.Buffered(1)
    return pl.pallas_call(
        functools.partial(_moba_prompt_kernel, nb=nb, topn=topn),
        grid=(nbatch, nb),
        in_specs=[pl.BlockSpec((MOBA_BLOCK, hw), qmap),
                  pl.BlockSpec((seq, hw), lambda b, qi: (b, 0), pipeline_mode=once),
                  pl.BlockSpec((hw, seq), lambda b, qi: (0, b), pipeline_mode=once),
                  pl.BlockSpec((None, nb, hw), lambda b, qi: (b, 0, 0))],
        out_specs=pl.BlockSpec((MOBA_BLOCK, hw), qmap),
        out_shape=jax.ShapeDtypeStruct((m, hw), BF16),
        scratch_shapes=[pltpu.VMEM((MOBA_HEADS, MOBA_HD, MOBA_BLOCK), BF16),
                        pltpu.VMEM((MOBA_HEADS, nb, MOBA_BLOCK), F32),
                        pltpu.VMEM((MOBA_HEADS, 1, MOBA_BLOCK), F32),
                        pltpu.VMEM((MOBA_HEADS, 1, MOBA_BLOCK), F32),
                        pltpu.VMEM((MOBA_HEADS, MOBA_HD, MOBA_BLOCK), F32)],
        compiler_params=_cparams("parallel", "arbitrary"),
        name="moba_prompt",
    )(qr, kb, vt, km)


def _moba_sample_kernel(pt_ref, q_ref, kn_ref, vn_ref, ck_hbm, cv_hbm, o_ref,
                        buf, sem, qrow_ref, s_ref, p_ref, km_ref, acc_ref, pown_ref, l_ref,
                        *, layer, n_pages, nb, topn, valid):
    b = pl.program_id(0)
    nseq = pl.num_programs(0)
    gp, ring = MOBA_GROUP_PAGES, MOBA_RING_GROUPS
    ngk = n_pages // gp
    total = 2 * ngk
    look = ring - 1
    sp = q_ref.shape[0]
    hw = MOBA_HEADS * MOBA_HD
    ppb = MOBA_BLOCK // PAGE_SIZE
    nbs = gp // ppb
    scale = MOBA_HD ** -0.5

    def page_copy(src, page, slot, j):
        return pltpu.make_async_copy(src.at[layer, page], buf.at[slot * gp + j], sem.at[slot])

    def start_group(seq, gi):
        src = ck_hbm if gi < ngk else cv_hbm
        for j in range(gp):
            page_copy(src, pt_ref[seq * n_pages + (gi % ngk) * gp + j], gi % ring, j).start()

    def wait_group(gi):
        for j in range(gp):
            page_copy(ck_hbm, 0, gi % ring, j).wait()

    def query_rows():
        qt = jnp.concatenate([q_ref[...]] * (LANES // sp), axis=0)
        rh = lax.broadcasted_iota(jnp.int32, (LANES, hw), 0) // sp
        lh = lax.broadcasted_iota(jnp.int32, (LANES, hw), 1) // MOBA_HD
        return jnp.where(rh == lh, qt, 0.0)

    def page_heads(slot, j):
        return [buf[slot * gp + j, pl.ds(h, PAGE_SIZE, stride=MOBA_HEADS), :] for h in range(MOBA_HEADS)]

    def key_group(gk):
        slot = gk % ring
        sums, kall = [], []
        for j in range(gp):
            xh = page_heads(slot, j)
            kall.append(jnp.concatenate([t.astype(BF16) for t in xh], axis=1))
            sums.append(jnp.concatenate([jnp.sum(t, axis=0, keepdims=True) for t in xh], axis=1))
        keys = gp * PAGE_SIZE
        s_ref[gk * keys:(gk + 1) * keys, :] = _dot_nt(jnp.concatenate(kall, axis=0), qrow_ref[...])
        for jb in range(nbs):
            blk = gk * nbs + jb
            km_ref[blk:blk + 1, :] = sum(sums[jb * ppb:(jb + 1) * ppb]) * (1.0 / MOBA_BLOCK)

    def value_group(gv):
        slot = (ngk + gv) % ring
        keys = gp * PAGE_SIZE
        vall = jnp.concatenate(
            [jnp.concatenate([t.astype(BF16) for t in page_heads(slot, j)], axis=1) for j in range(gp)], axis=0)
        acc_ref[...] += _dot_tn(p_ref[gv * keys:(gv + 1) * keys, :], vall)

    def select_and_weigh():
        b_idx = lax.broadcasted_iota(jnp.int32, (nb, LANES), 0)
        sel = _top_rows(_dot_nt_f32(km_ref[...], query_rows()), b_idx, topn, nb, 0)
        s_own = _dot_nt(kn_ref[...].astype(BF16), qrow_ref[...])
        kt = lax.broadcasted_iota(jnp.int32, (sp, LANES), 0)
        qt = lax.broadcasted_iota(jnp.int32, (sp, LANES), 1) % sp
        s_own = jnp.where(jnp.logical_and(kt <= qt, kt < valid), s_own, NEG_INF)

        def masked(n):
            return jnp.where(sel[n:n + 1, :] > 0.0, s_ref[n * MOBA_BLOCK:(n + 1) * MOBA_BLOCK, :], NEG_INF)

        mx = jnp.max(s_own, axis=0, keepdims=True)
        for n in range(nb):
            mx = jnp.maximum(mx, jnp.max(masked(n), axis=0, keepdims=True))
        p_own = jnp.exp(s_own - mx)
        l = jnp.sum(p_own, axis=0, keepdims=True)
        for n in range(nb):
            pn = jnp.exp(masked(n) - mx)
            l = l + jnp.sum(pn, axis=0, keepdims=True)
            p_ref[n * MOBA_BLOCK:(n + 1) * MOBA_BLOCK, :] = pn.astype(BF16)
        pown_ref[...] = jnp.concatenate([p_own, jnp.zeros((LANES - sp, LANES), F32)], axis=0).astype(BF16)
        l_ref[...] = jnp.broadcast_to(l, l_ref.shape)

    def finish():
        vn = jnp.concatenate([vn_ref[...], jnp.zeros((LANES - sp, hw), F32)], axis=0).astype(BF16)
        acc = acc_ref[...] + _dot_tn(pown_ref[...], vn)
        o = acc / l_ref[...].T[:, :1]
        o_ref[...] = jnp.concatenate(
            [o[h * sp:(h + 1) * sp, h * MOBA_HD:(h + 1) * MOBA_HD] for h in range(MOBA_HEADS)], axis=1
        ).astype(o_ref.dtype)

    @pl.when(b == 0)
    def _():
        for gi in range(look):
            start_group(b, gi)

    qrow_ref[...] = (query_rows() * scale).astype(BF16)
    acc_ref[...] = jnp.zeros_like(acc_ref)

    for gi in range(total):
        ahead = gi + look
        if ahead < total:
            start_group(b, ahead)
        else:
            @pl.when(b + 1 < nseq)
            def _():
                start_group(b + 1, ahead - total)
        wait_group(gi)
        if gi < ngk:
            key_group(gi)
            if gi == ngk - 1:
                select_and_weigh()
        else:
            value_group(gi - ngk)
    finish()


def _moba_sample(qr, kr, v, cache_k, cache_v, layer, page_table, valid):
    m, hw = qr.shape
    cache_k = cache_k.reshape(cache_k.shape[:2] + (PAGE_SIZE * MOBA_HEADS, MOBA_HD))
    cache_v = cache_v.reshape(cache_v.shape[:2] + (PAGE_SIZE * MOBA_HEADS, MOBA_HD))
    db, n_pages = page_table.shape
    sp = m // db
    past = n_pages * PAGE_SIZE
    assert past % MOBA_BLOCK == 0
    nb = past // MOBA_BLOCK
    topn = min(MOBA_TOPK, nb)
    ppb = MOBA_BLOCK // PAGE_SIZE
    gp, ring = MOBA_GROUP_PAGES, MOBA_RING_GROUPS
    assert n_pages % gp == 0 and gp % ppb == 0 and (2 * n_pages // gp) % ring == 0 and nb >= 1
    qmap = lambda b, pt: (b, 0)
    grid_spec = pltpu.PrefetchScalarGridSpec(
        num_scalar_prefetch=1,
        grid=(db,),
        in_specs=[pl.BlockSpec((sp, hw), qmap)] * 3 + [pl.BlockSpec(memory_space=pl.ANY)] * 2,
        out_specs=pl.BlockSpec((sp, hw), qmap),
        scratch_shapes=[pltpu.VMEM((ring * gp, PAGE_SIZE * MOBA_HEADS, MOBA_HD), F32), pltpu.SemaphoreType.DMA((ring,)),
                        pltpu.VMEM((LANES, hw), BF16), pltpu.VMEM((past, LANES), F32), pltpu.VMEM((past, LANES), BF16),
                        pltpu.VMEM((nb, hw), F32), pltpu.VMEM((LANES, hw), F32), pltpu.VMEM((LANES, LANES), BF16),
                        pltpu.VMEM((SUBLANES, LANES), F32)],
    )
    return pl.pallas_call(
        functools.partial(_moba_sample_kernel, layer=layer, n_pages=n_pages, nb=nb, topn=topn, valid=valid),
        grid_spec=grid_spec,
        out_shape=jax.ShapeDtypeStruct((m, hw), BF16),
        compiler_params=_cparams("arbitrary"),
        name="moba_sample",
    )(page_table.reshape(-1), qr, kr, v, cache_k, cache_v)


def _s5_disc_kernel(lr_ref, li_ref, dt_ref, bre_ref, bim_ref, pwr_ref, pwi_ref, bbr_ref, bbi_ref):
    lr, li, dt = lr_ref[...], li_ref[...], dt_ref[...]
    k = (lax.broadcasted_iota(jnp.int32, pwr_ref.shape, 0) + 1).astype(F32)
    mag = jnp.exp(k * (lr * dt))
    ang = k * (li * dt)
    pwr = mag * jnp.cos(ang)
    pwi = mag * jnp.sin(ang)
    pwr_ref[...] = pwr
    pwi_ref[...] = pwi
    nr, ni = pwr[0:1] - 1.0, pwi[0:1]
    den = lr * lr + li * li
    f_re = (nr * lr + ni * li) / den
    f_im = (ni * lr - nr * li) / den
    bre, bim = bre_ref[...], bim_ref[...]
    bbr_ref[...] = f_re * bre - f_im * bim
    bbi_ref[...] = f_re * bim + f_im * bre


def _s5_discretize(lam_re, lam_im, log_dt, b_re, b_im):
    n = S5_N
    flat = lambda a: a.astype(F32).reshape(1, n)
    dt = jnp.exp(jnp.repeat(log_dt.astype(F32), S5_STATE)).reshape(1, n)
    bt = lambda b: jnp.transpose(b.astype(F32), (2, 0, 1)).reshape(S5_GROUP, n)
    args = (flat(lam_re), flat(lam_im), dt, bt(b_re), bt(b_im))
    return pl.pallas_call(
        _s5_disc_kernel,
        in_specs=[_full(a) for a in args],
        out_shape=[jax.ShapeDtypeStruct((S5_TILE // SUBLANES, n), F32)] * 2
                  + [jax.ShapeDtypeStruct((S5_GROUP, n), F32)] * 2,
        name="s5_discretize",
    )(*args)


def _s5_block_diag(pieces):
    r, g, c = pieces.shape
    eye = jnp.eye(g, dtype=pieces.dtype)
    return jnp.einsum('rgc,gh->grhc', pieces, eye).reshape(g * r, g * c)


def _s5_sample_kernel(u_ref, bb_ref, pwr_ref, pwi_ref, cc_ref, d_ref, wg_ref, bg_ref, s0r_ref, s0i_ref,
                      o_ref, xr_ref, xi_ref):
    n = S5_N
    u = u_ref[...]
    tt = u.shape[0]
    ngrp = tt // SUBLANES
    bu = _dot(u.astype(BF16), bb_ref[...])
    xr = bu[:, :n].reshape(ngrp, SUBLANES, n)
    xi = bu[:, n:].reshape(ngrp, SUBLANES, n)
    row = lax.broadcasted_iota(jnp.int32, (ngrp, SUBLANES, n), 1)
    for d in (1, 2, 4):
        pr, pi = pwr_ref[d - 1:d, :], pwi_ref[d - 1:d, :]
        sr, si = pltpu.roll(xr, d, 1), pltpu.roll(xi, d, 1)
        keep = row >= d
        xr, xi = (xr + jnp.where(keep, pr * sr - pi * si, 0.0),
                  xi + jnp.where(keep, pr * si + pi * sr, 0.0))
    pw_r, pw_i = pwr_ref[0:SUBLANES, :], pwi_ref[0:SUBLANES, :]
    s0r = s0r_ref[...].reshape(ngrp, SUBLANES, n)
    s0i = s0i_ref[...].reshape(ngrp, SUBLANES, n)
    xr, xi = xr + pw_r * s0r - pw_i * s0i, xi + pw_r * s0i + pw_i * s0r
    xr_ref[...] = xr.reshape(tt, n)
    xi_ref[...] = xi.reshape(tt, n)

    xcat = jnp.concatenate([xr.reshape(tt, n), xi.reshape(tt, n)], axis=1).astype(BF16)
    y = _dot(xcat, cc_ref[...]) + d_ref[...] * u
    y = y * (0.5 * (1.0 + jnp.tanh(math.sqrt(2.0 / math.pi) * (y + 0.044715 * (y * y * y)))))
    z = _dot(y.astype(BF16), wg_ref[...]) + bg_ref[...]
    o_ref[...] = (y * _sigmoid(z)).astype(o_ref.dtype)


def _s5_prompt_kernel(*refs, nu):
    u_refs = refs[:nu]
    bb_ref, pwr_ref, pwi_ref, cc_ref, d_ref, wg_ref, bg_ref, o_ref, sr_ref, si_ref = refs[nu:nu + 10]
    xr_s, xi_s, o_s, cr_s, ci_s = refs[nu + 10:]
    n = S5_N
    tt = u_refs[0].shape[0]
    cl = tt // SUBLANES
    t = pl.program_id(1)

    @pl.when(t == 0)
    def _():
        cr_s[...] = jnp.zeros_like(cr_s)
        ci_s[...] = jnp.zeros_like(ci_s)

    u = jnp.concatenate(
        [jnp.concatenate([r[pl.ds(p, SUBLANES, stride=cl), :] for r in u_refs], axis=1) for p in range(cl)], axis=0)
    bu = _dot(u.astype(BF16), bb_ref[...])
    lr, li = pwr_ref[0:1, :], pwi_ref[0:1, :]
    xr, xi = bu[0:SUBLANES, :n], bu[0:SUBLANES, n:]
    xr_s[0] = xr
    xi_s[0] = xi
    for p in range(1, cl):
        rows = slice(p * SUBLANES, (p + 1) * SUBLANES)
        xr, xi = lr * xr - li * xi + bu[rows, :n], lr * xi + li * xr + bu[rows, n:]
        xr_s[p] = xr
        xi_s[p] = xi

    full_r, full_i = pwr_ref[cl - 1:cl, :], pwi_ref[cl - 1:cl, :]
    cr, ci = cr_s[...], ci_s[...]
    in_r, in_i = [], []
    for c in range(SUBLANES):
        in_r.append(cr)
        in_i.append(ci)
        cr, ci = xr[c:c + 1] + full_r * cr - full_i * ci, xi[c:c + 1] + full_r * ci + full_i * cr
    cr_s[...] = cr
    ci_s[...] = ci
    in_r = jnp.concatenate(in_r, axis=0)
    in_i = jnp.concatenate(in_i, axis=0)
    for p in range(cl):
        pr, pi = pwr_ref[p:p + 1, :], pwi_ref[p:p + 1, :]
        xr_s[p] = xr_s[p] + pr * in_r - pi * in_i
        xi_s[p] = xi_s[p] + pr * in_i + pi * in_r

    @pl.when(t == pl.num_programs(1) - 1)
    def _():
        sr_ref[...] = cr
        si_ref[...] = ci

    xcat = jnp.concatenate([xr_s[...].reshape(tt, n), xi_s[...].reshape(tt, n)], axis=1).astype(BF16)
    y = _dot(xcat, cc_ref[...]) + d_ref[...] * u
    y = y * (0.5 * (1.0 + jnp.tanh(math.sqrt(2.0 / math.pi) * (y + 0.044715 * (y * y * y)))))
    z = _dot(y.astype(BF16), wg_ref[...]) + bg_ref[...]
    o = y * _sigmoid(z)
    for k in range(nu):
        o_s[k] = o[:, k * LANES:(k + 1) * LANES]
    for c in range(SUBLANES):
        rows = [o_s[k, pl.ds(c, cl, stride=SUBLANES), :] for k in range(nu)]
        o_ref[c * cl:(c + 1) * cl, :] = jnp.concatenate(rows, axis=1).astype(o_ref.dtype)


def _s5_prompt(u_slabs, bb, pwr, pwi, cc, d, wg, bg, seq):
    nu = len(u_slabs)
    m = u_slabs[0].shape[0]
    w = nu * LANES
    nbatch = m // seq
    tt = S5_TILE
    assert seq % tt == 0 and pwr.shape[0] == tt // SUBLANES
    nt = seq // tt
    n = S5_N
    slab = pl.BlockSpec((tt, LANES), lambda b, t: (b * nt + t, 0))
    row = pl.BlockSpec((tt, w), lambda b, t: (b * nt + t, 0))
    st = pl.BlockSpec((None, 1, n), lambda b, t: (b, 0, 0))
    return pl.pallas_call(
        functools.partial(_s5_prompt_kernel, nu=nu),
        grid=(nbatch, nt),
        in_specs=[slab] * nu + [_full(bb), _full(pwr), _full(pwi), _full(cc), _full(d), _full(wg), _full(bg)],
        out_specs=[row, st, st],
        out_shape=[jax.ShapeDtypeStruct((m, w), BF16), jax.ShapeDtypeStruct((nbatch, 1, n), F32),
                   jax.ShapeDtypeStruct((nbatch, 1, n), F32)],
        scratch_shapes=[pltpu.VMEM((tt // SUBLANES, SUBLANES, n), F32), pltpu.VMEM((tt // SUBLANES, SUBLANES, n), F32),
                        pltpu.VMEM((nu, tt, LANES), F32), pltpu.VMEM((1, n), F32), pltpu.VMEM((1, n), F32)],
        compiler_params=_cparams("parallel", "arbitrary"),
        name="s5_prompt",
    )(*u_slabs, bb, pwr, pwi, cc, d, wg, bg)


def _s5_sample(u, bb, pwr, pwi, cc, d, wg, bg, s0r, s0i):
    m, w = u.shape
    n = S5_N
    tt = _row_tile(m, 256)
    row = pl.BlockSpec((tt, w), lambda i: (i, 0))
    st = pl.BlockSpec((tt, n), lambda i: (i, 0))
    return pl.pallas_call(
        _s5_sample_kernel,
        grid=(m // tt,),
        in_specs=[row, _full(bb), _full(pwr), _full(pwi), _full(cc), _full(d), _full(wg), _full(bg), st, st],
        out_specs=[row, st, st],
        out_shape=[jax.ShapeDtypeStruct((m, w), BF16), jax.ShapeDtypeStruct((m, n), F32),
                   jax.ShapeDtypeStruct((m, n), F32)],
        compiler_params=_cparams("parallel"),
        name="s5_sample",
    )(u, bb, pwr, pwi, cc, d, wg, bg, s0r, s0i)


def kernel(x_prompt, x_sample, cache_mem_k, cache_mem_v, state_gla, cache_mla_ckv, cache_mla_kpe, cache_moba_k, cache_moba_v, state_s5, page_table, mem_prompt, g_ffn1, w_ffn1_gate, w_ffn1_up, w_ffn1_down, g_mix, w_in_a, w_gla_gate, b_gla_gate, g_gla_norm, g_mla_q, w_mla_qb, g_mla_kv, w_mla_kvb, w_out_a, w_in_b, s5_lam_re, s5_lam_im, s5_log_dt, s5_b_re, s5_b_im, s5_c_re, s5_c_im, s5_d, w_s5_glu, b_s5_glu, w_out_b, g_xattn, w_xq, w_xk, w_xv, w_xo, g_ffn2, w_ffn2_gate, w_ffn2_up, w_ffn2_down, g_final):
    B, T, D = x_prompt.shape
    DB, S, _ = x_sample.shape
    depth = g_ffn1.shape[0]
    n_pages = page_table.shape[1]
    past = n_pages * PAGE_SIZE
    SP = SAMPLE_PAD
    assert S <= SP
    n_mem = mem_prompt.shape[1]
    bf = lambda a: a.astype(BF16)

    xp = x_prompt.reshape(B * T, D)
    xs = jnp.pad(x_sample, ((0, 0), (0, SP - S), (0, 0))).reshape(DB * SP, D)
    pos_p = jnp.arange(T, dtype=jnp.int32)
    pos_s = jnp.tile(past + jnp.arange(SP, dtype=jnp.int32), DB)
    mla_tab_p = _rope_tables(pos_p, MLA_ROPE, LANES, MLA_HEADS)
    mla_tab_s = _rope_tables(pos_s, MLA_ROPE, LANES, MLA_HEADS)
    moba_tab_p = _rope_tables(pos_p, MOBA_HD, MOBA_HD, MOBA_HEADS)
    moba_tab_s = _rope_tables(pos_s, MOBA_HD, MOBA_HD, MOBA_HEADS)
    mem2d = mem_prompt.reshape(B * n_mem, D)
    cache_kpet = jnp.swapaxes(cache_mla_kpe, 2, 3)
    ones_d = jnp.ones((D,), F32)
    gla_chunk = math.gcd(T, GLA_CHUNK)

    def unpad(a):
        return a.reshape((DB, SP) + a.shape[1:])[:, :S]

    mem_k_p, mem_v_p = [], []
    gla_p, gla_s = [], []
    ckv_p, kpe_p, ckv_s, kpe_s = [], [], [], []
    mk_p, mv_p, mk_s, mv_s = [], [], [], []
    s5_p, s5_s = [], []

    for l in range(depth):
        i = l // 2
        wg1, wu1, wd1 = bf(w_ffn1_gate[l]), bf(w_ffn1_up[l]), bf(w_ffn1_down[l])
        xp = _ffn(xp, g_ffn1[l], wg1, wu1, wd1)
        xs = _ffn(xs, g_ffn1[l], wg1, wu1, wd1)

        if l % 2 == 0:
            qa_w, ka_w, va_w, ga_w, ra_w, ql_w, kvl_w, kr_w = jnp.split(
                w_in_a[i], [256, 512, 1024, 1040, 1552, 1936, 2192], axis=1)
            padc = lambda w: jnp.pad(w, ((0, 0), (0, LANES - w.shape[1])))
            w_in = bf(jnp.concatenate([va_w, ra_w, qa_w, ka_w, kvl_w, ql_w, padc(kr_w), padc(ga_w)], axis=1))
            widths = (512, 512, 256, 256, 256, 384, LANES, LANES)
            wgate = bf(jnp.pad(w_gla_gate[i], ((0, LANES - GLA_GATE_RANK), (0, 0))))
            bgate = b_gla_gate[i].reshape(1, -1)
            gnorm = g_gla_norm[i].reshape(1, -1)
            wqb = w_mla_qb[i].reshape(MLA_Q_LORA, MLA_HEADS, MLA_NOPE + MLA_ROPE)
            wqn = bf(wqb[:, :, :MLA_NOPE].reshape(MLA_Q_LORA, MLA_HEADS * MLA_NOPE))
            wqp = bf(jnp.pad(wqb[:, :, MLA_NOPE:], ((0, 0), (0, 0), (0, LANES - MLA_ROPE))).reshape(MLA_Q_LORA, MLA_HEADS * LANES))
            wkvb = w_mla_kvb[i].reshape(MLA_KV_LORA, MLA_HEADS, MLA_NOPE + MLA_V)
            wuk = bf(jnp.transpose(wkvb[:, :, :MLA_NOPE], (1, 2, 0)))
            wuv = bf(jnp.transpose(wkvb[:, :, MLA_NOPE:], (1, 0, 2)))
            wuvt = bf(jnp.transpose(wkvb[:, :, MLA_NOPE:], (1, 2, 0)))
            gq = g_mla_q[i].reshape(1, -1)
            gkv = g_mla_kv[i].reshape(1, -1)
            wo = bf(w_out_a[i])
            wo_a, wo_b = wo[:GLA_HEADS * GLA_DV], wo[GLA_HEADS * GLA_DV:]

            def even(x, tabs, s0, prompt):
                va, ra, qa, ka, kvl, ql, ksl, gsl = _proj(x, g_mix[l], w_in, widths)
                if prompt:
                    o_a, st = _gla(qa, ka, va, gsl, ra, s0, wgate, bgate, gnorm,
                                   seq=T, rows=gla_chunk, chunk=gla_chunk, valid=gla_chunk)
                else:
                    o_a, st = _gla(qa, ka, va, gsl, ra, s0, wgate, bgate, gnorm,
                                   seq=SP, rows=SP, chunk=2 * SP, valid=S)
                if prompt:
                    q_t, k_mla, ckv, kpe, c_t = _mla_prep(ql, kvl, ksl, tabs[0], tabs[1], gq, gkv, wqn, wqp, wuk, True)
                    o_b = _mla_prompt(q_t, k_mla, c_t, wuvt, T)
                else:
                    q_mla, k_mla, ckv, kpe = _mla_prep(ql, kvl, ksl, tabs[0], tabs[1], gq, gkv, wqn, wqp, wuk, False)
                    o_b = _mla_sample(q_mla, k_mla, wuv, cache_mla_ckv, cache_kpet, i, page_table, S)
                return _outproj(o_a, o_b, wo_a, wo_b, x), st, ckv, kpe

            xp, st, ckv, kpe = even(xp, mla_tab_p, jnp.zeros((B, GLA_HEADS, GLA_DK, GLA_DV), F32), True)
            gla_p.append(st)
            ckv_p.append(ckv.reshape(B, T, MLA_KV_LORA))
            kpe_p.append(kpe.reshape(B, T, MLA_ROPE))
            xs, st, ckv, kpe = even(xs, mla_tab_s, state_gla[i], False)
            gla_s.append(st)
            ckv_s.append(unpad(ckv))
            kpe_s.append(unpad(kpe))
        else:
            hw = MOBA_HEADS * MOBA_HD
            w_in = bf(w_in_b[i])
            widths = (hw, hw, hw, S5_WIDTH)
            pwr, pwi, bbr, bbi = _s5_discretize(s5_lam_re[i], s5_lam_im[i], s5_log_dt[i], s5_b_re[i], s5_b_im[i])
            to_blocks = lambda a: a.reshape(a.shape[0], S5_GROUPS, S5_STATE)
            bb = bf(jnp.concatenate([_s5_block_diag(to_blocks(bbr)), _s5_block_diag(to_blocks(bbi))], axis=1))
            c_t = lambda c: jnp.transpose(c.astype(F32), (2, 0, 1))
            cc = bf(jnp.concatenate([_s5_block_diag(c_t(s5_c_re[i])), -_s5_block_diag(c_t(s5_c_im[i]))], axis=0))
            dsk = s5_d[i].reshape(1, -1)
            wglu = bf(w_s5_glu[i])
            bglu = b_s5_glu[i].reshape(1, -1)
            wo = bf(w_out_b[i])
            wo_c, wo_d = wo[:hw], wo[hw:]

            q, k, v, *u_slabs = _proj(xp, g_mix[l], w_in, (hw, hw, hw) + (LANES,) * (S5_WIDTH // LANES))
            qr, kr, v, kb, vt, km = _moba_prep(q, k, v, moba_tab_p[0], moba_tab_p[1], True)
            o_c = _moba_prompt(qr, kb, vt, km.reshape(B, T // MOBA_BLOCK, hw), T)
            o_d, sr, si = _s5_prompt(u_slabs, bb, pwr, pwi, cc, dsk, wglu, bglu, T)
            xp = _outproj(o_c, o_d, wo_c, wo_d, xp)
            mk_p.append(kr.reshape(B, T, MOBA_HEADS, MOBA_HD))
            mv_p.append(v.reshape(B, T, MOBA_HEADS, MOBA_HD))
            s5_p.append(jnp.stack([sr.reshape(B, S5_GROUPS, S5_STATE), si.reshape(B, S5_GROUPS, S5_STATE)], axis=-1))

            q, k, v, u = _proj(xs, g_mix[l], w_in, widths)
            qr, kr = _moba_prep(q, k, v, moba_tab_s[0], moba_tab_s[1], False)
            o_c = _moba_sample(qr, kr, v, cache_moba_k, cache_moba_v, i, page_table, S)
            s0 = state_s5[i].astype(F32).reshape(DB, S5_N, 2)
            s0r = jnp.repeat(s0[..., 0], SP, axis=0)
            s0i = jnp.repeat(s0[..., 1], SP, axis=0)
            o_d, xr, xi = _s5_sample(u, bb, pwr, pwi, cc, dsk, wglu, bglu, s0r, s0i)
            xs = _outproj(o_c, o_d, wo_c, wo_d, xs)
            mk_s.append(unpad(kr).reshape(DB, S, MOBA_HEADS, MOBA_HD))
            mv_s.append(unpad(v).reshape(DB, S, MOBA_HEADS, MOBA_HD))
            last = lambda a: a.reshape(DB, SP, S5_GROUPS, S5_STATE)[:, S - 1]
            s5_s.append(jnp.stack([last(xr), last(xi)], axis=-1))

        mkp, mvp = _proj(mem2d, ones_d, bf(jnp.concatenate([w_xk[l], w_xv[l]], axis=1)), (D, D), norm=False)
        mem_k_p.append(mkp.reshape(B, n_mem, X_HEADS, D // X_HEADS))
        mem_v_p.append(mvp.reshape(B, n_mem, X_HEADS, D // X_HEADS))
        wq, wo = bf(w_xq[l]), bf(w_xo[l])
        xp = _xattn_prompt(xp, g_xattn[l], wq, wo, mkp.reshape(B, n_mem, D), mvp.reshape(B, n_mem, D), T)
        xs = _xattn_sample(xs, g_xattn[l], wq, wo, cache_mem_k, cache_mem_v, l)

        wg2, wu2, wd2 = bf(w_ffn2_gate[l]), bf(w_ffn2_up[l]), bf(w_ffn2_down[l])
        xp = _ffn(xp, g_ffn2[l], wg2, wu2, wd2)
        xs = _ffn(xs, g_ffn2[l], wg2, wu2, wd2)

    y_prompt = _final_norm(xp, g_final).reshape(B, T, D)
    y_sample = unpad(_final_norm(xs, g_final))
    return (y_prompt, y_sample,
            jnp.stack(mem_k_p), jnp.stack(mem_v_p),
            jnp.stack(gla_p), jnp.stack(gla_s),
            jnp.stack(ckv_p), jnp.stack(kpe_p), jnp.stack(ckv_s), jnp.stack(kpe_s),
            jnp.stack(mk_p), jnp.stack(mv_p), jnp.stack(mk_s), jnp.stack(mv_s),
            jnp.stack(s5_p), jnp.stack(s5_s))
```
